```python
import math
import jax, jax.numpy as jnp
from jax import lax
import numpy as np

D_MODEL = 1024
BATCH = 4
SEQ = 8192
DEPTH = 2

MEM_LEN = 256
MLA_HEADS = 6
MLA_NOPE = 64
MLA_ROPE = 32
MLA_V = 64
MLA_Q_RANK = 256
MLA_KV_RANK = 128
ROPE_THETA = 10000.0
DIFF_HEADS = 6
DIFF_HD = 32
MOBA_HEADS = 4
MOBA_HD = 64
MOBA_BLOCK = 256
MOBA_TOPK = 3
MOBA_Q_CHUNK = 32
Q_BLOCK = 128
CROSS_HEADS = 4
CROSS_HD = D_MODEL // CROSS_HEADS
D_FF = 4 * D_MODEL
EPS = 1e-6
NEG = -1e30
N_ALIBI = DIFF_HEADS + MOBA_HEADS

C_Q = MLA_Q_RANK
C_KV = MLA_KV_RANK
C_KR = MLA_ROPE
C_DIFF = DIFF_HEADS * 2 * DIFF_HD
C_MOBA = MOBA_HEADS * MOBA_HD
D_IN = C_Q + C_KV + C_KR + 3 * C_DIFF + 3 * C_MOBA
D_MIX = MLA_HEADS * MLA_V + DIFF_HEADS * 2 * DIFF_HD + MOBA_HEADS * MOBA_HD

kernel_name = "hymba_style_mla_diff_moba_trunk"


def rmsnorm(x, g):
    xf = x.astype(jnp.float32)
    y = xf * lax.rsqrt(jnp.mean(xf * xf, axis=-1, keepdims=True) + EPS)
    return (y * g.astype(jnp.float32)).astype(x.dtype)


def alibi_slopes():
    return jnp.asarray([2.0 ** (-8.0 * (h + 1) / N_ALIBI) for h in range(N_ALIBI)], jnp.float32)


def apply_rope(x, pos):
    half = x.shape[-1] // 2
    inv = ROPE_THETA ** (-jnp.arange(half, dtype=jnp.float32) / half)
    ang = pos.astype(jnp.float32)[:, None, :, None] * inv
    cos, sin = jnp.cos(ang), jnp.sin(ang)
    xf = x.astype(jnp.float32)
    x1, x2 = xf[..., :half], xf[..., half:]
    return jnp.concatenate([x1 * cos - x2 * sin, x1 * sin + x2 * cos], axis=-1).astype(x.dtype)


def to_heads(t, hd):
    B, S, _ = t.shape
    return t.reshape(B, S, -1, hd).transpose(0, 2, 1, 3)


def from_heads(t):
    B, H, S, d = t.shape
    return t.transpose(0, 2, 1, 3).reshape(B, S, H * d)


def sweep_query_blocks(body, B, H, S, dv):
    out = lax.map(body, jnp.arange(S // Q_BLOCK, dtype=jnp.int32) * Q_BLOCK)
    return jnp.moveaxis(out, 0, 2).reshape(B, H, S, dv)


def mla_causal_attention(q, k, v):
    B, H, S, dqk = q.shape
    scale = dqk ** -0.5
    kidx = jnp.arange(S)

    def body(start):
        qc = lax.dynamic_slice_in_dim(q, start, Q_BLOCK, axis=2)
        s = jnp.einsum('bhqd,bhkd->bhqk', qc, k, preferred_element_type=jnp.float32) * scale
        qidx = start + jnp.arange(Q_BLOCK)
        s = jnp.where(kidx[None, :] <= qidx[:, None], s, NEG)
        p = jax.nn.softmax(s, axis=-1)
        return jnp.einsum('bhqk,bhkd->bhqd', p.astype(v.dtype), v)

    return sweep_query_blocks(body, B, H, S, v.shape[-1])


def diff_causal_attention(q1, q2, k1, k2, v, pos, slopes, lam):
    B, H, S, d = q1.shape
    scale = d ** -0.5
    kidx = jnp.arange(S)

    def body(start):
        pq = lax.dynamic_slice_in_dim(pos, start, Q_BLOCK, axis=1)
        qidx = start + jnp.arange(Q_BLOCK)
        causal = kidx[None, :] <= qidx[:, None]
        dist = jnp.abs(pq[:, :, None] - pos[:, None, :]).astype(jnp.float32)
        bias = -slopes[None, :, None, None] * dist[:, None]

        def probs(qf, k):
            qc = lax.dynamic_slice_in_dim(qf, start, Q_BLOCK, axis=2)
            s = jnp.einsum('bhqd,bhkd->bhqk', qc, k, preferred_element_type=jnp.float32) * scale + bias
            return jax.nn.softmax(jnp.where(causal, s, NEG), axis=-1)

        p = probs(q1, k1) - lam * probs(q2, k2)
        return jnp.einsum('bhqk,bhkd->bhqd', p.astype(v.dtype), v)

    return sweep_query_blocks(body, B, H, S, v.shape[-1])


def moba_causal_attention(q, k, v, pos, slopes):
    B, H, S, dh = q.shape
    NB = -(-S // MOBA_BLOCK)
    pad = NB * MOBA_BLOCK - S
    kb = jnp.pad(k, ((0, 0), (0, 0), (0, pad), (0, 0))).reshape(B, H, NB, MOBA_BLOCK, dh)
    vb = jnp.pad(v, ((0, 0), (0, 0), (0, pad), (0, 0))).reshape(B, H, NB, MOBA_BLOCK, dh)
    pb = jnp.pad(pos, ((0, 0), (0, pad)), mode='edge').reshape(B, NB, MOBA_BLOCK)
    kmean = jnp.mean(kb.astype(jnp.float32), axis=3)
    K = min(MOBA_TOPK, NB)
    scale = dh ** -0.5
    bi = jnp.arange(B)[:, None, None, None]
    hi = jnp.arange(H)[None, :, None, None]
    m5 = slopes[None, :, None, None, None]

    def body(start):
        qc = lax.dynamic_slice_in_dim(q, start, MOBA_Q_CHUNK, axis=2)
        pq = lax.dynamic_slice_in_dim(pos, start, MOBA_Q_CHUNK, axis=1)
        qidx = start + jnp.arange(MOBA_Q_CHUNK)
        ob = start // MOBA_BLOCK
        gate = jnp.einsum('bhqd,bhnd->bhqn', qc.astype(jnp.float32), kmean)
        gate = jnp.where(jnp.arange(NB) < ob, gate, NEG)
        _, sel = lax.top_k(gate, K)
        valid = jnp.arange(K) < ob
        kg = kb[bi, hi, sel]
        vg = vb[bi, hi, sel]
        pg = pb[bi, sel]
        s_sel = jnp.einsum('bhqd,bhqkjd->bhqkj', qc, kg, preferred_element_type=jnp.float32) * scale
        s_sel = s_sel - m5 * jnp.abs(pq[:, None, :, None, None] - pg).astype(jnp.float32)
        s_sel = jnp.where(valid[None, None, None, :, None], s_sel, NEG)
        s_sel = s_sel.reshape(B, H, MOBA_Q_CHUNK, K * MOBA_BLOCK)
        ko = lax.dynamic_index_in_dim(kb, ob, axis=2, keepdims=False)
        vo = lax.dynamic_index_in_dim(vb, ob, axis=2, keepdims=False)
        po = lax.dynamic_index_in_dim(pb, ob, axis=1, keepdims=False)
        s_own = jnp.einsum('bhqd,bhjd->bhqj', qc, ko, preferred_element_type=jnp.float32) * scale
        s_own = s_own - slopes[None, :, None, None] * jnp.abs(pq[:, None, :, None] - po[:, None, None, :]).astype(jnp.float32)
        kidx = ob * MOBA_BLOCK + jnp.arange(MOBA_BLOCK)
        s_own = jnp.where(kidx[None, :] <= qidx[:, None], s_own, NEG)
        p = jax.nn.softmax(jnp.concatenate([s_sel, s_own], axis=-1), axis=-1).astype(v.dtype)
        p_sel = p[..., :K * MOBA_BLOCK].reshape(B, H, MOBA_Q_CHUNK, K, MOBA_BLOCK)
        return (jnp.einsum('bhqkj,bhqkjd->bhqd', p_sel, vg)
                + jnp.einsum('bhqj,bhjd->bhqd', p[..., K * MOBA_BLOCK:], vo))

    out = lax.map(body, jnp.arange(S // MOBA_Q_CHUNK, dtype=jnp.int32) * MOBA_Q_CHUNK)
    return jnp.moveaxis(out, 0, 2).reshape(B, H, S, dh)


def hybrid_mixer(n, pos, layer_idx, w_in, q_norm, w_uq, kv_norm, w_ukv,
                 lq1, lk1, lq2, lk2, sub_norm, w_out):
    B, S, _ = n.shape
    proj = n @ w_in
    cuts = [int(c) for c in np.cumsum([C_Q, C_KV, C_KR, C_DIFF, C_DIFF, C_DIFF, C_MOBA, C_MOBA])]
    c_q, c_kv, k_r, dq, dk, dv, mq, mk, mv = jnp.split(proj, cuts, axis=-1)
    slopes = alibi_slopes()

    qh = to_heads(rmsnorm(c_q, q_norm) @ w_uq, MLA_NOPE + MLA_ROPE)
    kvh = to_heads(rmsnorm(c_kv, kv_norm) @ w_ukv, MLA_NOPE + MLA_V)
    q_rope = apply_rope(qh[..., MLA_NOPE:], pos)
    k_rope = apply_rope(k_r[:, None], pos)
    q_a = jnp.concatenate([qh[..., :MLA_NOPE], q_rope], axis=-1)
    k_a = jnp.concatenate([kvh[..., :MLA_NOPE],
                           jnp.broadcast_to(k_rope, (B, MLA_HEADS, S, MLA_ROPE))], axis=-1)
    o_a = mla_causal_attention(q_a, k_a, kvh[..., MLA_NOPE:])

    dqh, dkh, dvh = to_heads(dq, 2 * DIFF_HD), to_heads(dk, 2 * DIFF_HD), to_heads(dv, 2 * DIFF_HD)
    lam_init = 0.8 - 0.6 * math.exp(-0.3 * layer_idx)
    lam = (jnp.exp(jnp.sum(lq1.astype(jnp.float32) * lk1.astype(jnp.float32)))
           - jnp.exp(jnp.sum(lq2.astype(jnp.float32) * lk2.astype(jnp.float32))) + lam_init)
    o_b = diff_causal_attention(dqh[..., :DIFF_HD], dqh[..., DIFF_HD:],
                                dkh[..., :DIFF_HD], dkh[..., DIFF_HD:],
                                dvh, pos, slopes[:DIFF_HEADS], lam)
    o_b = rmsnorm(o_b, sub_norm) * (1.0 - lam_init)

    o_c = moba_causal_attention(to_heads(mq, MOBA_HD), to_heads(mk, MOBA_HD),
                                to_heads(mv, MOBA_HD), pos, slopes[DIFF_HEADS:])

    o = jnp.concatenate([from_heads(o_a), from_heads(o_b), from_heads(o_c)], axis=-1)
    return o @ w_out


def memory_cross_attention(n, mem_n, wq, wkv, wo):
    q = to_heads(n @ wq, CROSS_HD)
    kv = mem_n @ wkv
    k = to_heads(kv[..., :D_MODEL], CROSS_HD)
    v = to_heads(kv[..., D_MODEL:], CROSS_HD)
    s = jnp.einsum('bhqd,bhmd->bhqm', q, k, preferred_element_type=jnp.float32) * (CROSS_HD ** -0.5)
    p = jax.nn.softmax(s, axis=-1).astype(v.dtype)
    return from_heads(jnp.einsum('bhqm,bhmd->bhqd', p, v)) @ wo


def squared_relu_mlp(n, w1, w2):
    return jnp.square(jax.nn.relu(n @ w1)) @ w2


def setup_inputs(seed: int = 0) -> dict:
    key = jax.random.key(seed)
    k = jax.random.split(key, 24)
    L = DEPTH
    f32 = jnp.float32
    nrm = lambda kk, shape, scale: jax.random.normal(kk, shape, f32) * scale
    gain = lambda kk, shape: 1.0 + 0.02 * jax.random.normal(kk, shape, f32)
    offset = jax.random.randint(k[2], (BATCH, 1), 0, 1024, dtype=jnp.int32)
    positions = offset + jnp.arange(SEQ, dtype=jnp.int32)[None, :]
    return {
        "x": nrm(k[0], (BATCH, SEQ, D_MODEL), 1.0),
        "mem": nrm(k[1], (BATCH, MEM_LEN, D_MODEL), 1.0),
        "positions": positions,
        "attn_norm": gain(k[3], (L, D_MODEL)),
        "w_in": nrm(k[4], (L, D_MODEL, D_IN), D_MODEL ** -0.5),
        "mla_q_norm": gain(k[5], (L, MLA_Q_RANK)),
        "mla_w_uq": nrm(k[6], (L, MLA_Q_RANK, MLA_HEADS * (MLA_NOPE + MLA_ROPE)), MLA_Q_RANK ** -0.5),
        "mla_kv_norm": gain(k[7], (L, MLA_KV_RANK)),
        "mla_w_ukv": nrm(k[8], (L, MLA_KV_RANK, MLA_HEADS * (MLA_NOPE + MLA_V)), MLA_KV_RANK ** -0.5),
        "diff_lambda_q1": nrm(k[9], (L, DIFF_HD), 0.1),
        "diff_lambda_k1": nrm(k[10], (L, DIFF_HD), 0.1),
        "diff_lambda_q2": nrm(k[11], (L, DIFF_HD), 0.1),
        "diff_lambda_k2": nrm(k[12], (L, DIFF_HD), 0.1),
        "diff_sub_norm": gain(k[13], (L, 2 * DIFF_HD)),
        "w_out": nrm(k[14], (L, D_MIX, D_MODEL), D_MIX ** -0.5),
        "cross_norm": gain(k[15], (L, D_MODEL)),
        "mem_norm": gain(k[16], (L, D_MODEL)),
        "cross_wq": nrm(k[17], (L, D_MODEL, D_MODEL), D_MODEL ** -0.5),
        "cross_wkv": nrm(k[18], (L, D_MODEL, 2 * D_MODEL), D_MODEL ** -0.5),
        "cross_wo": nrm(k[19], (L, D_MODEL, D_MODEL), D_MODEL ** -0.5),
        "mlp_norm": gain(k[20], (L, D_MODEL)),
        "mlp_w1": nrm(k[21], (L, D_MODEL, D_FF), D_MODEL ** -0.5),
        "mlp_w2": nrm(k[22], (L, D_FF, D_MODEL), D_FF ** -0.5),
        "final_norm": gain(k[23], (D_MODEL,)),
    }


def reference(x, mem, positions, attn_norm, w_in, mla_q_norm, mla_w_uq, mla_kv_norm, mla_w_ukv,
              diff_lambda_q1, diff_lambda_k1, diff_lambda_q2, diff_lambda_k2, diff_sub_norm, w_out,
              cross_norm, mem_norm, cross_wq, cross_wkv, cross_wo, mlp_norm, mlp_w1, mlp_w2, final_norm):
    h = x
    for l in range(DEPTH):
        h = h + hybrid_mixer(rmsnorm(h, attn_norm[l]), positions, l, w_in[l],
                             mla_q_norm[l], mla_w_uq[l], mla_kv_norm[l], mla_w_ukv[l],
                             diff_lambda_q1[l], diff_lambda_k1[l], diff_lambda_q2[l], diff_lambda_k2[l],
                             diff_sub_norm[l], w_out[l])
        h = h + memory_cross_attention(rmsnorm(h, cross_norm[l]), rmsnorm(mem, mem_norm[l]),
                                       cross_wq[l], cross_wkv[l], cross_wo[l])
        h = h + squared_relu_mlp(rmsnorm(h, mlp_norm[l]), mlp_w1[l], mlp_w2[l])
    return rmsnorm(h, final_norm)
```

```python
import functools
import math

import jax
import jax.numpy as jnp
import numpy as np
from jax import lax
from jax.experimental import pallas as pl
from jax.experimental.pallas import tpu as pltpu

D_MODEL = 1024
MLA_HEADS = 6
MLA_NOPE = 64
MLA_ROPE = 32
MLA_V = 64
MLA_Q_RANK = 256
MLA_KV_RANK = 128
ROPE_THETA = 10000.0
DIFF_HEADS = 6
DIFF_HD = 32
MOBA_HEADS = 4
MOBA_HD = 64
MOBA_BLOCK = 256
MOBA_TOPK = 3
CROSS_HEADS = 4
CROSS_HD = D_MODEL // CROSS_HEADS
D_FF = 4 * D_MODEL
EPS = 1e-6
NEG = -1e30
N_ALIBI = DIFF_HEADS + MOBA_HEADS
C_DIFF = DIFF_HEADS * 2 * DIFF_HD
C_MOBA = MOBA_HEADS * MOBA_HD

LANES = 128
HALF = LANES // 2
LOG2E = 1.4426950408889634
VMEM_LIMIT = 56 * 1024 * 1024

F32 = jnp.float32
BF16 = jnp.bfloat16

DIFF_MAP2 = HALF
DIFF_BIAS = DIFF_HD
MOBA_BIAS = MOBA_HD
MOBA_SEL = 96
N_SPLIT = 3

O_CQ = 0
O_CKV = O_CQ + MLA_Q_RANK
O_KR = O_CKV + MLA_KV_RANK
O_KRS = O_KR + LANES
O_DQ = O_KRS + LANES
O_DK = O_DQ + DIFF_HEADS * LANES
O_DV = O_DK + DIFF_HEADS * LANES
O_MQ = O_DV + DIFF_HEADS * LANES
O_MK = O_MQ + MOBA_HEADS * LANES
O_MV = O_MK + MOBA_HEADS * LANES
C_WIDE = O_MV + MOBA_HEADS * LANES


def _alibi_slope(h):
    return 2.0 ** (-8.0 * (h + 1) / N_ALIBI)


def _rms(x, g):
    return x * lax.rsqrt(jnp.mean(x * x, axis=-1, keepdims=True) + EPS) * g


def _dot(a, b):
    return jnp.dot(a, b, preferred_element_type=F32)


def _dot_nt(a, b):
    return lax.dot_general(a, b, (((1,), (1,)), ((), ())), preferred_element_type=F32)


def _lane(shape):
    return lax.broadcasted_iota(jnp.int32, shape, len(shape) - 1)


def _onehot_lanes(lanes, value=1.0):
    l = _lane((1, LANES))
    out = jnp.zeros((1, LANES), F32)
    for i in lanes:
        out = jnp.where(l == i, value, out)
    return out


def _split3(x):
    hi = x.astype(BF16).astype(F32)
    r = x - hi
    mid = r.astype(BF16).astype(F32)
    lo = (r - mid).astype(BF16).astype(F32)
    return hi, mid, lo


def _inproj_kernel(h_ref, pos_ref, rel_ref, inv_ref, g_ref, w_ref, qn_ref, wuq_ref, kvn_ref, wukv_ref,
                   qa_ref, ka_ref, va_ref, qd_ref, kd_ref, vd_ref, mq_ref, mk_ref, mv_ref, kmean_ref,
                   *, tm):
    si = pl.program_id(1)
    nb = _rms(h_ref[0], g_ref[...]).astype(BF16)
    lane = _lane((1, LANES))
    one_even = jnp.where(lane == HALF, 1.0, 0.0)
    one_odd = jnp.where(lane == 0, 1.0, 0.0)

    pm = _dot(nb, w_ref[:, O_CQ:O_DQ])
    cq = _rms(pm[:, O_CQ:O_CKV], qn_ref[...]).astype(BF16)
    ckv = _rms(pm[:, O_CKV:O_KR], kvn_ref[...]).astype(BF16)
    q2 = _dot(cq, wuq_ref[...])
    kv2 = _dot(ckv, wukv_ref[...])
    ang = pos_ref[0].astype(F32) * inv_ref[...]
    cs, sn = jnp.cos(ang), jnp.sin(ang)
    krope = pm[:, O_KR:O_KRS] * cs + pm[:, O_KRS:O_DQ] * sn
    qscale = (MLA_NOPE + MLA_ROPE) ** -0.5 * LOG2E
    hw = MLA_HEADS * LANES
    for h in range(MLA_HEADS):
        sl = slice(h * LANES, (h + 1) * LANES)
        sl2 = slice(hw + h * LANES, hw + (h + 1) * LANES)
        qa_ref[0, h] = ((q2[:, sl] * cs + q2[:, sl2] * sn) * qscale).astype(BF16)
        ka_ref[0, h] = (kv2[:, sl] + krope).astype(BF16)
        va_ref[0, h] = (kv2[:, sl2] + (one_even if h % 2 == 0 else one_odd)).astype(BF16)

    relf = rel_ref[0].astype(F32)

    pq = _dot(nb, w_ref[:, O_DQ:O_DK])
    pk = _dot(nb, w_ref[:, O_DK:O_DV])
    pv = _dot(nb, w_ref[:, O_DV:O_MQ])
    dscale = DIFF_HD ** -0.5 * LOG2E
    q_ones = _onehot_lanes([DIFF_BIAS + i for i in range(N_SPLIT)]
                           + [DIFF_MAP2 + DIFF_BIAS + i for i in range(N_SPLIT)])
    for h in range(DIFF_HEADS):
        sl = slice(h * LANES, (h + 1) * LANES)
        qd_ref[0, h] = (pq[:, sl] * dscale + q_ones).astype(BF16)
        kb = pk[:, sl]
        for i, piece in enumerate(_split3(relf * (_alibi_slope(h) * LOG2E))):
            kb = kb + piece * _onehot_lanes([DIFF_BIAS + i, DIFF_MAP2 + DIFF_BIAS + i])
        kd_ref[0, h] = kb.astype(BF16)
        vd_ref[0, h] = (pv[:, sl] + (one_even if h % 2 == 0 else one_odd)).astype(BF16)

    pq = _dot(nb, w_ref[:, O_MQ:O_MK])
    pk = _dot(nb, w_ref[:, O_MK:O_MV])
    pv = _dot(nb, w_ref[:, O_MV:C_WIDE])
    row = lax.broadcasted_iota(jnp.int32, (tm, 1), 0)
    blk = (si * tm + row) // MOBA_BLOCK
    blk_onehot = jnp.where(_lane((tm, LANES)) == MOBA_SEL + blk, 1.0, 0.0)
    for h in range(MOBA_HEADS):
        sl = slice(h * LANES, (h + 1) * LANES)
        mq_ref[0, h] = pq[:, sl]
        kb = pk[:, sl] + blk_onehot
        for i, piece in enumerate(_split3(relf * (_alibi_slope(DIFF_HEADS + h) * LOG2E))):
            kb = kb + piece * _onehot_lanes([MOBA_BIAS + i])
        mk_ref[0, h] = kb.astype(BF16)
        mv_ref[0, h] = (pv[:, sl] + (one_even if h % 2 == 0 else one_odd)).astype(BF16)
    for j in range(tm // MOBA_BLOCK):
        kmean_ref[0, j] = jnp.mean(pk[j * MOBA_BLOCK:(j + 1) * MOBA_BLOCK, :], axis=0, keepdims=True)


def _inproj(h, pos, rel, inv_lane, g, w_wide, qn, wuq, kvn, wukv, *, tm):
    B, S, _ = h.shape
    grid = (B, S // tm)
    tok = lambda b, i: (b, i, 0)
    const2 = lambda b, i: (0, 0)
    head_out = lambda nh, dt: (jax.ShapeDtypeStruct((B, nh, S, LANES), dt),
                               pl.BlockSpec((1, nh, tm, LANES), lambda b, i: (b, 0, i, 0)))
    outs = [head_out(MLA_HEADS, BF16)] * 3 + [head_out(DIFF_HEADS, BF16)] * 3 + [
        head_out(MOBA_HEADS, F32), head_out(MOBA_HEADS, BF16), head_out(MOBA_HEADS, BF16),
        (jax.ShapeDtypeStruct((B, S // MOBA_BLOCK, 1, MOBA_HEADS * LANES), F32),
         pl.BlockSpec((1, tm // MOBA_BLOCK, 1, MOBA_HEADS * LANES), lambda b, i: (b, i, 0, 0)))]
    full = lambda a: pl.BlockSpec(a.shape, const2)
    return pl.pallas_call(
        functools.partial(_inproj_kernel, tm=tm),
        grid=grid,
        in_specs=[pl.BlockSpec((1, tm, D_MODEL), tok), pl.BlockSpec((1, tm, 1), tok),
                  pl.BlockSpec((1, tm, 1), tok), full(inv_lane), full(g), full(w_wide), full(qn),
                  full(wuq), full(kvn), full(wukv)],
        out_specs=[o[1] for o in outs],
        out_shape=[o[0] for o in outs],
        compiler_params=pltpu.CompilerParams(dimension_semantics=("parallel", "parallel"),
                                             vmem_limit_bytes=VMEM_LIMIT),
        name="inproj",
    )(h, pos, rel, inv_lane, g, w_wide, qn, wuq, kvn, wukv)


def _moba_gate_kernel(mq_ref, km_ref, o_ref, *, tm):
    si = pl.program_id(1)
    lane = _lane((tm, LANES))
    row = lax.broadcasted_iota(jnp.int32, (tm, 1), 0)
    own = (si * tm + row) // MOBA_BLOCK
    valid = (lane >= MOBA_SEL) & (lane - MOBA_SEL < own)
    q_ones = _onehot_lanes([MOBA_BIAS + i for i in range(N_SPLIT)])
    for h in range(MOBA_HEADS):
        qf = mq_ref[0, h]
        gate = lax.dot_general(qf, km_ref[0, h], (((1,), (1,)), ((), ())),
                               precision=lax.Precision.HIGHEST, preferred_element_type=F32)
        g = jnp.where(valid, gate, NEG)
        sel = jnp.zeros((tm, LANES), jnp.bool_)
        for _ in range(MOBA_TOPK):
            mx = jnp.max(g, axis=-1, keepdims=True)
            idx = jnp.min(jnp.where(g == mx, lane, LANES), axis=-1, keepdims=True)
            pick = (lane == idx) & (mx > 0.5 * NEG)
            sel = sel | pick
            g = jnp.where(pick, NEG, g)
        selbias = jnp.where((lane >= MOBA_SEL) & jnp.logical_not(sel), NEG, 0.0)
        o_ref[0, h] = (qf * (MOBA_HD ** -0.5 * LOG2E) + q_ones + selbias).astype(BF16)


def _moba_gate(mq, km_pad, *, tm):
    B, H, S, _ = mq.shape
    return pl.pallas_call(
        functools.partial(_moba_gate_kernel, tm=tm),
        grid=(B, S // tm),
        in_specs=[pl.BlockSpec((1, H, tm, LANES), lambda b, i: (b, 0, i, 0)),
                  pl.BlockSpec((1, H, LANES, LANES), lambda b, i: (b, 0, 0, 0))],
        out_specs=pl.BlockSpec((1, H, tm, LANES), lambda b, i: (b, 0, i, 0)),
        out_shape=jax.ShapeDtypeStruct((B, H, S, LANES), BF16),
        compiler_params=pltpu.CompilerParams(dimension_semantics=("parallel", "parallel"),
                                             vmem_limit_bytes=VMEM_LIMIT),
        name="moba_gate",
    )(mq, km_pad)


def _kv_tile(ref, hh, ki, t):
    return ref[0, hh, pl.ds(pl.multiple_of(ki * t, t), t), :]


def _online_step(q, k, v, m, acc, mask):
    s = _dot_nt(q, k)
    if mask is not None:
        s = jnp.where(mask, s, NEG)
    m_new = jnp.maximum(m, jnp.max(s, axis=-1, keepdims=True))
    p = jnp.exp2(s - m_new).astype(BF16)
    acc = acc * jnp.exp2(m - m_new) + _dot(p, v)
    return m_new, acc


def _causal_mask(t):
    return (lax.broadcasted_iota(jnp.int32, (t, t), 1) <= lax.broadcasted_iota(jnp.int32, (t, t), 0))


def _normalize(acc, hh):
    col = HALF if hh == 0 else 0
    return acc * (1.0 / acc[:, col:col + 1])


def _merge_pair(o0, o1):
    return jnp.where(_lane(o0.shape) < HALF, o0, o1)


def _mla_attn_kernel(q_ref, k_ref, v_ref, o_ref, *, t):
    qi = pl.program_id(2)
    outs = []
    for hh in range(2):
        q = q_ref[0, hh]

        def body(ki, c, hh=hh, q=q):
            return _online_step(q, _kv_tile(k_ref, hh, ki, t), _kv_tile(v_ref, hh, ki, t), c[0], c[1], None)

        init = (jnp.full((t, 1), NEG, F32), jnp.zeros((t, LANES), F32))
        m, acc = lax.fori_loop(0, qi, body, init)
        m, acc = _online_step(q, _kv_tile(k_ref, hh, qi, t), _kv_tile(v_ref, hh, qi, t), m, acc, _causal_mask(t))
        outs.append(_normalize(acc, hh))
    o_ref[0] = _merge_pair(*outs).astype(o_ref.dtype)


def _diff_attn_kernel(lam_ref, gain_ref, q_ref, k_ref, v_ref, o_ref, *, t, lam_init):
    qi = pl.program_id(2)
    lv = lam_ref[...]
    lam = (jnp.exp(jnp.sum(lv[0:1] * lv[1:2], axis=-1, keepdims=True))
           - jnp.exp(jnp.sum(lv[2:3] * lv[3:4], axis=-1, keepdims=True)) + lam_init)
    lane = _lane((t, LANES))
    outs = []
    for hh in range(2):
        q = q_ref[0, hh]
        q1 = jnp.where(lane < HALF, q, jnp.zeros_like(q))
        q2 = jnp.where(lane >= HALF, q, jnp.zeros_like(q))

        def body(ki, c, hh=hh, q1=q1, q2=q2):
            k = _kv_tile(k_ref, hh, ki, t)
            v = _kv_tile(v_ref, hh, ki, t)
            m1, a1 = _online_step(q1, k, v, c[0], c[1], None)
            m2, a2 = _online_step(q2, k, v, c[2], c[3], None)
            return m1, a1, m2, a2

        init = (jnp.full((t, 1), NEG, F32), jnp.zeros((t, LANES), F32),
                jnp.full((t, 1), NEG, F32), jnp.zeros((t, LANES), F32))
        c = lax.fori_loop(0, qi, body, init)
        k = _kv_tile(k_ref, hh, qi, t)
        v = _kv_tile(v_ref, hh, qi, t)
        mask = _causal_mask(t)
        _, a1 = _online_step(q1, k, v, c[0], c[1], mask)
        _, a2 = _online_step(q2, k, v, c[2], c[3], mask)
        outs.append(_normalize(a1, hh) - lam * _normalize(a2, hh))
    o = _merge_pair(*outs)
    sq = o * o
    lo = lane < HALF
    ms0 = jnp.sum(jnp.where(lo, sq, 0.0), axis=-1, keepdims=True) * (1.0 / (2 * DIFF_HD))
    ms1 = jnp.sum(jnp.where(lo, 0.0, sq), axis=-1, keepdims=True) * (1.0 / (2 * DIFF_HD))
    inv = jnp.where(lo, lax.rsqrt(ms0 + EPS), lax.rsqrt(ms1 + EPS))
    o_ref[0] = (o * inv * gain_ref[...] * (1.0 - lam_init)).astype(o_ref.dtype)


def _moba_attn_kernel(q_ref, k_ref, v_ref, o_ref, *, t):
    qi = pl.program_id(2)
    lane = _lane((t, LANES))
    outs = []
    for hh in range(2):
        q = q_ref[0, hh]
        q_own = jnp.where(lane < MOBA_SEL, q, jnp.zeros_like(q))
        init = (jnp.full((t, 1), NEG, F32), jnp.zeros((t, LANES), F32))
        c = _online_step(q_own, _kv_tile(k_ref, hh, qi, t), _kv_tile(v_ref, hh, qi, t), init[0], init[1],
                         _causal_mask(t))

        def body(ki, c, hh=hh, q=q):
            return _online_step(q, _kv_tile(k_ref, hh, ki, t), _kv_tile(v_ref, hh, ki, t), c[0], c[1], None)

        _, acc = lax.fori_loop(0, qi, body, c)
        outs.append(_normalize(acc, hh))
    o_ref[0] = _merge_pair(*outs).astype(o_ref.dtype)


def _pair_attention(kernel_fn, q, k, v, extra=(), *, t, name):
    B, H, S, _ = q.shape
    qspec = pl.BlockSpec((1, 2, t, LANES), lambda b, p, i: (b, p, i, 0))
    kvspec = pl.BlockSpec((1, 2, S, LANES), lambda b, p, i: (b, p, 0, 0))
    xspecs = [pl.BlockSpec(a.shape, lambda b, p, i: (0, 0)) for a in extra]
    return pl.pallas_call(
        kernel_fn,
        grid=(B, H // 2, S // t),
        in_specs=xspecs + [qspec, kvspec, kvspec],
        out_specs=pl.BlockSpec((1, t, LANES), lambda b, p, i: (b, i, p)),
        out_shape=jax.ShapeDtypeStruct((B, S, (H // 2) * LANES), BF16),
        compiler_params=pltpu.CompilerParams(dimension_semantics=("parallel", "parallel", "parallel"),
                                             vmem_limit_bytes=VMEM_LIMIT),
        name=name,
    )(*extra, q, k, v)


def _memkv_kernel(x_ref, g_ref, w_ref, o_ref):
    o_ref[0] = _dot(_rms(x_ref[0], g_ref[...]).astype(BF16), w_ref[...]).astype(o_ref.dtype)


def _memkv(mem, g, wkv):
    B, M, _ = mem.shape
    return pl.pallas_call(
        _memkv_kernel,
        grid=(B,),
        in_specs=[pl.BlockSpec((1, M, D_MODEL), lambda b: (b, 0, 0)), pl.BlockSpec(g.shape, lambda b: (0, 0)),
                  pl.BlockSpec(wkv.shape, lambda b: (0, 0))],
        out_specs=pl.BlockSpec((1, M, 2 * D_MODEL), lambda b: (b, 0, 0)),
        out_shape=jax.ShapeDtypeStruct((B, M, 2 * D_MODEL), BF16),
        compiler_params=pltpu.CompilerParams(dimension_semantics=("parallel",), vmem_limit_bytes=VMEM_LIMIT),
        name="memkv",
    )(mem, g, wkv)


def _mix_cross_kernel(h_ref, oa_ref, ob_ref, oc_ref, wout_ref, g_ref, wq_ref, kv_ref, wo_ref, o_ref):
    na, nb = oa_ref.shape[-1], ob_ref.shape[-1]
    h1 = (h_ref[0] + _dot(oa_ref[0], wout_ref[0:na]) + _dot(ob_ref[0], wout_ref[na:na + nb])
          + _dot(oc_ref[0], wout_ref[na + nb:]))
    n = _rms(h1, g_ref[...]).astype(BF16)
    q = (_dot(n, wq_ref[...]) * (CROSS_HD ** -0.5 * LOG2E)).astype(BF16)
    ctx = []
    for h in range(CROSS_HEADS):
        sl = slice(h * CROSS_HD, (h + 1) * CROSS_HD)
        s = _dot_nt(q[:, sl], kv_ref[0, :, sl])
        p = jnp.exp2(s - jnp.max(s, axis=-1, keepdims=True))
        l = jnp.sum(p, axis=-1, keepdims=True)
        c = _dot(p.astype(BF16), kv_ref[0, :, D_MODEL + h * CROSS_HD:D_MODEL + (h + 1) * CROSS_HD])
        ctx.append((c * (1.0 / l)).astype(BF16))
    o_ref[0] = h1 + _dot(jnp.concatenate(ctx, axis=-1), wo_ref[...])


def _mix_cross(h, oa, ob, oc, wout, g, wq, memkv, wo, *, tm):
    B, S, _ = h.shape
    tok = lambda b, i: (b, i, 0)
    const2 = lambda b, i: (0, 0)
    full = lambda a: pl.BlockSpec(a.shape, const2)
    return pl.pallas_call(
        _mix_cross_kernel,
        grid=(B, S // tm),
        in_specs=[pl.BlockSpec((1, tm, D_MODEL), tok), pl.BlockSpec((1, tm, oa.shape[-1]), tok),
                  pl.BlockSpec((1, tm, ob.shape[-1]), tok), pl.BlockSpec((1, tm, oc.shape[-1]), tok),
                  full(wout), full(g), full(wq),
                  pl.BlockSpec((1,) + memkv.shape[1:], lambda b, i: (b, 0, 0)), full(wo)],
        out_specs=pl.BlockSpec((1, tm, D_MODEL), tok),
        out_shape=jax.ShapeDtypeStruct(h.shape, F32),
        compiler_params=pltpu.CompilerParams(dimension_semantics=("parallel", "parallel"),
                                             vmem_limit_bytes=VMEM_LIMIT),
        name="mix_cross",
    )(h, oa, ob, oc, wout, g, wq, memkv, wo)


def _mlp_kernel(h_ref, g_ref, w1_ref, w2_ref, gf_ref, o_ref, *, final, chunk):
    h = h_ref[0]
    n = _rms(h, g_ref[...]).astype(BF16)
    acc = h
    for c in range(D_FF // chunk):
        a = jnp.maximum(_dot(n, w1_ref[:, c * chunk:(c + 1) * chunk]), 0.0)
        acc = acc + _dot((a * a).astype(BF16), w2_ref[c * chunk:(c + 1) * chunk, :])
    o_ref[0] = _rms(acc, gf_ref[...]) if final else acc


def _mlp(h, g, w1, w2, gf, *, tm, final):
    B, S, _ = h.shape
    tok = lambda b, i: (b, i, 0)
    const2 = lambda b, i: (0, 0)
    resident = lambda a: pl.BlockSpec(a.shape, const2, pipeline_mode=pl.Buffered(1))
    return pl.pallas_call(
        functools.partial(_mlp_kernel, final=final, chunk=D_MODEL),
        grid=(B, S // tm),
        in_specs=[pl.BlockSpec((1, tm, D_MODEL), tok), pl.BlockSpec(g.shape, const2), resident(w1), resident(w2),
                  pl.BlockSpec(gf.shape, const2)],
        out_specs=pl.BlockSpec((1, tm, D_MODEL), tok),
        out_shape=jax.ShapeDtypeStruct(h.shape, F32),
        compiler_params=pltpu.CompilerParams(dimension_semantics=("parallel", "parallel"),
                                             vmem_limit_bytes=VMEM_LIMIT),
        name="mlp",
    )(h, g, w1, w2, gf)


def _rot_pairs(w):
    half = w.shape[-1] // 2
    return jnp.concatenate([-w[..., half:], w[..., :half]], axis=-1)


def _pad_lanes(w, lo, width=LANES):
    pad = [(0, 0)] * (w.ndim - 1) + [(lo, width - lo - w.shape[-1])]
    return jnp.pad(w, pad)


def _pair_v_layout(w, nheads, hd):
    K = w.shape[0]
    e = w.reshape(K, nheads // 2, 2, hd)
    even = _pad_lanes(e[:, :, 0], 0)
    odd = _pad_lanes(e[:, :, 1], HALF)
    return jnp.stack([even, odd], axis=2).reshape(K, nheads * LANES)


def _widen_w_in(w):
    K = w.shape[0]
    cuts = np.cumsum([0, MLA_Q_RANK, MLA_KV_RANK, MLA_ROPE, C_DIFF, C_DIFF, C_DIFF, C_MOBA, C_MOBA, C_MOBA])
    cq, ckv, kr, dq, dk, dv, mq, mk, mv = [w[:, int(a):int(b)] for a, b in zip(cuts[:-1], cuts[1:])]
    two_maps = lambda x: _pad_lanes(x.reshape(K, DIFF_HEADS, 2, DIFF_HD), 0, HALF).reshape(K, DIFF_HEADS * LANES)
    heads = lambda x: _pad_lanes(x.reshape(K, MOBA_HEADS, MOBA_HD), 0).reshape(K, MOBA_HEADS * LANES)
    return jnp.concatenate([
        cq, ckv, _pad_lanes(kr, MLA_NOPE), _pad_lanes(_rot_pairs(kr), MLA_NOPE),
        two_maps(dq), two_maps(dk), _pair_v_layout(dv, DIFF_HEADS, 2 * DIFF_HD),
        heads(mq), heads(mk), _pair_v_layout(mv, MOBA_HEADS, MOBA_HD)], axis=1).astype(BF16)


def _widen_w_uq(w):
    K = w.shape[0]
    e = w.reshape(K, MLA_HEADS, MLA_NOPE + MLA_ROPE)
    plain = _pad_lanes(e, 0).reshape(K, MLA_HEADS * LANES)
    rot = _pad_lanes(_rot_pairs(e[..., MLA_NOPE:]), MLA_NOPE).reshape(K, MLA_HEADS * LANES)
    return jnp.concatenate([plain, rot], axis=1).astype(BF16)


def _widen_w_ukv(w):
    K = w.shape[0]
    e = w.reshape(K, MLA_HEADS, MLA_NOPE + MLA_V)
    kn = _pad_lanes(e[..., :MLA_NOPE], 0).reshape(K, MLA_HEADS * LANES)
    vv = _pair_v_layout(e[..., MLA_NOPE:].reshape(K, MLA_HEADS * MLA_V), MLA_HEADS, MLA_V)
    return jnp.concatenate([kn, vv], axis=1).astype(BF16)


def kernel(x, mem, positions, attn_norm, w_in, mla_q_norm, mla_w_uq, mla_kv_norm, mla_w_ukv, diff_lambda_q1, diff_lambda_k1, diff_lambda_q2, diff_lambda_k2, diff_sub_norm, w_out, cross_norm, mem_norm, cross_wq, cross_wkv, cross_wo, mlp_norm, mlp_w1, mlp_w2, final_norm):
    B, S, _ = x.shape
    depth = w_in.shape[0]
    tm = 512
    t = MOBA_BLOCK
    assert S % tm == 0 and tm % MOBA_BLOCK == 0 and S // MOBA_BLOCK <= LANES - MOBA_SEL

    pos = positions.astype(jnp.int32)[..., None]
    rel = pos - pos[:, :1]
    half = MLA_ROPE // 2
    inv = ROPE_THETA ** (-jnp.arange(half, dtype=F32) / half)
    inv_lane = _pad_lanes(jnp.concatenate([inv, inv])[None, :], MLA_NOPE)
    row = lambda v: v.astype(F32)[None, :]

    h = x
    for l in range(depth):
        outs = _inproj(h, pos, rel, inv_lane, row(attn_norm[l]), _widen_w_in(w_in[l]), row(mla_q_norm[l]),
                       _widen_w_uq(mla_w_uq[l]), row(mla_kv_norm[l]), _widen_w_ukv(mla_w_ukv[l]), tm=tm)
        qa, ka, va, qd, kd, vd, mq, mk, mv, kmean = outs
        km = kmean.reshape(B, S // MOBA_BLOCK, MOBA_HEADS, LANES).transpose(0, 2, 1, 3)
        km_pad = jnp.pad(km, ((0, 0), (0, 0), (MOBA_SEL, LANES - MOBA_SEL - S // MOBA_BLOCK), (0, 0)))
        mq_aug = _moba_gate(mq, km_pad, tm=tm)

        o_a = _pair_attention(functools.partial(_mla_attn_kernel, t=t), qa, ka, va, t=t, name="mla_attn")
        lam_rows = jnp.stack([diff_lambda_q1[l], diff_lambda_k1[l], diff_lambda_q2[l], diff_lambda_k2[l]])
        lam_rows = jnp.pad(lam_rows.astype(F32), ((0, 4), (0, LANES - DIFF_HD)))
        gain = jnp.tile(diff_sub_norm[l].astype(F32), 2)[None, :]
        lam_init = 0.8 - 0.6 * math.exp(-0.3 * l)
        o_b = _pair_attention(functools.partial(_diff_attn_kernel, t=t, lam_init=lam_init), qd, kd, vd,
                              extra=(lam_rows, gain), t=t, name="diff_attn")
        o_c = _pair_attention(functools.partial(_moba_attn_kernel, t=t), mq_aug, mk, mv, t=t, name="moba_attn")

        memkv = _memkv(mem, row(mem_norm[l]), cross_wkv[l].astype(BF16))
        h = _mix_cross(h, o_a, o_b, o_c, w_out[l].astype(BF16), row(cross_norm[l]), cross_wq[l].astype(BF16),
                       memkv, cross_wo[l].astype(BF16), tm=tm)
        h = _mlp(h, row(mlp_norm[l]), mlp_w1[l].astype(BF16), mlp_w2[l].astype(BF16), row(final_norm),
                 tm=tm, final=(l == depth - 1))
    return h
```

```python
import functools
import math

import jax
import jax.numpy as jnp
import numpy as np
from jax import lax
from jax.experimental import pallas as pl
from jax.experimental.pallas import tpu as pltpu

D_MODEL = 1024
MLA_HEADS = 6
MLA_NOPE = 64
MLA_ROPE = 32
MLA_V = 64
MLA_Q_RANK = 256
MLA_KV_RANK = 128
ROPE_THETA = 10000.0
DIFF_HEADS = 6
DIFF_HD = 32
MOBA_HEADS = 4
MOBA_HD = 64
MOBA_BLOCK = 256
MOBA_TOPK = 3
CROSS_HEADS = 4
CROSS_HD = D_MODEL // CROSS_HEADS
D_FF = 4 * D_MODEL
EPS = 1e-6
NEG = -1e30
N_ALIBI = DIFF_HEADS + MOBA_HEADS
C_DIFF = DIFF_HEADS * 2 * DIFF_HD
C_MOBA = MOBA_HEADS * MOBA_HD

LANES = 128
HALF = LANES // 2
LOG2E = 1.4426950408889634
VMEM_LIMIT = 56 * 1024 * 1024

F32 = jnp.float32
BF16 = jnp.bfloat16

V_DIM = 64
V_ONE = V_DIM
V_ROWS = 80
DIFF_MAP2 = HALF
DIFF_BIAS = DIFF_HD
MOBA_BIAS = MOBA_HD
MOBA_SEL = 96
N_SPLIT = 3

O_CQ = 0
O_CKV = O_CQ + MLA_Q_RANK
O_KR = O_CKV + MLA_KV_RANK
O_KRS = O_KR + LANES
O_DQ = O_KRS + LANES
O_DK = O_DQ + DIFF_HEADS * LANES
O_DV = O_DK + DIFF_HEADS * LANES
O_MQ = O_DV + DIFF_HEADS * LANES
O_MK = O_MQ + MOBA_HEADS * LANES
O_MV = O_MK + MOBA_HEADS * LANES
C_WIDE = O_MV + MOBA_HEADS * LANES


def _alibi_slope(h):
    return 2.0 ** (-8.0 * (h + 1) / N_ALIBI)


def _rms(x, g):
    return x * lax.rsqrt(jnp.mean(x * x, axis=-1, keepdims=True) + EPS) * g


def _dot(a, b):
    return jnp.dot(a, b, preferred_element_type=F32)


def _dot_nt(a, b):
    return lax.dot_general(a, b, (((1,), (1,)), ((), ())), preferred_element_type=F32)


def _lane(shape):
    return lax.broadcasted_iota(jnp.int32, shape, len(shape) - 1)


def _onehot_lanes(lanes, value=1.0):
    l = _lane((1, LANES))
    out = jnp.zeros((1, LANES), F32)
    for i in lanes:
        out = jnp.where(l == i, value, out)
    return out


def _split3(x):
    hi = x.astype(BF16).astype(F32)
    r = x - hi
    mid = r.astype(BF16).astype(F32)
    lo = (r - mid).astype(BF16).astype(F32)
    return hi, mid, lo


def _store_vt(ref, h, v, t):
    for j in range(v.shape[0] // t):
        ref[0, h, j] = v[j * t:(j + 1) * t].T[:V_ROWS].astype(BF16)


def _inproj_kernel(h_ref, pos_ref, rel_ref, inv_ref, g_ref, w_ref, qn_ref, wuq_ref, kvn_ref, wukv_ref,
                   qa_ref, ka_ref, va_ref, qd_ref, kd_ref, vd_ref, mq_ref, mk_ref, mv_ref, kmean_ref,
                   *, tm, t):
    si = pl.program_id(1)
    nb = _rms(h_ref[0], g_ref[...]).astype(BF16)
    lane = _lane((1, LANES))
    v_one = jnp.where(lane == V_ONE, 1.0, 0.0)

    pm = _dot(nb, w_ref[:, O_CQ:O_DQ])
    cq = _rms(pm[:, O_CQ:O_CKV], qn_ref[...]).astype(BF16)
    ckv = _rms(pm[:, O_CKV:O_KR], kvn_ref[...]).astype(BF16)
    q2 = _dot(cq, wuq_ref[...])
    kv2 = _dot(ckv, wukv_ref[...])
    ang = pos_ref[0].astype(F32) * inv_ref[...]
    cs, sn = jnp.cos(ang), jnp.sin(ang)
    krope = pm[:, O_KR:O_KRS] * cs + pm[:, O_KRS:O_DQ] * sn
    qscale = (MLA_NOPE + MLA_ROPE) ** -0.5 * LOG2E
    hw = MLA_HEADS * LANES
    for h in range(MLA_HEADS):
        sl = slice(h * LANES, (h + 1) * LANES)
        sl2 = slice(hw + h * LANES, hw + (h + 1) * LANES)
        qa_ref[0, h] = ((q2[:, sl] * cs + q2[:, sl2] * sn) * qscale).T.astype(BF16)
        ka_ref[0, h] = (kv2[:, sl] + krope).astype(BF16)
        _store_vt(va_ref, h, kv2[:, sl2] + v_one, t)

    relf = rel_ref[0].astype(F32)

    pq = _dot(nb, w_ref[:, O_DQ:O_DK])
    pk = _dot(nb, w_ref[:, O_DK:O_DV])
    pv = _dot(nb, w_ref[:, O_DV:O_MQ])
    dscale = DIFF_HD ** -0.5 * LOG2E
    q_ones = _onehot_lanes([DIFF_BIAS + i for i in range(N_SPLIT)]
                           + [DIFF_MAP2 + DIFF_BIAS + i for i in range(N_SPLIT)])
    for h in range(DIFF_HEADS):
        sl = slice(h * LANES, (h + 1) * LANES)
        qd_ref[0, h] = (pq[:, sl] * dscale + q_ones).T.astype(BF16)
        kb = pk[:, sl]
        for i, piece in enumerate(_split3(relf * (_alibi_slope(h) * LOG2E))):
            kb = kb + piece * _onehot_lanes([DIFF_BIAS + i, DIFF_MAP2 + DIFF_BIAS + i])
        kd_ref[0, h] = kb.astype(BF16)
        _store_vt(vd_ref, h, pv[:, sl] + v_one, t)

    pq = _dot(nb, w_ref[:, O_MQ:O_MK])
    pk = _dot(nb, w_ref[:, O_MK:O_MV])
    pv = _dot(nb, w_ref[:, O_MV:C_WIDE])
    row = lax.broadcasted_iota(jnp.int32, (tm, 1), 0)
    blk = (si * tm + row) // MOBA_BLOCK
    blk_onehot = jnp.where(_lane((tm, LANES)) == MOBA_SEL + blk, 1.0, 0.0)
    for h in range(MOBA_HEADS):
        sl = slice(h * LANES, (h + 1) * LANES)
        mq_ref[0, h] = pq[:, sl]
        kb = pk[:, sl] + blk_onehot
        for i, piece in enumerate(_split3(relf * (_alibi_slope(DIFF_HEADS + h) * LOG2E))):
            kb = kb + piece * _onehot_lanes([MOBA_BIAS + i])
        mk_ref[0, h] = kb.astype(BF16)
        _store_vt(mv_ref, h, pv[:, sl] + v_one, t)
    for j in range(tm // MOBA_BLOCK):
        kmean_ref[0, j] = jnp.mean(pk[j * MOBA_BLOCK:(j + 1) * MOBA_BLOCK, :], axis=0, keepdims=True)


def _inproj(h, pos, rel, inv_lane, g, w_wide, qn, wuq, kvn, wukv, *, tm, t):
    B, S, _ = h.shape
    grid = (B, S // tm)
    tok = lambda b, i: (b, i, 0)
    const2 = lambda b, i: (0, 0)
    q_out = lambda nh: (jax.ShapeDtypeStruct((B, nh, LANES, S), BF16),
                        pl.BlockSpec((1, nh, LANES, tm), lambda b, i: (b, 0, 0, i)))
    k_out = lambda nh, dt: (jax.ShapeDtypeStruct((B, nh, S, LANES), dt),
                            pl.BlockSpec((1, nh, tm, LANES), lambda b, i: (b, 0, i, 0)))
    v_out = lambda nh: (jax.ShapeDtypeStruct((B, nh, S // t, V_ROWS, t), BF16),
                        pl.BlockSpec((1, nh, tm // t, V_ROWS, t), lambda b, i: (b, 0, i, 0, 0)))
    outs = [q_out(MLA_HEADS), k_out(MLA_HEADS, BF16), v_out(MLA_HEADS),
            q_out(DIFF_HEADS), k_out(DIFF_HEADS, BF16), v_out(DIFF_HEADS),
            k_out(MOBA_HEADS, F32), k_out(MOBA_HEADS, BF16), v_out(MOBA_HEADS),
            (jax.ShapeDtypeStruct((B, S // MOBA_BLOCK, 1, MOBA_HEADS * LANES), F32),
             pl.BlockSpec((1, tm // MOBA_BLOCK, 1, MOBA_HEADS * LANES), lambda b, i: (b, i, 0, 0)))]
    full = lambda a: pl.BlockSpec(a.shape, const2)
    return pl.pallas_call(
        functools.partial(_inproj_kernel, tm=tm, t=t),
        grid=grid,
        in_specs=[pl.BlockSpec((1, tm, D_MODEL), tok), pl.BlockSpec((1, tm, 1), tok),
                  pl.BlockSpec((1, tm, 1), tok), full(inv_lane), full(g), full(w_wide), full(qn),
                  full(wuq), full(kvn), full(wukv)],
        out_specs=[o[1] for o in outs],
        out_shape=[o[0] for o in outs],
        compiler_params=pltpu.CompilerParams(dimension_semantics=("parallel", "parallel"),
                                             vmem_limit_bytes=VMEM_LIMIT),
        name="inproj",
    )(h, pos, rel, inv_lane, g, w_wide, qn, wuq, kvn, wukv)


def _moba_gate_kernel(mq_ref, km_ref, o_ref, *, tm):
    si = pl.program_id(1)
    lane = _lane((tm, LANES))
    lanef = lane.astype(F32)
    row = lax.broadcasted_iota(jnp.int32, (tm, 1), 0)
    own = (si * tm + row) // MOBA_BLOCK
    valid = (lane >= MOBA_SEL) & (lane - MOBA_SEL < own)
    q_ones = _onehot_lanes([MOBA_BIAS + i for i in range(N_SPLIT)])
    for h in range(MOBA_HEADS):
        qf = mq_ref[0, h]
        gate = lax.dot_general(qf, km_ref[0, h], (((1,), (1,)), ((), ())),
                               precision=lax.Precision.HIGHEST, preferred_element_type=F32)
        g = jnp.where(valid, gate, NEG)
        sel = lane - MOBA_SEL == own
        for _ in range(MOBA_TOPK):
            mx = jnp.max(g, axis=-1, keepdims=True)
            idx = jnp.min(jnp.where(g == mx, lanef, float(LANES)), axis=-1, keepdims=True)
            pick = (lanef == idx) & (mx > 0.5 * NEG)
            sel = sel | pick
            g = jnp.where(pick, NEG, g)
        selbias = jnp.where((lane >= MOBA_SEL) & jnp.logical_not(sel), NEG, 0.0)
        o_ref[0, h] = (qf * (MOBA_HD ** -0.5 * LOG2E) + q_ones + selbias).T.astype(BF16)


def _moba_gate(mq, km_pad, *, tm):
    B, H, S, _ = mq.shape
    return pl.pallas_call(
        functools.partial(_moba_gate_kernel, tm=tm),
        grid=(B, S // tm),
        in_specs=[pl.BlockSpec((1, H, tm, LANES), lambda b, i: (b, 0, i, 0)),
                  pl.BlockSpec((1, H, LANES, LANES), lambda b, i: (b, 0, 0, 0))],
        out_specs=pl.BlockSpec((1, H, LANES, tm), lambda b, i: (b, 0, 0, i)),
        out_shape=jax.ShapeDtypeStruct((B, H, LANES, S), BF16),
        compiler_params=pltpu.CompilerParams(dimension_semantics=("parallel", "parallel"),
                                             vmem_limit_bytes=VMEM_LIMIT),
        name="moba_gate",
    )(mq, km_pad)


def _causal_flash(chains, k_ref, vt_ref, scratch, qi, t, tk):
    assert t == 2 * tk
    s_a, s_b, mb_a, mb_b, m_scr, acc_scr = scratch
    nc = len(chains)

    def scores(kt, s_dst, mb_dst):
        tiles = {}
        for c, (hh, qt) in enumerate(chains):
            if hh not in tiles:
                tiles[hh] = k_ref[0, hh, pl.ds(pl.multiple_of(kt * tk, tk), tk), :]
            s = _dot(tiles[hh], qt)
            s_dst[c] = s
            mb_dst[c] = jnp.max(s, axis=0, keepdims=True)

    def softmax_pv(kt, s_src, mb_src, first_key=None):
        for c, (hh, _) in enumerate(chains):
            s = s_src[c]
            if first_key is None:
                mb = mb_src[c]
            else:
                keep = (lax.broadcasted_iota(jnp.int32, (tk, t), 0) + first_key
                        <= lax.broadcasted_iota(jnp.int32, (tk, t), 1))
                s = jnp.where(keep, s, NEG)
                mb = jnp.max(s, axis=0, keepdims=True)
            m = m_scr[c]
            m_new = jnp.maximum(m, mb)
            p = jnp.exp2(s - m_new).astype(BF16)
            acc_scr[c] = acc_scr[c] * jnp.exp2(m - m_new) + _dot(vt_ref[0, hh, kt], p)
            m_scr[c] = m_new

    for c in range(nc):
        m_scr[c] = jnp.full((1, t), NEG, F32)
        acc_scr[c] = jnp.zeros((V_ROWS, t), F32)
    scores(0, s_a, mb_a)

    def pair(j, carry):
        scores(2 * j + 1, s_b, mb_b)
        softmax_pv(2 * j, s_a, mb_a)
        scores(2 * j + 2, s_a, mb_a)
        softmax_pv(2 * j + 1, s_b, mb_b)
        return carry

    lax.fori_loop(0, qi, pair, 0)
    scores(2 * qi + 1, s_b, mb_b)
    softmax_pv(2 * qi, s_a, None, first_key=0)
    softmax_pv(2 * qi + 1, s_b, None, first_key=tk)
    outs = []
    for c in range(nc):
        acc = acc_scr[c]
        outs.append(acc[:V_DIM] * (1.0 / acc[V_ONE:V_ONE + 1]))
    return outs


def _flash_scratch(nc, t, tk):
    return ([pltpu.VMEM((nc, tk, t), F32)] * 2 + [pltpu.VMEM((nc, 1, t), F32)] * 3
            + [pltpu.VMEM((nc, V_ROWS, t), F32)])


def _mla_attn_kernel(q_ref, k_ref, vt_ref, o_ref, *scratch, t, tk):
    o0, o1 = _causal_flash([(0, q_ref[0, 0]), (1, q_ref[0, 1])], k_ref, vt_ref, scratch, pl.program_id(2), t, tk)
    o_ref[0] = jnp.concatenate([o0, o1], axis=0).T.astype(o_ref.dtype)


def _diff_attn_kernel(lam_ref, gain_ref, q_ref, k_ref, vt_ref, o_ref, *scratch, t, tk, lam_init):
    lv = lam_ref[...]
    lam = (jnp.exp(jnp.sum(lv[0:1] * lv[1:2], axis=-1, keepdims=True))
           - jnp.exp(jnp.sum(lv[2:3] * lv[3:4], axis=-1, keepdims=True)) + lam_init)
    feat = lax.broadcasted_iota(jnp.int32, (LANES, t), 0)
    chains = []
    for hh in range(2):
        q = q_ref[0, hh]
        chains.append((hh, jnp.where(feat < DIFF_MAP2, q, jnp.zeros_like(q))))
        chains.append((hh, jnp.where(feat >= DIFF_MAP2, q, jnp.zeros_like(q))))
    a0, b0, a1, b1 = _causal_flash(chains, k_ref, vt_ref, scratch, pl.program_id(2), t, tk)
    outs = []
    for a, b in ((a0, b0), (a1, b1)):
        o = a - lam * b
        outs.append(o * lax.rsqrt(jnp.mean(o * o, axis=0, keepdims=True) + EPS))
    o_ref[0] = (jnp.concatenate(outs, axis=0).T * gain_ref[...] * (1.0 - lam_init)).astype(o_ref.dtype)


def _pair_attention(kernel_fn, q, k, vt, extra=(), *, t, tk, chains, name):
    B, H, _, S = q.shape
    qspec = pl.BlockSpec((1, 2, LANES, t), lambda b, p, i: (b, p, 0, i))
    kspec = pl.BlockSpec((1, 2, S, LANES), lambda b, p, i: (b, p, 0, 0))
    vspec = pl.BlockSpec((1, 2, S // tk, V_ROWS, tk), lambda b, p, i: (b, p, 0, 0, 0))
    xspecs = [pl.BlockSpec(a.shape, lambda b, p, i: (0, 0)) for a in extra]
    return pl.pallas_call(
        functools.partial(kernel_fn, t=t, tk=tk),
        grid=(B, H // 2, S // t),
        in_specs=xspecs + [qspec, kspec, vspec],
        out_specs=pl.BlockSpec((1, t, LANES), lambda b, p, i: (b, i, p)),
        out_shape=jax.ShapeDtypeStruct((B, S, (H // 2) * LANES), BF16),
        scratch_shapes=_flash_scratch(chains, t, tk),
        compiler_params=pltpu.CompilerParams(dimension_semantics=("parallel", "parallel", "parallel"),
                                             vmem_limit_bytes=VMEM_LIMIT),
        name=name,
    )(*extra, q, k, vt)


def _memkv_kernel(x_ref, g_ref, w_ref, o_ref):
    o_ref[0] = _dot(_rms(x_ref[0], g_ref[...]).astype(BF16), w_ref[...]).astype(o_ref.dtype)


def _memkv(mem, g, wkv):
    B, M, _ = mem.shape
    return pl.pallas_call(
        _memkv_kernel,
        grid=(B,),
        in_specs=[pl.BlockSpec((1, M, D_MODEL), lambda b: (b, 0, 0)), pl.BlockSpec(g.shape, lambda b: (0, 0)),
                  pl.BlockSpec(wkv.shape, lambda b: (0, 0))],
        out_specs=pl.BlockSpec((1, M, 2 * D_MODEL), lambda b: (b, 0, 0)),
        out_shape=jax.ShapeDtypeStruct((B, M, 2 * D_MODEL), BF16),
        compiler_params=pltpu.CompilerParams(dimension_semantics=("parallel",), vmem_limit_bytes=VMEM_LIMIT),
        name="memkv",
    )(mem, g, wkv)


def _mix_cross_kernel(h_ref, oa_ref, ob_ref, oc_ref, wout_ref, g_ref, wq_ref, kv_ref, wo_ref, o_ref):
    na, nb = oa_ref.shape[-1], ob_ref.shape[-1]
    h1 = (h_ref[0] + _dot(oa_ref[0], wout_ref[0:na]) + _dot(ob_ref[0], wout_ref[na:na + nb])
          + _dot(oc_ref[0], wout_ref[na + nb:]))
    n = _rms(h1, g_ref[...]).astype(BF16)
    q = (_dot(n, wq_ref[...]) * (CROSS_HD ** -0.5 * LOG2E)).astype(BF16)
    ctx = []
    for h in range(CROSS_HEADS):
        sl = slice(h * CROSS_HD, (h + 1) * CROSS_HD)
        s = _dot_nt(q[:, sl], kv_ref[0, :, sl])
        p = jnp.exp2(s - jnp.max(s, axis=-1, keepdims=True))
        l = jnp.sum(p, axis=-1, keepdims=True)
        c = _dot(p.astype(BF16), kv_ref[0, :, D_MODEL + h * CROSS_HD:D_MODEL + (h + 1) * CROSS_HD])
        ctx.append((c * (1.0 / l)).astype(BF16))
    o_ref[0] = h1 + _dot(jnp.concatenate(ctx, axis=-1), wo_ref[...])


def _mix_cross(h, oa, ob, oc, wout, g, wq, memkv, wo, *, tm):
    B, S, _ = h.shape
    tok = lambda b, i: (b, i, 0)
    const2 = lambda b, i: (0, 0)
    full = lambda a: pl.BlockSpec(a.shape, const2)
    return pl.pallas_call(
        _mix_cross_kernel,
        grid=(B, S // tm),
        in_specs=[pl.BlockSpec((1, tm, D_MODEL), tok), pl.BlockSpec((1, tm, oa.shape[-1]), tok),
                  pl.BlockSpec((1, tm, ob.shape[-1]), tok), pl.BlockSpec((1, tm, oc.shape[-1]), tok),
                  full(wout), full(g), full(wq),
                  pl.BlockSpec((1,) + memkv.shape[1:], lambda b, i: (b, 0, 0)), full(wo)],
        out_specs=pl.BlockSpec((1, tm, D_MODEL), tok),
        out_shape=jax.ShapeDtypeStruct(h.shape, F32),
        compiler_params=pltpu.CompilerParams(dimension_semantics=("parallel", "parallel"),
                                             vmem_limit_bytes=VMEM_LIMIT),
        name="mix_cross",
    )(h, oa, ob, oc, wout, g, wq, memkv, wo)


def _mlp_kernel(h_ref, g_ref, w1_ref, w2_ref, gf_ref, o_ref, *, final, chunk):
    h = h_ref[0]
    n = _rms(h, g_ref[...]).astype(BF16)
    acc = h
    for c in range(D_FF // chunk):
        a = jnp.maximum(_dot(n, w1_ref[:, c * chunk:(c + 1) * chunk]), 0.0)
        acc = acc + _dot((a * a).astype(BF16), w2_ref[c * chunk:(c + 1) * chunk, :])
    o_ref[0] = _rms(acc, gf_ref[...]) if final else acc


def _mlp(h, g, w1, w2, gf, *, tm, final):
    B, S, _ = h.shape
    tok = lambda b, i: (b, i, 0)
    const2 = lambda b, i: (0, 0)
    resident = lambda a: pl.BlockSpec(a.shape, const2, pipeline_mode=pl.Buffered(1))
    return pl.pallas_call(
        functools.partial(_mlp_kernel, final=final, chunk=D_MODEL),
        grid=(B, S // tm),
        in_specs=[pl.BlockSpec((1, tm, D_MODEL), tok), pl.BlockSpec(g.shape, const2), resident(w1), resident(w2),
                  pl.BlockSpec(gf.shape, const2)],
        out_specs=pl.BlockSpec((1, tm, D_MODEL), tok),
        out_shape=jax.ShapeDtypeStruct(h.shape, F32),
        compiler_params=pltpu.CompilerParams(dimension_semantics=("parallel", "parallel"),
                                             vmem_limit_bytes=VMEM_LIMIT),
        name="mlp",
    )(h, g, w1, w2, gf)


def _rot_pairs(w):
    half = w.shape[-1] // 2
    return jnp.concatenate([-w[..., half:], w[..., :half]], axis=-1)


def _pad_lanes(w, lo, width=LANES):
    pad = [(0, 0)] * (w.ndim - 1) + [(lo, width - lo - w.shape[-1])]
    return jnp.pad(w, pad)


def _head_groups(w, nheads, hd, lo=0, width=LANES):
    K = w.shape[0]
    return _pad_lanes(w.reshape(K, nheads, hd), lo, width).reshape(K, nheads * width)


def _widen_w_in(w):
    cuts = np.cumsum([0, MLA_Q_RANK, MLA_KV_RANK, MLA_ROPE, C_DIFF, C_DIFF, C_DIFF, C_MOBA, C_MOBA, C_MOBA])
    cq, ckv, kr, dq, dk, dv, mq, mk, mv = [w[:, int(a):int(b)] for a, b in zip(cuts[:-1], cuts[1:])]
    two_maps = lambda x: _head_groups(x, 2 * DIFF_HEADS, DIFF_HD, 0, HALF)
    return jnp.concatenate([
        cq, ckv, _pad_lanes(kr, MLA_NOPE), _pad_lanes(_rot_pairs(kr), MLA_NOPE),
        two_maps(dq), two_maps(dk), _head_groups(dv, DIFF_HEADS, 2 * DIFF_HD),
        _head_groups(mq, MOBA_HEADS, MOBA_HD), _head_groups(mk, MOBA_HEADS, MOBA_HD),
        _head_groups(mv, MOBA_HEADS, MOBA_HD)], axis=1).astype(BF16)


def _widen_w_uq(w):
    K = w.shape[0]
    e = w.reshape(K, MLA_HEADS, MLA_NOPE + MLA_ROPE)
    plain = _pad_lanes(e, 0).reshape(K, MLA_HEADS * LANES)
    rot = _pad_lanes(_rot_pairs(e[..., MLA_NOPE:]), MLA_NOPE).reshape(K, MLA_HEADS * LANES)
    return jnp.concatenate([plain, rot], axis=1).astype(BF16)


def _widen_w_ukv(w):
    K = w.shape[0]
    e = w.reshape(K, MLA_HEADS, MLA_NOPE + MLA_V)
    kn = _pad_lanes(e[..., :MLA_NOPE], 0).reshape(K, MLA_HEADS * LANES)
    vv = _pad_lanes(e[..., MLA_NOPE:], 0).reshape(K, MLA_HEADS * LANES)
    return jnp.concatenate([kn, vv], axis=1).astype(BF16)


def kernel(x, mem, positions, attn_norm, w_in, mla_q_norm, mla_w_uq, mla_kv_norm, mla_w_ukv, diff_lambda_q1, diff_lambda_k1, diff_lambda_q2, diff_lambda_k2, diff_sub_norm, w_out, cross_norm, mem_norm, cross_wq, cross_wkv, cross_wo, mlp_norm, mlp_w1, mlp_w2, final_norm):
    B, S, _ = x.shape
    depth = w_in.shape[0]
    tm = 512
    t = 512
    tk = t // 2
    assert S % tm == 0 and tm % t == 0 and tk % MOBA_BLOCK == 0 and S // MOBA_BLOCK <= LANES - MOBA_SEL

    pos = positions.astype(jnp.int32)[..., None]
    rel = pos - pos[:, :1]
    half = MLA_ROPE // 2
    inv = ROPE_THETA ** (-jnp.arange(half, dtype=F32) / half)
    inv_lane = _pad_lanes(jnp.concatenate([inv, inv])[None, :], MLA_NOPE)
    row = lambda v: v.astype(F32)[None, :]

    h = x
    for l in range(depth):
        outs = _inproj(h, pos, rel, inv_lane, row(attn_norm[l]), _widen_w_in(w_in[l]), row(mla_q_norm[l]),
                       _widen_w_uq(mla_w_uq[l]), row(mla_kv_norm[l]), _widen_w_ukv(mla_w_ukv[l]), tm=tm, t=tk)
        qa, ka, va, qd, kd, vd, mq, mk, mv, kmean = outs
        km = kmean.reshape(B, S // MOBA_BLOCK, MOBA_HEADS, LANES).transpose(0, 2, 1, 3)
        km_pad = jnp.pad(km, ((0, 0), (0, 0), (MOBA_SEL, LANES - MOBA_SEL - S // MOBA_BLOCK), (0, 0)))
        mq_aug = _moba_gate(mq, km_pad, tm=tm)

        attn = functools.partial(_pair_attention, t=t, tk=tk)
        o_a = attn(_mla_attn_kernel, qa, ka, va, chains=2, name="mla_attn")
        lam_rows = jnp.stack([diff_lambda_q1[l], diff_lambda_k1[l], diff_lambda_q2[l], diff_lambda_k2[l]])
        lam_rows = jnp.pad(lam_rows.astype(F32), ((0, 4), (0, LANES - DIFF_HD)))
        gain = jnp.tile(diff_sub_norm[l].astype(F32), 2)[None, :]
        lam_init = 0.8 - 0.6 * math.exp(-0.3 * l)
        o_b = attn(functools.partial(_diff_attn_kernel, lam_init=lam_init), qd, kd, vd,
                   extra=(lam_rows, gain), chains=4, name="diff_attn")
        o_c = attn(_mla_attn_kernel, mq_aug, mk, mv, chains=2, name="moba_attn")

        memkv = _memkv(mem, row(mem_norm[l]), cross_wkv[l].astype(BF16))
        h = _mix_cross(h, o_a, o_b, o_c, w_out[l].astype(BF16), row(cross_norm[l]), cross_wq[l].astype(BF16),
                       memkv, cross_wo[l].astype(BF16), tm=tm)
        h = _mlp(h, row(mlp_norm[l]), mlp_w1[l].astype(BF16), mlp_w2[l].astype(BF16), row(final_norm),
                 tm=tm, final=(l == depth - 1))
    return h
```

```python
import functools
import math

import jax
import jax.numpy as jnp
import numpy as np
from jax import lax
from jax.experimental import pallas as pl
from jax.experimental.pallas import tpu as pltpu

D_MODEL = 1024
MLA_HEADS = 6
MLA_NOPE = 64
MLA_ROPE = 32
MLA_V = 64
MLA_Q_RANK = 256
MLA_KV_RANK = 128
ROPE_THETA = 10000.0
DIFF_HEADS = 6
DIFF_HD = 32
MOBA_HEADS = 4
MOBA_HD = 64
MOBA_BLOCK = 256
MOBA_TOPK = 3
CROSS_HEADS = 4
CROSS_HD = D_MODEL // CROSS_HEADS
D_FF = 4 * D_MODEL
EPS = 1e-6
NEG = -1e30
N_ALIBI = DIFF_HEADS + MOBA_HEADS
C_DIFF = DIFF_HEADS * 2 * DIFF_HD
C_MOBA = MOBA_HEADS * MOBA_HD

LANES = 128
HALF = LANES // 2
LOG2E = 1.4426950408889634
VMEM_LIMIT = 56 * 1024 * 1024

F32 = jnp.float32
BF16 = jnp.bfloat16

V_DIM = 64
V_ONE = V_DIM
V_ROWS = 80
DIFF_MAP2 = HALF
DIFF_BIAS = DIFF_HD
MOBA_BIAS = MOBA_HD
MOBA_SEL = 96
N_SPLIT = 3
UNROLL_CHAIN_PAIRS = 8

O_CQ = 0
O_CKV = O_CQ + MLA_Q_RANK
O_KR = O_CKV + MLA_KV_RANK
O_KRS = O_KR + LANES
O_DQ = O_KRS + LANES
O_DK = O_DQ + DIFF_HEADS * LANES
O_DV = O_DK + DIFF_HEADS * LANES
O_MQ = O_DV + DIFF_HEADS * LANES
O_MK = O_MQ + MOBA_HEADS * LANES
O_MV = O_MK + MOBA_HEADS * LANES
C_WIDE = O_MV + MOBA_HEADS * LANES


def _alibi_slope(h):
    return 2.0 ** (-8.0 * (h + 1) / N_ALIBI)


def _rms(x, g):
    return x * lax.rsqrt(jnp.mean(x * x, axis=-1, keepdims=True) + EPS) * g


def _dot(a, b):
    return jnp.dot(a, b, preferred_element_type=F32)


def _dot_nt(a, b):
    return lax.dot_general(a, b, (((1,), (1,)), ((), ())), preferred_element_type=F32)


def _dot_tn(a, b):
    return lax.dot_general(a, b, (((0,), (0,)), ((), ())), preferred_element_type=F32)


def _lane(shape):
    return lax.broadcasted_iota(jnp.int32, shape, len(shape) - 1)


def _onehot_lanes(lanes, value=1.0):
    l = _lane((1, LANES))
    out = jnp.zeros((1, LANES), F32)
    for i in lanes:
        out = jnp.where(l == i, value, out)
    return out


def _split3(x):
    hi = x.astype(BF16).astype(F32)
    r = x - hi
    mid = r.astype(BF16).astype(F32)
    lo = (r - mid).astype(BF16).astype(F32)
    return hi, mid, lo


def _store_vt(ref, h, v, t):
    for j in range(v.shape[0] // t):
        ref[0, h, j] = v[j * t:(j + 1) * t].T[:V_ROWS].astype(BF16)


def _inproj_kernel(h_ref, pos_ref, rel_ref, inv_ref, g_ref, w_ref, qn_ref, wuq_ref, kvn_ref, wukv_ref,
                   qa_ref, ka_ref, va_ref, qd_ref, kd_ref, vd_ref, mq_ref, mk_ref, mv_ref, kmean_ref,
                   *, tm, t):
    si = pl.program_id(1)
    nb = _rms(h_ref[0], g_ref[...]).astype(BF16)
    lane = _lane((1, LANES))
    v_one = jnp.where(lane == V_ONE, 1.0, 0.0)

    pm = _dot(nb, w_ref[:, O_CQ:O_DQ])
    cq = _rms(pm[:, O_CQ:O_CKV], qn_ref[...]).astype(BF16)
    ckv = _rms(pm[:, O_CKV:O_KR], kvn_ref[...]).astype(BF16)
    q2 = _dot(cq, wuq_ref[...])
    kv2 = _dot(ckv, wukv_ref[...])
    ang = pos_ref[0].astype(F32) * inv_ref[...]
    cs, sn = jnp.cos(ang), jnp.sin(ang)
    krope = pm[:, O_KR:O_KRS] * cs + pm[:, O_KRS:O_DQ] * sn
    qscale = (MLA_NOPE + MLA_ROPE) ** -0.5 * LOG2E
    hw = MLA_HEADS * LANES
    for h in range(MLA_HEADS):
        sl = slice(h * LANES, (h + 1) * LANES)
        sl2 = slice(hw + h * LANES, hw + (h + 1) * LANES)
        qa_ref[0, h] = ((q2[:, sl] * cs + q2[:, sl2] * sn) * qscale).T.astype(BF16)
        ka_ref[0, h] = (kv2[:, sl] + krope).astype(BF16)
        _store_vt(va_ref, h, kv2[:, sl2] + v_one, t)

    relf = rel_ref[0].astype(F32)

    pq = _dot(nb, w_ref[:, O_DQ:O_DK])
    pk = _dot(nb, w_ref[:, O_DK:O_DV])
    pv = _dot(nb, w_ref[:, O_DV:O_MQ])
    dscale = DIFF_HD ** -0.5 * LOG2E
    q_ones = _onehot_lanes([DIFF_BIAS + i for i in range(N_SPLIT)]
                           + [DIFF_MAP2 + DIFF_BIAS + i for i in range(N_SPLIT)])
    for h in range(DIFF_HEADS):
        sl = slice(h * LANES, (h + 1) * LANES)
        qd_ref[0, h] = (pq[:, sl] * dscale + q_ones).T.astype(BF16)
        kb = pk[:, sl]
        for i, piece in enumerate(_split3(relf * (_alibi_slope(h) * LOG2E))):
            kb = kb + piece * _onehot_lanes([DIFF_BIAS + i, DIFF_MAP2 + DIFF_BIAS + i])
        kd_ref[0, h] = kb.astype(BF16)
        _store_vt(vd_ref, h, pv[:, sl] + v_one, t)

    pq = _dot(nb, w_ref[:, O_MQ:O_MK])
    pk = _dot(nb, w_ref[:, O_MK:O_MV])
    pv = _dot(nb, w_ref[:, O_MV:C_WIDE])
    row = lax.broadcasted_iota(jnp.int32, (tm, 1), 0)
    blk = (si * tm + row) // MOBA_BLOCK
    blk_onehot = jnp.where(_lane((tm, LANES)) == MOBA_SEL + blk, 1.0, 0.0)
    for h in range(MOBA_HEADS):
        sl = slice(h * LANES, (h + 1) * LANES)
        mq_ref[0, h] = pq[:, sl]
        kb = pk[:, sl] + blk_onehot
        for i, piece in enumerate(_split3(relf * (_alibi_slope(DIFF_HEADS + h) * LOG2E))):
            kb = kb + piece * _onehot_lanes([MOBA_BIAS + i])
        mk_ref[0, h] = kb.astype(BF16)
        _store_vt(mv_ref, h, pv[:, sl] + v_one, t)
    for j in range(tm // MOBA_BLOCK):
        kmean_ref[0, j] = jnp.mean(pk[j * MOBA_BLOCK:(j + 1) * MOBA_BLOCK, :], axis=0, keepdims=True)


def _inproj(h, pos, rel, inv_lane, g, w_wide, qn, wuq, kvn, wukv, *, tm, t):
    B, S, _ = h.shape
    grid = (B, S // tm)
    tok = lambda b, i: (b, i, 0)
    const2 = lambda b, i: (0, 0)
    q_out = lambda nh: (jax.ShapeDtypeStruct((B, nh, LANES, S), BF16),
                        pl.BlockSpec((1, nh, LANES, tm), lambda b, i: (b, 0, 0, i)))
    k_out = lambda nh, dt: (jax.ShapeDtypeStruct((B, nh, S, LANES), dt),
                            pl.BlockSpec((1, nh, tm, LANES), lambda b, i: (b, 0, i, 0)))
    v_out = lambda nh: (jax.ShapeDtypeStruct((B, nh, S // t, V_ROWS, t), BF16),
                        pl.BlockSpec((1, nh, tm // t, V_ROWS, t), lambda b, i: (b, 0, i, 0, 0)))
    outs = [q_out(MLA_HEADS), k_out(MLA_HEADS, BF16), v_out(MLA_HEADS),
            q_out(DIFF_HEADS), k_out(DIFF_HEADS, BF16), v_out(DIFF_HEADS),
            k_out(MOBA_HEADS, F32), k_out(MOBA_HEADS, BF16), v_out(MOBA_HEADS),
            (jax.ShapeDtypeStruct((B, S // MOBA_BLOCK, 1, MOBA_HEADS * LANES), F32),
             pl.BlockSpec((1, tm // MOBA_BLOCK, 1, MOBA_HEADS * LANES), lambda b, i: (b, i, 0, 0)))]
    full = lambda a: pl.BlockSpec(a.shape, const2)
    return pl.pallas_call(
        functools.partial(_inproj_kernel, tm=tm, t=t),
        grid=grid,
        in_specs=[pl.BlockSpec((1, tm, D_MODEL), tok), pl.BlockSpec((1, tm, 1), tok),
                  pl.BlockSpec((1, tm, 1), tok), full(inv_lane), full(g), full(w_wide), full(qn),
                  full(wuq), full(kvn), full(wukv)],
        out_specs=[o[1] for o in outs],
        out_shape=[o[0] for o in outs],
        compiler_params=pltpu.CompilerParams(dimension_semantics=("parallel", "parallel"),
                                             vmem_limit_bytes=VMEM_LIMIT),
        name="inproj",
    )(h, pos, rel, inv_lane, g, w_wide, qn, wuq, kvn, wukv)


def _moba_gate_kernel(mq_ref, km_ref, o_ref, *, tm):
    si = pl.program_id(1)
    lane = _lane((tm, LANES))
    lanef = lane.astype(F32)
    row = lax.broadcasted_iota(jnp.int32, (tm, 1), 0)
    own = (si * tm + row) // MOBA_BLOCK
    valid = (lane >= MOBA_SEL) & (lane - MOBA_SEL < own)
    q_ones = _onehot_lanes([MOBA_BIAS + i for i in range(N_SPLIT)])
    for h in range(MOBA_HEADS):
        qf = mq_ref[0, h]
        gate = lax.dot_general(qf, km_ref[0, h], (((1,), (1,)), ((), ())),
                               precision=lax.Precision.HIGHEST, preferred_element_type=F32)
        g = jnp.where(valid, gate, NEG)
        sel = lane - MOBA_SEL == own
        for _ in range(MOBA_TOPK):
            mx = jnp.max(g, axis=-1, keepdims=True)
            idx = jnp.min(jnp.where(g == mx, lanef, float(LANES)), axis=-1, keepdims=True)
            pick = (lanef == idx) & (mx > 0.5 * NEG)
            sel = sel | pick
            g = jnp.where(pick, NEG, g)
        selbias = jnp.where((lane >= MOBA_SEL) & jnp.logical_not(sel), NEG, 0.0)
        o_ref[0, h] = (qf * (MOBA_HD ** -0.5 * LOG2E) + q_ones + selbias).T.astype(BF16)


def _moba_gate(mq, km_pad, *, tm):
    B, H, S, _ = mq.shape
    return pl.pallas_call(
        functools.partial(_moba_gate_kernel, tm=tm),
        grid=(B, S // tm),
        in_specs=[pl.BlockSpec((1, H, tm, LANES), lambda b, i: (b, 0, i, 0)),
                  pl.BlockSpec((1, H, LANES, LANES), lambda b, i: (b, 0, 0, 0))],
        out_specs=pl.BlockSpec((1, H, LANES, tm), lambda b, i: (b, 0, 0, i)),
        out_shape=jax.ShapeDtypeStruct((B, H, LANES, S), BF16),
        compiler_params=pltpu.CompilerParams(dimension_semantics=("parallel", "parallel"),
                                             vmem_limit_bytes=VMEM_LIMIT),
        name="moba_gate",
    )(mq, km_pad)


def _causal_flash(chains, k_ref, vt_ref, scratch, qi, t, tk):
    assert t == 2 * tk
    s_a, s_b, mb_a, mb_b, m_scr, acc_scr = scratch
    nc = len(chains)

    every, first, second = slice(0, t), slice(0, tk), slice(tk, t)

    def scores(kt, s_dst, mb_dst, qs=every):
        tiles = {}
        for c, (hh, qt) in enumerate(chains):
            if hh not in tiles:
                tiles[hh] = k_ref[0, hh, pl.ds(pl.multiple_of(kt * tk, tk), tk), :]
            s = _dot(tiles[hh], qt[:, qs])
            s_dst[c, :, qs] = s
            mb_dst[c, :, qs] = jnp.max(s, axis=0, keepdims=True)

    def softmax_pv(kt, s_src, mb_src, qs=every, diagonal=False):
        for c, (hh, _) in enumerate(chains):
            s = s_src[c, :, qs]
            if diagonal:
                keep = (lax.broadcasted_iota(jnp.int32, (tk, tk), 0)
                        <= lax.broadcasted_iota(jnp.int32, (tk, tk), 1))
                s = jnp.where(keep, s, NEG)
                mb = jnp.max(s, axis=0, keepdims=True)
            else:
                mb = mb_src[c, :, qs]
            m = m_scr[c, :, qs]
            m_new = jnp.maximum(m, mb)
            p = jnp.exp2(s - m_new).astype(BF16)
            acc_scr[c, :, qs] = acc_scr[c, :, qs] * jnp.exp2(m - m_new) + _dot(vt_ref[0, hh, kt], p)
            m_scr[c, :, qs] = m_new

    for c in range(nc):
        m_scr[c] = jnp.full((1, t), NEG, F32)
        acc_scr[c] = jnp.zeros((V_ROWS, t), F32)
    scores(0, s_a, mb_a)

    def pair(j, carry):
        scores(2 * j + 1, s_b, mb_b)
        softmax_pv(2 * j, s_a, mb_a)
        scores(2 * j + 2, s_a, mb_a)
        softmax_pv(2 * j + 1, s_b, mb_b)
        return carry

    unroll = UNROLL_CHAIN_PAIRS // nc

    def pairs(i, carry):
        for u in range(unroll):
            pair(unroll * i + u, carry)
        return carry

    n_main = qi // unroll
    lax.fori_loop(0, n_main, pairs, 0)
    lax.fori_loop(n_main * unroll, qi, pair, 0)
    scores(2 * qi + 1, s_b, mb_b, second)
    softmax_pv(2 * qi, s_a, mb_a, first, diagonal=True)
    softmax_pv(2 * qi, s_a, mb_a, second)
    softmax_pv(2 * qi + 1, s_b, mb_b, second, diagonal=True)
    outs = []
    for c in range(nc):
        acc = acc_scr[c]
        outs.append(acc[:V_DIM] * (1.0 / acc[V_ONE:V_ONE + 1]))
    return outs


def _flash_scratch(nc, t, tk):
    return ([pltpu.VMEM((nc, tk, t), F32)] * 2 + [pltpu.VMEM((nc, 1, t), F32)] * 3
            + [pltpu.VMEM((nc, V_ROWS, t), F32)])


def _mla_attn_kernel(q_ref, k_ref, vt_ref, o_ref, *scratch, t, tk):
    o0, o1 = _causal_flash([(0, q_ref[0, 0]), (1, q_ref[0, 1])], k_ref, vt_ref, scratch, pl.program_id(2), t, tk)
    o_ref[0] = jnp.concatenate([o0, o1], axis=0).astype(o_ref.dtype)


def _diff_attn_kernel(lam_ref, gain_ref, q_ref, k_ref, vt_ref, o_ref, *scratch, t, tk, lam_init):
    lv = lam_ref[...]
    lam = (jnp.exp(jnp.sum(lv[0:1] * lv[1:2], axis=-1, keepdims=True))
           - jnp.exp(jnp.sum(lv[2:3] * lv[3:4], axis=-1, keepdims=True)) + lam_init)
    feat = lax.broadcasted_iota(jnp.int32, (LANES, t), 0)
    chains = []
    for hh in range(2):
        q = q_ref[0, hh]
        chains.append((hh, jnp.where(feat < DIFF_MAP2, q, jnp.zeros_like(q))))
        chains.append((hh, jnp.where(feat >= DIFF_MAP2, q, jnp.zeros_like(q))))
    a0, b0, a1, b1 = _causal_flash(chains, k_ref, vt_ref, scratch, pl.program_id(2), t, tk)
    outs = []
    for a, b in ((a0, b0), (a1, b1)):
        o = a - lam * b
        outs.append(o * lax.rsqrt(jnp.mean(o * o, axis=0, keepdims=True) + EPS))
    o_ref[0] = (jnp.concatenate(outs, axis=0) * gain_ref[...] * (1.0 - lam_init)).astype(o_ref.dtype)


def _pair_attention(kernel_fn, q, k, vt, extra=(), *, t, tk, chains, name):
    B, H, _, S = q.shape
    qspec = pl.BlockSpec((1, 2, LANES, t), lambda b, p, i: (b, p, 0, i))
    kspec = pl.BlockSpec((1, 2, S, LANES), lambda b, p, i: (b, p, 0, 0))
    vspec = pl.BlockSpec((1, 2, S // tk, V_ROWS, tk), lambda b, p, i: (b, p, 0, 0, 0))
    xspecs = [pl.BlockSpec(a.shape, lambda b, p, i: (0, 0)) for a in extra]
    return pl.pallas_call(
        functools.partial(kernel_fn, t=t, tk=tk),
        grid=(B, H // 2, S // t),
        in_specs=xspecs + [qspec, kspec, vspec],
        out_specs=pl.BlockSpec((1, LANES, t), lambda b, p, i: (b, p, i)),
        out_shape=jax.ShapeDtypeStruct((B, (H // 2) * LANES, S), BF16),
        scratch_shapes=_flash_scratch(chains, t, tk),
        compiler_params=pltpu.CompilerParams(dimension_semantics=("parallel", "parallel", "parallel"),
                                             vmem_limit_bytes=VMEM_LIMIT),
        name=name,
    )(*extra, q, k, vt)


def _memkv_kernel(x_ref, g_ref, w_ref, o_ref):
    o_ref[0] = _dot(_rms(x_ref[0], g_ref[...]).astype(BF16), w_ref[...]).astype(o_ref.dtype)


def _memkv(mem, g, wkv):
    B, M, _ = mem.shape
    return pl.pallas_call(
        _memkv_kernel,
        grid=(B,),
        in_specs=[pl.BlockSpec((1, M, D_MODEL), lambda b: (b, 0, 0)), pl.BlockSpec(g.shape, lambda b: (0, 0)),
                  pl.BlockSpec(wkv.shape, lambda b: (0, 0))],
        out_specs=pl.BlockSpec((1, M, 2 * D_MODEL), lambda b: (b, 0, 0)),
        out_shape=jax.ShapeDtypeStruct((B, M, 2 * D_MODEL), BF16),
        compiler_params=pltpu.CompilerParams(dimension_semantics=("parallel",), vmem_limit_bytes=VMEM_LIMIT),
        name="memkv",
    )(mem, g, wkv)


def _mix_cross_kernel(h_ref, oa_ref, ob_ref, oc_ref, wout_ref, g_ref, wq_ref, kv_ref, wo_ref, o_ref):
    na, nb = oa_ref.shape[1], ob_ref.shape[1]
    h1 = (h_ref[0] + _dot_tn(oa_ref[0], wout_ref[0:na]) + _dot_tn(ob_ref[0], wout_ref[na:na + nb])
          + _dot_tn(oc_ref[0], wout_ref[na + nb:]))
    n = _rms(h1, g_ref[...]).astype(BF16)
    q = (_dot(n, wq_ref[...]) * (CROSS_HD ** -0.5 * LOG2E)).astype(BF16)
    ctx = []
    for h in range(CROSS_HEADS):
        sl = slice(h * CROSS_HD, (h + 1) * CROSS_HD)
        s = _dot_nt(q[:, sl], kv_ref[0, :, sl])
        p = jnp.exp2(s - jnp.max(s, axis=-1, keepdims=True))
        l = jnp.sum(p, axis=-1, keepdims=True)
        c = _dot(p.astype(BF16), kv_ref[0, :, D_MODEL + h * CROSS_HD:D_MODEL + (h + 1) * CROSS_HD])
        ctx.append((c * (1.0 / l)).astype(BF16))
    o_ref[0] = h1 + _dot(jnp.concatenate(ctx, axis=-1), wo_ref[...])


def _mix_cross(h, oa, ob, oc, wout, g, wq, memkv, wo, *, tm):
    B, S, _ = h.shape
    tok = lambda b, i: (b, i, 0)
    const2 = lambda b, i: (0, 0)
    full = lambda a: pl.BlockSpec(a.shape, const2)
    return pl.pallas_call(
        _mix_cross_kernel,
        grid=(B, S // tm),
        in_specs=[pl.BlockSpec((1, tm, D_MODEL), tok)]
                 + [pl.BlockSpec((1, o.shape[1], tm), lambda b, i: (b, 0, i)) for o in (oa, ob, oc)] + [
                  full(wout), full(g), full(wq),
                  pl.BlockSpec((1,) + memkv.shape[1:], lambda b, i: (b, 0, 0)), full(wo)],
        out_specs=pl.BlockSpec((1, tm, D_MODEL), tok),
        out_shape=jax.ShapeDtypeStruct(h.shape, F32),
        compiler_params=pltpu.CompilerParams(dimension_semantics=("parallel", "parallel"),
                                             vmem_limit_bytes=VMEM_LIMIT),
        name="mix_cross",
    )(h, oa, ob, oc, wout, g, wq, memkv, wo)


def _mlp_kernel(h_ref, g_ref, w1_ref, w2_ref, gf_ref, o_ref, *, final, chunk):
    h = h_ref[0]
    n = _rms(h, g_ref[...]).astype(BF16)
    acc = h
    for c in range(D_FF // chunk):
        a = jnp.maximum(_dot(n, w1_ref[:, c * chunk:(c + 1) * chunk]), 0.0)
        acc = acc + _dot((a * a).astype(BF16), w2_ref[c * chunk:(c + 1) * chunk, :])
    o_ref[0] = _rms(acc, gf_ref[...]) if final else acc


def _mlp(h, g, w1, w2, gf, *, tm, final):
    B, S, _ = h.shape
    tok = lambda b, i: (b, i, 0)
    const2 = lambda b, i: (0, 0)
    resident = lambda a: pl.BlockSpec(a.shape, const2, pipeline_mode=pl.Buffered(1))
    return pl.pallas_call(
        functools.partial(_mlp_kernel, final=final, chunk=D_MODEL),
        grid=(B, S // tm),
        in_specs=[pl.BlockSpec((1, tm, D_MODEL), tok), pl.BlockSpec(g.shape, const2), resident(w1), resident(w2),
                  pl.BlockSpec(gf.shape, const2)],
        out_specs=pl.BlockSpec((1, tm, D_MODEL), tok),
        out_shape=jax.ShapeDtypeStruct(h.shape, F32),
        compiler_params=pltpu.CompilerParams(dimension_semantics=("parallel", "parallel"),
                                             vmem_limit_bytes=VMEM_LIMIT),
        name="mlp",
    )(h, g, w1, w2, gf)


def _rot_pairs(w):
    half = w.shape[-1] // 2
    return jnp.concatenate([-w[..., half:], w[..., :half]], axis=-1)


def _pad_lanes(w, lo, width=LANES):
    pad = [(0, 0)] * (w.ndim - 1) + [(lo, width - lo - w.shape[-1])]
    return jnp.pad(w, pad)


def _head_groups(w, nheads, hd, lo=0, width=LANES):
    K = w.shape[0]
    return _pad_lanes(w.reshape(K, nheads, hd), lo, width).reshape(K, nheads * width)


def _widen_w_in(w):
    cuts = np.cumsum([0, MLA_Q_RANK, MLA_KV_RANK, MLA_ROPE, C_DIFF, C_DIFF, C_DIFF, C_MOBA, C_MOBA, C_MOBA])
    cq, ckv, kr, dq, dk, dv, mq, mk, mv = [w[:, int(a):int(b)] for a, b in zip(cuts[:-1], cuts[1:])]
    two_maps = lambda x: _head_groups(x, 2 * DIFF_HEADS, DIFF_HD, 0, HALF)
    return jnp.concatenate([
        cq, ckv, _pad_lanes(kr, MLA_NOPE), _pad_lanes(_rot_pairs(kr), MLA_NOPE),
        two_maps(dq), two_maps(dk), _head_groups(dv, DIFF_HEADS, 2 * DIFF_HD),
        _head_groups(mq, MOBA_HEADS, MOBA_HD), _head_groups(mk, MOBA_HEADS, MOBA_HD),
        _head_groups(mv, MOBA_HEADS, MOBA_HD)], axis=1).astype(BF16)


def _widen_w_uq(w):
    K = w.shape[0]
    e = w.reshape(K, MLA_HEADS, MLA_NOPE + MLA_ROPE)
    plain = _pad_lanes(e, 0).reshape(K, MLA_HEADS * LANES)
    rot = _pad_lanes(_rot_pairs(e[..., MLA_NOPE:]), MLA_NOPE).reshape(K, MLA_HEADS * LANES)
    return jnp.concatenate([plain, rot], axis=1).astype(BF16)


def _widen_w_ukv(w):
    K = w.shape[0]
    e = w.reshape(K, MLA_HEADS, MLA_NOPE + MLA_V)
    kn = _pad_lanes(e[..., :MLA_NOPE], 0).reshape(K, MLA_HEADS * LANES)
    vv = _pad_lanes(e[..., MLA_NOPE:], 0).reshape(K, MLA_HEADS * LANES)
    return jnp.concatenate([kn, vv], axis=1).astype(BF16)


def kernel(x, mem, positions, attn_norm, w_in, mla_q_norm, mla_w_uq, mla_kv_norm, mla_w_ukv, diff_lambda_q1, diff_lambda_k1, diff_lambda_q2, diff_lambda_k2, diff_sub_norm, w_out, cross_norm, mem_norm, cross_wq, cross_wkv, cross_wo, mlp_norm, mlp_w1, mlp_w2, final_norm):
    B, S, _ = x.shape
    depth = w_in.shape[0]
    tm = 512
    t = 512
    tk = t // 2
    assert S % tm == 0 and tm % t == 0 and tk % MOBA_BLOCK == 0 and S // MOBA_BLOCK <= LANES - MOBA_SEL

    pos = positions.astype(jnp.int32)[..., None]
    rel = pos - pos[:, :1]
    half = MLA_ROPE // 2
    inv = ROPE_THETA ** (-jnp.arange(half, dtype=F32) / half)
    inv_lane = _pad_lanes(jnp.concatenate([inv, inv])[None, :], MLA_NOPE)
    row = lambda v: v.astype(F32)[None, :]

    h = x
    for l in range(depth):
        outs = _inproj(h, pos, rel, inv_lane, row(attn_norm[l]), _widen_w_in(w_in[l]), row(mla_q_norm[l]),
                       _widen_w_uq(mla_w_uq[l]), row(mla_kv_norm[l]), _widen_w_ukv(mla_w_ukv[l]), tm=tm, t=tk)
        qa, ka, va, qd, kd, vd, mq, mk, mv, kmean = outs
        km = kmean.reshape(B, S // MOBA_BLOCK, MOBA_HEADS, LANES).transpose(0, 2, 1, 3)
        km_pad = jnp.pad(km, ((0, 0), (0, 0), (MOBA_SEL, LANES - MOBA_SEL - S // MOBA_BLOCK), (0, 0)))
        mq_aug = _moba_gate(mq, km_pad, tm=tm)

        attn = functools.partial(_pair_attention, t=t, tk=tk)
        o_a = attn(_mla_attn_kernel, qa, ka, va, chains=2, name="mla_attn")
        lam_rows = jnp.stack([diff_lambda_q1[l], diff_lambda_k1[l], diff_lambda_q2[l], diff_lambda_k2[l]])
        lam_rows = jnp.pad(lam_rows.astype(F32), ((0, 4), (0, LANES - DIFF_HD)))
        gain = jnp.tile(diff_sub_norm[l].astype(F32), 2)[:, None]
        lam_init = 0.8 - 0.6 * math.exp(-0.3 * l)
        o_b = attn(functools.partial(_diff_attn_kernel, lam_init=lam_init), qd, kd, vd,
                   extra=(lam_rows, gain), chains=4, name="diff_attn")
        o_c = attn(_mla_attn_kernel, mq_aug, mk, mv, chains=2, name="moba_attn")

        memkv = _memkv(mem, row(mem_norm[l]), cross_wkv[l].astype(BF16))
        h = _mix_cross(h, o_a, o_b, o_c, w_out[l].astype(BF16), row(cross_norm[l]), cross_wq[l].astype(BF16),
                       memkv, cross_wo[l].astype(BF16), tm=tm)
        h = _mlp(h, row(mlp_norm[l]), mlp_w1[l].astype(BF16), mlp_w2[l].astype(BF16), row(final_norm),
                 tm=tm, final=(l == depth - 1))
    return h
```

```python
import functools
import math

import jax
import jax.numpy as jnp
import numpy as np
from jax import lax
from jax.experimental import pallas as pl
from jax.experimental.pallas import tpu as pltpu

D_MODEL = 1024
MLA_HEADS = 6
MLA_NOPE = 64
MLA_ROPE = 32
MLA_V = 64
MLA_Q_RANK = 256
MLA_KV_RANK = 128
ROPE_THETA = 10000.0
DIFF_HEADS = 6
DIFF_HD = 32
MOBA_HEADS = 4
MOBA_HD = 64
MOBA_BLOCK = 256
MOBA_TOPK = 3
CROSS_HEADS = 4
CROSS_HD = D_MODEL // CROSS_HEADS
D_FF = 4 * D_MODEL
EPS = 1e-6
NEG = -1e30
N_ALIBI = DIFF_HEADS + MOBA_HEADS
C_DIFF = DIFF_HEADS * 2 * DIFF_HD
C_MOBA = MOBA_HEADS * MOBA_HD

LANES = 128
HALF = LANES // 2
LOG2E = 1.4426950408889634
VMEM_LIMIT = 56 * 1024 * 1024

F32 = jnp.float32
BF16 = jnp.bfloat16

V_DIM = 64
V_ONE = V_DIM
V_ROWS = 80
DIFF_MAP2 = HALF
DIFF_BIAS = DIFF_HD
MOBA_BIAS = MOBA_HD
MOBA_SEL = 96
N_SPLIT = 3
UNROLL_CHAIN_PAIRS = 8

O_CQ = 0
O_CKV = O_CQ + MLA_Q_RANK
O_KR = O_CKV + MLA_KV_RANK
O_KRS = O_KR + LANES
O_DQ = O_KRS + LANES
O_DK = O_DQ + C_DIFF
O_DV = O_DK + DIFF_HEADS * LANES
O_MQ = O_DV + C_DIFF
O_MK = O_MQ + C_MOBA
O_MV = O_MK + MOBA_HEADS * LANES
C_WIDE = O_MV + C_MOBA
UQ_NOPE = 0
UQ_ROPE = MLA_HEADS * MLA_NOPE
UQ_ROT = UQ_ROPE + 2 * LANES
UQ_COLS = UQ_ROT + 2 * LANES
UKV_V = MLA_HEADS * LANES
UKV_COLS = UKV_V + MLA_HEADS * MLA_V


def _alibi_slope(h):
    return 2.0 ** (-8.0 * (h + 1) / N_ALIBI)


def _rms(x, g):
    return x * lax.rsqrt(jnp.mean(x * x, axis=-1, keepdims=True) + EPS) * g


def _dot(a, b):
    return jnp.dot(a, b, preferred_element_type=F32)


def _dot_nt(a, b):
    return lax.dot_general(a, b, (((1,), (1,)), ((), ())), preferred_element_type=F32)


def _dot_tn(a, b):
    return lax.dot_general(a, b, (((0,), (0,)), ((), ())), preferred_element_type=F32)


def _lane(shape):
    return lax.broadcasted_iota(jnp.int32, shape, len(shape) - 1)


def _onehot_lanes(lanes, value=1.0):
    l = _lane((1, LANES))
    out = jnp.zeros((1, LANES), F32)
    for i in lanes:
        out = jnp.where(l == i, value, out)
    return out


def _split3(x):
    hi = x.astype(BF16).astype(F32)
    r = x - hi
    mid = r.astype(BF16).astype(F32)
    lo = (r - mid).astype(BF16).astype(F32)
    return hi, mid, lo


def _store_vt(ref, h, vt, t):
    tm = vt.shape[1]
    tail = jnp.where(lax.broadcasted_iota(jnp.int32, (V_ROWS - V_DIM, tm), 0) == 0, 1.0, 0.0)
    x = jnp.concatenate([vt, tail], axis=0).astype(BF16)
    for j in range(tm // t):
        ref[0, h, j] = x[:, j * t:(j + 1) * t]


def _bias_rows(n, tm):
    return jnp.where(lax.broadcasted_iota(jnp.int32, (n, tm), 0) < N_SPLIT, 1.0, 0.0)


def _inproj_kernel(h_ref, pos_ref, rel_ref, inv_ref, g_ref, w_ref, qn_ref, wuq_ref, kvn_ref, wukv_ref,
                   qa_ref, ka_ref, va_ref, qd_ref, kd_ref, vd_ref, mq_ref, mk_ref, mv_ref, kmean_ref,
                   *, tm, t):
    si = pl.program_id(1)
    nb = _rms(h_ref[0], g_ref[...]).astype(BF16)
    group = lambda x, o, i: x[:, o + i * LANES:o + (i + 1) * LANES]

    pm = _dot(nb, w_ref[:, O_CQ:O_DQ])
    cq = _rms(pm[:, O_CQ:O_CKV], qn_ref[...]).astype(BF16)
    ckv = _rms(pm[:, O_CKV:O_KR], kvn_ref[...]).astype(BF16)
    q2 = _dot(cq, wuq_ref[...])
    kv2 = _dot(ckv, wukv_ref[...])
    ang = inv_ref[...] * pos_ref[0].astype(F32)
    reps = LANES // ang.shape[0]
    cs = jnp.concatenate([jnp.cos(ang)] * reps, axis=0).T
    sn = jnp.concatenate([jnp.sin(ang)] * reps, axis=0).T
    krope = pm[:, O_KR:O_KRS] * cs + pm[:, O_KRS:O_DQ] * sn
    qscale = (MLA_NOPE + MLA_ROPE) ** -0.5 * LOG2E
    rope_t = [(group(q2, UQ_ROPE, i) * cs + group(q2, UQ_ROT, i) * sn).T for i in range(2)]
    pad_rows = jnp.zeros((LANES - MLA_NOPE - MLA_ROPE, tm), F32)
    for p in range(MLA_HEADS // 2):
        nope_t = group(q2, UQ_NOPE, p).T
        v_t = group(kv2, UKV_V, p).T
        for e in range(2):
            h = 2 * p + e
            r = (h % 4) * MLA_ROPE
            qa = jnp.concatenate([nope_t[e * HALF:(e + 1) * HALF], rope_t[h // 4][r:r + MLA_ROPE], pad_rows], axis=0)
            qa_ref[0, h] = (qa * qscale).astype(BF16)
            _store_vt(va_ref, h, v_t[e * HALF:(e + 1) * HALF], t)
    for h in range(MLA_HEADS):
        ka_ref[0, h] = (group(kv2, 0, h) + krope).astype(BF16)

    relf = rel_ref[0].astype(F32)

    pq = _dot(nb, w_ref[:, O_DQ:O_DK])
    pk = _dot(nb, w_ref[:, O_DK:O_DV])
    pv = _dot(nb, w_ref[:, O_DV:O_MQ])
    dscale = DIFF_HD ** -0.5 * LOG2E
    ones_rows = _bias_rows(DIFF_MAP2 - DIFF_HD, tm)
    for p in range(DIFF_HEADS // 2):
        q_t = group(pq, 0, p).T * dscale
        v_t = group(pv, 0, p).T
        for e in range(2):
            o = e * HALF
            qd_ref[0, 2 * p + e] = jnp.concatenate(
                [q_t[o:o + DIFF_HD], ones_rows, q_t[o + DIFF_HD:o + HALF], ones_rows], axis=0).astype(BF16)
            _store_vt(vd_ref, 2 * p + e, v_t[o:o + HALF], t)
    for h in range(DIFF_HEADS):
        kb = group(pk, 0, h)
        for i, piece in enumerate(_split3(relf * (_alibi_slope(h) * LOG2E))):
            kb = kb + piece * _onehot_lanes([DIFF_BIAS + i, DIFF_MAP2 + DIFF_BIAS + i])
        kd_ref[0, h] = kb.astype(BF16)

    pq = _dot(nb, w_ref[:, O_MQ:O_MK])
    pk = _dot(nb, w_ref[:, O_MK:O_MV])
    pv = _dot(nb, w_ref[:, O_MV:C_WIDE])
    for p in range(MOBA_HEADS // 2):
        q_t = group(pq, 0, p).T
        v_t = group(pv, 0, p).T
        for e in range(2):
            mq_ref[0, 2 * p + e] = q_t[e * HALF:(e + 1) * HALF]
            _store_vt(mv_ref, 2 * p + e, v_t[e * HALF:(e + 1) * HALF], t)
    row = lax.broadcasted_iota(jnp.int32, (tm, 1), 0)
    blk = (si * tm + row) // MOBA_BLOCK
    blk_onehot = jnp.where(_lane((tm, LANES)) == MOBA_SEL + blk, 1.0, 0.0)
    for h in range(MOBA_HEADS):
        kb = group(pk, 0, h) + blk_onehot
        for i, piece in enumerate(_split3(relf * (_alibi_slope(DIFF_HEADS + h) * LOG2E))):
            kb = kb + piece * _onehot_lanes([MOBA_BIAS + i])
        mk_ref[0, h] = kb.astype(BF16)
    for j in range(tm // MOBA_BLOCK):
        kmean_ref[0, j] = jnp.mean(pk[j * MOBA_BLOCK:(j + 1) * MOBA_BLOCK, :], axis=0, keepdims=True)


def _inproj(h, pos, rel, inv_lane, g, w_wide, qn, wuq, kvn, wukv, *, tm, t):
    B, S, _ = h.shape
    grid = (B, S // tm)
    tok = lambda b, i: (b, i, 0)
    const2 = lambda b, i: (0, 0)
    q_out = lambda nh: (jax.ShapeDtypeStruct((B, nh, LANES, S), BF16),
                        pl.BlockSpec((1, nh, LANES, tm), lambda b, i: (b, 0, 0, i)))
    k_out = lambda nh, dt: (jax.ShapeDtypeStruct((B, nh, S, LANES), dt),
                            pl.BlockSpec((1, nh, tm, LANES), lambda b, i: (b, 0, i, 0)))
    v_out = lambda nh: (jax.ShapeDtypeStruct((B, nh, S // t, V_ROWS, t), BF16),
                        pl.BlockSpec((1, nh, tm // t, V_ROWS, t), lambda b, i: (b, 0, i, 0, 0)))
    outs = [q_out(MLA_HEADS), k_out(MLA_HEADS, BF16), v_out(MLA_HEADS),
            q_out(DIFF_HEADS), k_out(DIFF_HEADS, BF16), v_out(DIFF_HEADS),
            (jax.ShapeDtypeStruct((B, MOBA_HEADS, MOBA_HD, S), F32),
             pl.BlockSpec((1, MOBA_HEADS, MOBA_HD, tm), lambda b, i: (b, 0, 0, i))),
            k_out(MOBA_HEADS, BF16), v_out(MOBA_HEADS),
            (jax.ShapeDtypeStruct((B, S // MOBA_BLOCK, 1, MOBA_HEADS * LANES), F32),
             pl.BlockSpec((1, tm // MOBA_BLOCK, 1, MOBA_HEADS * LANES), lambda b, i: (b, i, 0, 0)))]
    full = lambda a: pl.BlockSpec(a.shape, const2)
    return pl.pallas_call(
        functools.partial(_inproj_kernel, tm=tm, t=t),
        grid=grid,
        in_specs=[pl.BlockSpec((1, tm, D_MODEL), tok), pl.BlockSpec((1, 1, tm), lambda b, i: (b, 0, i)),
                  pl.BlockSpec((1, tm, 1), tok), full(inv_lane), full(g), full(w_wide), full(qn),
                  full(wuq), full(kvn), full(wukv)],
        out_specs=[o[1] for o in outs],
        out_shape=[o[0] for o in outs],
        compiler_params=pltpu.CompilerParams(dimension_semantics=("parallel", "parallel"),
                                             vmem_limit_bytes=VMEM_LIMIT),
        name="inproj",
    )(h, pos, rel, inv_lane, g, w_wide, qn, wuq, kvn, wukv)


def _moba_gate_kernel(mq_ref, km_ref, o_ref, *, tm):
    si = pl.program_id(1)
    nblk = LANES - MOBA_SEL
    blk = lax.broadcasted_iota(jnp.int32, (nblk, tm), 0)
    blkf = blk.astype(F32)
    own = (si * tm + _lane((1, tm))) // MOBA_BLOCK
    valid = blk < own
    ones_rows = _bias_rows(MOBA_SEL - MOBA_HD, tm)
    for h in range(MOBA_HEADS):
        qt = mq_ref[0, h]
        gate = jnp.dot(km_ref[0, h], qt, precision=lax.Precision.HIGHEST, preferred_element_type=F32)
        g = jnp.where(valid, gate, NEG)
        sel = blk == own
        for _ in range(MOBA_TOPK):
            mx = jnp.max(g, axis=0, keepdims=True)
            idx = jnp.min(jnp.where(g == mx, blkf, float(nblk)), axis=0, keepdims=True)
            pick = (blkf == idx) & (mx > 0.5 * NEG)
            sel = sel | pick
            g = jnp.where(pick, NEG, g)
        selbias = jnp.where(sel, 0.0, NEG)
        o_ref[0, h] = jnp.concatenate([qt * (MOBA_HD ** -0.5 * LOG2E), ones_rows, selbias],
                                      axis=0).astype(BF16)


def _moba_gate(mq, km_pad, *, tm):
    B, H, _, S = mq.shape
    return pl.pallas_call(
        functools.partial(_moba_gate_kernel, tm=tm),
        grid=(B, S // tm),
        in_specs=[pl.BlockSpec((1, H, MOBA_HD, tm), lambda b, i: (b, 0, 0, i)),
                  pl.BlockSpec((1, H, LANES - MOBA_SEL, MOBA_HD), lambda b, i: (b, 0, 0, 0))],
        out_specs=pl.BlockSpec((1, H, LANES, tm), lambda b, i: (b, 0, 0, i)),
        out_shape=jax.ShapeDtypeStruct((B, H, LANES, S), BF16),
        compiler_params=pltpu.CompilerParams(dimension_semantics=("parallel", "parallel"),
                                             vmem_limit_bytes=VMEM_LIMIT),
        name="moba_gate",
    )(mq, km_pad)


def _causal_flash(chains, k_ref, vt_ref, scratch, qi, t, tk):
    assert t == 2 * tk
    s_a, s_b, mb_a, mb_b, m_scr, acc_scr = scratch
    nc = len(chains)

    every, first, second = slice(0, t), slice(0, tk), slice(tk, t)

    def scores(kt, s_dst, mb_dst, qs=every):
        tiles = {}
        for c, (hh, qt) in enumerate(chains):
            if hh not in tiles:
                tiles[hh] = k_ref[0, hh, pl.ds(pl.multiple_of(kt * tk, tk), tk), :]
            s = _dot(tiles[hh], qt[:, qs])
            s_dst[c, :, qs] = s
            mb_dst[c, :, qs] = jnp.max(s, axis=0, keepdims=True)

    def softmax_pv(kt, s_src, mb_src, qs=every, diagonal=False):
        for c, (hh, _) in enumerate(chains):
            s = s_src[c, :, qs]
            if diagonal:
                keep = (lax.broadcasted_iota(jnp.int32, (tk, tk), 0)
                        <= lax.broadcasted_iota(jnp.int32, (tk, tk), 1))
                s = jnp.where(keep, s, NEG)
                mb = jnp.max(s, axis=0, keepdims=True)
            else:
                mb = mb_src[c, :, qs]
            m = m_scr[c, :, qs]
            m_new = jnp.maximum(m, mb)
            p = jnp.exp2(s - m_new).astype(BF16)
            acc_scr[c, :, qs] = acc_scr[c, :, qs] * jnp.exp2(m - m_new) + _dot(vt_ref[0, hh, kt], p)
            m_scr[c, :, qs] = m_new

    for c in range(nc):
        m_scr[c] = jnp.full((1, t), NEG, F32)
        acc_scr[c] = jnp.zeros((V_ROWS, t), F32)
    scores(0, s_a, mb_a)

    def pair(j, carry):
        scores(2 * j + 1, s_b, mb_b)
        softmax_pv(2 * j, s_a, mb_a)
        scores(2 * j + 2, s_a, mb_a)
        softmax_pv(2 * j + 1, s_b, mb_b)
        return carry

    unroll = UNROLL_CHAIN_PAIRS // nc

    def pairs(i, carry):
        for u in range(unroll):
            pair(unroll * i + u, carry)
        return carry

    n_main = qi // unroll
    lax.fori_loop(0, n_main, pairs, 0)
    lax.fori_loop(n_main * unroll, qi, pair, 0)
    scores(2 * qi + 1, s_b, mb_b, second)
    softmax_pv(2 * qi, s_a, mb_a, first, diagonal=True)
    softmax_pv(2 * qi, s_a, mb_a, second)
    softmax_pv(2 * qi + 1, s_b, mb_b, second, diagonal=True)
    outs = []
    for c in range(nc):
        acc = acc_scr[c]
        outs.append(acc[:V_DIM] * (1.0 / acc[V_ONE:V_ONE + 1]))
    return outs


def _flash_scratch(nc, t, tk):
    return ([pltpu.VMEM((nc, tk, t), F32)] * 2 + [pltpu.VMEM((nc, 1, t), F32)] * 3
            + [pltpu.VMEM((nc, V_ROWS, t), F32)])


def _mla_attn_kernel(q_ref, k_ref, vt_ref, o_ref, *scratch, t, tk):
    o0, o1 = _causal_flash([(0, q_ref[0, 0]), (1, q_ref[0, 1])], k_ref, vt_ref, scratch, pl.program_id(2), t, tk)
    o_ref[0] = jnp.concatenate([o0, o1], axis=0).astype(o_ref.dtype)


def _diff_attn_kernel(lam_ref, gain_ref, q_ref, k_ref, vt_ref, o_ref, *scratch, t, tk, lam_init):
    lv = lam_ref[...]
    lam = (jnp.exp(jnp.sum(lv[0:1] * lv[1:2], axis=-1, keepdims=True))
           - jnp.exp(jnp.sum(lv[2:3] * lv[3:4], axis=-1, keepdims=True)) + lam_init)
    feat = lax.broadcasted_iota(jnp.int32, (LANES, t), 0)
    chains = []
    for hh in range(2):
        q = q_ref[0, hh]
        chains.append((hh, jnp.where(feat < DIFF_MAP2, q, jnp.zeros_like(q))))
        chains.append((hh, jnp.where(feat >= DIFF_MAP2, q, jnp.zeros_like(q))))
    a0, b0, a1, b1 = _causal_flash(chains, k_ref, vt_ref, scratch, pl.program_id(2), t, tk)
    outs = []
    for a, b in ((a0, b0), (a1, b1)):
        o = a - lam * b
        outs.append(o * lax.rsqrt(jnp.mean(o * o, axis=0, keepdims=True) + EPS))
    o_ref[0] = (jnp.concatenate(outs, axis=0) * gain_ref[...] * (1.0 - lam_init)).astype(o_ref.dtype)


def _pair_attention(kernel_fn, q, k, vt, extra=(), *, t, tk, chains, name):
    B, H, _, S = q.shape
    qspec = pl.BlockSpec((1, 2, LANES, t), lambda b, p, i: (b, p, 0, i))
    kspec = pl.BlockSpec((1, 2, S, LANES), lambda b, p, i: (b, p, 0, 0))
    vspec = pl.BlockSpec((1, 2, S // tk, V_ROWS, tk), lambda b, p, i: (b, p, 0, 0, 0))
    xspecs = [pl.BlockSpec(a.shape, lambda b, p, i: (0, 0)) for a in extra]
    return pl.pallas_call(
        functools.partial(kernel_fn, t=t, tk=tk),
        grid=(B, H // 2, S // t),
        in_specs=xspecs + [qspec, kspec, vspec],
        out_specs=pl.BlockSpec((1, LANES, t), lambda b, p, i: (b, p, i)),
        out_shape=jax.ShapeDtypeStruct((B, (H // 2) * LANES, S), BF16),
        scratch_shapes=_flash_scratch(chains, t, tk),
        compiler_params=pltpu.CompilerParams(dimension_semantics=("parallel", "parallel", "parallel"),
                                             vmem_limit_bytes=VMEM_LIMIT),
        name=name,
    )(*extra, q, k, vt)


def _memkv_kernel(x_ref, g_ref, w_ref, o_ref):
    o_ref[0] = _dot(_rms(x_ref[0], g_ref[...]).astype(BF16), w_ref[...]).astype(o_ref.dtype)


def _memkv(mem, g, wkv):
    B, M, _ = mem.shape
    return pl.pallas_call(
        _memkv_kernel,
        grid=(B,),
        in_specs=[pl.BlockSpec((1, M, D_MODEL), lambda b: (b, 0, 0)), pl.BlockSpec(g.shape, lambda b: (0, 0)),
                  pl.BlockSpec(wkv.shape, lambda b: (0, 0))],
        out_specs=pl.BlockSpec((1, M, 2 * D_MODEL), lambda b: (b, 0, 0)),
        out_shape=jax.ShapeDtypeStruct((B, M, 2 * D_MODEL), BF16),
        compiler_params=pltpu.CompilerParams(dimension_semantics=("parallel",), vmem_limit_bytes=VMEM_LIMIT),
        name="memkv",
    )(mem, g, wkv)


def _mix_cross_kernel(h_ref, oa_ref, ob_ref, oc_ref, wout_ref, g_ref, wq_ref, kv_ref, wo_ref, o_ref):
    na, nb = oa_ref.shape[1], ob_ref.shape[1]
    h1 = (h_ref[0] + _dot_tn(oa_ref[0], wout_ref[0:na]) + _dot_tn(ob_ref[0], wout_ref[na:na + nb])
          + _dot_tn(oc_ref[0], wout_ref[na + nb:]))
    n = _rms(h1, g_ref[...]).astype(BF16)
    q = (_dot(n, wq_ref[...]) * (CROSS_HD ** -0.5 * LOG2E)).astype(BF16)
    ctx = []
    for h in range(CROSS_HEADS):
        sl = slice(h * CROSS_HD, (h + 1) * CROSS_HD)
        s = _dot_nt(q[:, sl], kv_ref[0, :, sl])
        p = jnp.exp2(s - jnp.max(s, axis=-1, keepdims=True))
        l = jnp.sum(p, axis=-1, keepdims=True)
        c = _dot(p.astype(BF16), kv_ref[0, :, D_MODEL + h * CROSS_HD:D_MODEL + (h + 1) * CROSS_HD])
        ctx.append((c * (1.0 / l)).astype(BF16))
    o_ref[0] = h1 + _dot(jnp.concatenate(ctx, axis=-1), wo_ref[...])


def _mix_cross(h, oa, ob, oc, wout, g, wq, memkv, wo, *, tm):
    B, S, _ = h.shape
    tok = lambda b, i: (b, i, 0)
    const2 = lambda b, i: (0, 0)
    full = lambda a: pl.BlockSpec(a.shape, const2)
    return pl.pallas_call(
        _mix_cross_kernel,
        grid=(B, S // tm),
        in_specs=[pl.BlockSpec((1, tm, D_MODEL), tok)]
                 + [pl.BlockSpec((1, o.shape[1], tm), lambda b, i: (b, 0, i)) for o in (oa, ob, oc)] + [
                  full(wout), full(g), full(wq),
                  pl.BlockSpec((1,) + memkv.shape[1:], lambda b, i: (b, 0, 0)), full(wo)],
        out_specs=pl.BlockSpec((1, tm, D_MODEL), tok),
        out_shape=jax.ShapeDtypeStruct(h.shape, F32),
        compiler_params=pltpu.CompilerParams(dimension_semantics=("parallel", "parallel"),
                                             vmem_limit_bytes=VMEM_LIMIT),
        name="mix_cross",
    )(h, oa, ob, oc, wout, g, wq, memkv, wo)


def _mlp_kernel(h_ref, g_ref, w1_ref, w2_ref, gf_ref, o_ref, *, final, chunk):
    h = h_ref[0]
    n = _rms(h, g_ref[...]).astype(BF16)
    acc = h
    for c in range(D_FF // chunk):
        a = jnp.maximum(_dot(n, w1_ref[:, c * chunk:(c + 1) * chunk]), 0.0)
        acc = acc + _dot((a * a).astype(BF16), w2_ref[c * chunk:(c + 1) * chunk, :])
    o_ref[0] = _rms(acc, gf_ref[...]) if final else acc


def _mlp(h, g, w1, w2, gf, *, tm, final):
    B, S, _ = h.shape
    tok = lambda b, i: (b, i, 0)
    const2 = lambda b, i: (0, 0)
    resident = lambda a: pl.BlockSpec(a.shape, const2, pipeline_mode=pl.Buffered(1))
    return pl.pallas_call(
        functools.partial(_mlp_kernel, final=final, chunk=D_MODEL),
        grid=(B, S // tm),
        in_specs=[pl.BlockSpec((1, tm, D_MODEL), tok), pl.BlockSpec(g.shape, const2), resident(w1), resident(w2),
                  pl.BlockSpec(gf.shape, const2)],
        out_specs=pl.BlockSpec((1, tm, D_MODEL), tok),
        out_shape=jax.ShapeDtypeStruct(h.shape, F32),
        compiler_params=pltpu.CompilerParams(dimension_semantics=("parallel", "parallel"),
                                             vmem_limit_bytes=VMEM_LIMIT),
        name="mlp",
    )(h, g, w1, w2, gf)


def _rot_pairs(w):
    half = w.shape[-1] // 2
    return jnp.concatenate([-w[..., half:], w[..., :half]], axis=-1)


def _pad_lanes(w, lo, width=LANES):
    pad = [(0, 0)] * (w.ndim - 1) + [(lo, width - lo - w.shape[-1])]
    return jnp.pad(w, pad)


def _head_groups(w, nheads, hd, lo=0, width=LANES):
    K = w.shape[0]
    return _pad_lanes(w.reshape(K, nheads, hd), lo, width).reshape(K, nheads * width)


def _widen_w_in(w):
    cuts = np.cumsum([0, MLA_Q_RANK, MLA_KV_RANK, MLA_ROPE, C_DIFF, C_DIFF, C_DIFF, C_MOBA, C_MOBA, C_MOBA])
    cq, ckv, kr, dq, dk, dv, mq, mk, mv = [w[:, int(a):int(b)] for a, b in zip(cuts[:-1], cuts[1:])]
    two_maps = lambda x: _head_groups(x, 2 * DIFF_HEADS, DIFF_HD, 0, HALF)
    out = jnp.concatenate([
        cq, ckv, _pad_lanes(kr, MLA_NOPE), _pad_lanes(_rot_pairs(kr), MLA_NOPE),
        dq, two_maps(dk), dv, mq, _head_groups(mk, MOBA_HEADS, MOBA_HD), mv], axis=1).astype(BF16)
    assert out.shape[1] == C_WIDE
    return out


def _widen_w_uq(w):
    K = w.shape[0]
    e = w.reshape(K, MLA_HEADS, MLA_NOPE + MLA_ROPE)
    nope = e[..., :MLA_NOPE].reshape(K, MLA_HEADS * MLA_NOPE)
    rope = _pad_lanes(e[..., MLA_NOPE:].reshape(K, MLA_HEADS * MLA_ROPE), 0, 2 * LANES)
    rot = _pad_lanes(_rot_pairs(e[..., MLA_NOPE:]).reshape(K, MLA_HEADS * MLA_ROPE), 0, 2 * LANES)
    out = jnp.concatenate([nope, rope, rot], axis=1).astype(BF16)
    assert out.shape[1] == UQ_COLS
    return out


def _widen_w_ukv(w):
    K = w.shape[0]
    e = w.reshape(K, MLA_HEADS, MLA_NOPE + MLA_V)
    kn = _pad_lanes(e[..., :MLA_NOPE], 0).reshape(K, MLA_HEADS * LANES)
    vv = e[..., MLA_NOPE:].reshape(K, MLA_HEADS * MLA_V)
    out = jnp.concatenate([kn, vv], axis=1).astype(BF16)
    assert out.shape[1] == UKV_COLS
    return out


def kernel(x, mem, positions, attn_norm, w_in, mla_q_norm, mla_w_uq, mla_kv_norm, mla_w_ukv, diff_lambda_q1, diff_lambda_k1, diff_lambda_q2, diff_lambda_k2, diff_sub_norm, w_out, cross_norm, mem_norm, cross_wq, cross_wkv, cross_wo, mlp_norm, mlp_w1, mlp_w2, final_norm):
    B, S, _ = x.shape
    depth = w_in.shape[0]
    tm = 512
    t = 512
    tk = t // 2
    assert S % tm == 0 and tm % t == 0 and tk % MOBA_BLOCK == 0 and S // MOBA_BLOCK <= LANES - MOBA_SEL

    pos = positions.astype(jnp.int32)[:, None, :]
    rel = (positions - positions[:, :1]).astype(jnp.int32)[..., None]
    half = MLA_ROPE // 2
    inv_lane = (ROPE_THETA ** (-jnp.arange(half, dtype=F32) / half))[:, None]
    row = lambda v: v.astype(F32)[None, :]

    h = x
    for l in range(depth):
        outs = _inproj(h, pos, rel, inv_lane, row(attn_norm[l]), _widen_w_in(w_in[l]), row(mla_q_norm[l]),
                       _widen_w_uq(mla_w_uq[l]), row(mla_kv_norm[l]), _widen_w_ukv(mla_w_ukv[l]), tm=tm, t=tk)
        qa, ka, va, qd, kd, vd, mq, mk, mv, kmean = outs
        km = kmean.reshape(B, S // MOBA_BLOCK, MOBA_HEADS, LANES)[..., :MOBA_HD].transpose(0, 2, 1, 3)
        km_pad = jnp.pad(km, ((0, 0), (0, 0), (0, LANES - MOBA_SEL - S // MOBA_BLOCK), (0, 0)))
        mq_aug = _moba_gate(mq, km_pad, tm=tm)

        attn = functools.partial(_pair_attention, t=t, tk=tk)
        o_a = attn(_mla_attn_kernel, qa, ka, va, chains=2, name="mla_attn")
        lam_rows = jnp.stack([diff_lambda_q1[l], diff_lambda_k1[l], diff_lambda_q2[l], diff_lambda_k2[l]])
        lam_rows = jnp.pad(lam_rows.astype(F32), ((0, 4), (0, LANES - DIFF_HD)))
        gain = jnp.tile(diff_sub_norm[l].astype(F32), 2)[:, None]
        lam_init = 0.8 - 0.6 * math.exp(-0.3 * l)
        o_b = attn(functools.partial(_diff_attn_kernel, lam_init=lam_init), qd, kd, vd,
                   extra=(lam_rows, gain), chains=4, name="diff_attn")
        o_c = attn(_mla_attn_kernel, mq_aug, mk, mv, chains=2, name="moba_attn")

        memkv = _memkv(mem, row(mem_norm[l]), cross_wkv[l].astype(BF16))
        h = _mix_cross(h, o_a, o_b, o_c, w_out[l].astype(BF16), row(cross_norm[l]), cross_wq[l].astype(BF16),
                       memkv, cross_wo[l].astype(BF16), tm=tm)
        h = _mlp(h, row(mlp_norm[l]), mlp_w1[l].astype(BF16), mlp_w2[l].astype(BF16), row(final_norm),
                 tm=tm, final=(l == depth - 1))
    return h
```

```python
import functools
import math

import jax
import jax.numpy as jnp
import numpy as np
from jax import lax
from jax.experimental import pallas as pl
from jax.experimental.pallas import tpu as pltpu

D_MODEL = 1024
MLA_HEADS = 6
MLA_NOPE = 64
MLA_ROPE = 32
MLA_V = 64
MLA_Q_RANK = 256
MLA_KV_RANK = 128
ROPE_THETA = 10000.0
DIFF_HEADS = 6
DIFF_HD = 32
MOBA_HEADS = 4
MOBA_HD = 64
MOBA_BLOCK = 256
MOBA_TOPK = 3
CROSS_HEADS = 4
CROSS_HD = D_MODEL // CROSS_HEADS
D_FF = 4 * D_MODEL
EPS = 1e-6
NEG = -1e30
N_ALIBI = DIFF_HEADS + MOBA_HEADS
C_DIFF = DIFF_HEADS * 2 * DIFF_HD
C_MOBA = MOBA_HEADS * MOBA_HD

LANES = 128
HALF = LANES // 2
LOG2E = 1.4426950408889634
VMEM_LIMIT = 56 * 1024 * 1024

F32 = jnp.float32
BF16 = jnp.bfloat16

V_DIM = 64
V_ONE = V_DIM
V_ROWS = 80
DIFF_MAP2 = HALF
DIFF_BIAS = DIFF_HD
MOBA_BIAS = MOBA_HD
MOBA_SEL = 96
N_SPLIT = 3
UNROLL_CHAIN_PAIRS = 8
UNDERFLOW_BITS = 152.0
NORM_MARGIN = 1.02

O_CQ = 0
O_CKV = O_CQ + MLA_Q_RANK
O_KR = O_CKV + MLA_KV_RANK
O_KRS = O_KR + LANES
O_DQ = O_KRS + LANES
O_DK = O_DQ + C_DIFF
O_DV = O_DK + DIFF_HEADS * LANES
O_MQ = O_DV + C_DIFF
O_MK = O_MQ + C_MOBA
O_MV = O_MK + MOBA_HEADS * LANES
C_WIDE = O_MV + C_MOBA
UQ_NOPE = 0
UQ_ROPE = MLA_HEADS * MLA_NOPE
UQ_ROT = UQ_ROPE + 2 * LANES
UQ_COLS = UQ_ROT + 2 * LANES
UKV_V = MLA_HEADS * LANES
UKV_COLS = UKV_V + MLA_HEADS * MLA_V


def _alibi_slope(h):
    return 2.0 ** (-8.0 * (h + 1) / N_ALIBI)


def _rms(x, g):
    return x * lax.rsqrt(jnp.mean(x * x, axis=-1, keepdims=True) + EPS) * g


def _dot(a, b):
    return jnp.dot(a, b, preferred_element_type=F32)


def _dot_nt(a, b):
    return lax.dot_general(a, b, (((1,), (1,)), ((), ())), preferred_element_type=F32)


def _dot_tn(a, b):
    return lax.dot_general(a, b, (((0,), (0,)), ((), ())), preferred_element_type=F32)


def _lane(shape):
    return lax.broadcasted_iota(jnp.int32, shape, len(shape) - 1)


def _onehot_lanes(lanes, value=1.0):
    l = _lane((1, LANES))
    out = jnp.zeros((1, LANES), F32)
    for i in lanes:
        out = jnp.where(l == i, value, out)
    return out


def _split3(x):
    hi = x.astype(BF16).astype(F32)
    r = x - hi
    mid = r.astype(BF16).astype(F32)
    lo = (r - mid).astype(BF16).astype(F32)
    return hi, mid, lo


def _store_vt(ref, h, vt, t):
    tm = vt.shape[1]
    tail = jnp.where(lax.broadcasted_iota(jnp.int32, (V_ROWS - V_DIM, tm), 0) == 0, 1.0, 0.0)
    x = jnp.concatenate([vt, tail], axis=0).astype(BF16)
    for j in range(tm // t):
        ref[0, h, j] = x[:, j * t:(j + 1) * t]


def _bias_rows(n, tm):
    return jnp.where(lax.broadcasted_iota(jnp.int32, (n, tm), 0) < N_SPLIT, 1.0, 0.0)


def _inproj_kernel(h_ref, pos_ref, rel_ref, inv_ref, g_ref, w_ref, qn_ref, wuq_ref, kvn_ref, wukv_ref,
                   qa_ref, ka_ref, va_ref, qd_ref, kd_ref, vd_ref, mq_ref, mk_ref, mv_ref, kmean_ref,
                   qnorm_ref, knorm_ref, *, tm, t):
    si = pl.program_id(1)
    nb = _rms(h_ref[0], g_ref[...]).astype(BF16)
    group = lambda x, o, i: x[:, o + i * LANES:o + (i + 1) * LANES]

    pm = _dot(nb, w_ref[:, O_CQ:O_DQ])
    cq = _rms(pm[:, O_CQ:O_CKV], qn_ref[...]).astype(BF16)
    ckv = _rms(pm[:, O_CKV:O_KR], kvn_ref[...]).astype(BF16)
    q2 = _dot(cq, wuq_ref[...])
    kv2 = _dot(ckv, wukv_ref[...])
    ang = inv_ref[...] * pos_ref[0].astype(F32)
    reps = LANES // ang.shape[0]
    cs = jnp.concatenate([jnp.cos(ang)] * reps, axis=0).T
    sn = jnp.concatenate([jnp.sin(ang)] * reps, axis=0).T
    krope = pm[:, O_KR:O_KRS] * cs + pm[:, O_KRS:O_DQ] * sn
    qscale = (MLA_NOPE + MLA_ROPE) ** -0.5 * LOG2E
    rope_t = [(group(q2, UQ_ROPE, i) * cs + group(q2, UQ_ROT, i) * sn).T for i in range(2)]
    pad_rows = jnp.zeros((LANES - MLA_NOPE - MLA_ROPE, tm), F32)
    for p in range(MLA_HEADS // 2):
        nope_t = group(q2, UQ_NOPE, p).T
        v_t = group(kv2, UKV_V, p).T
        for e in range(2):
            h = 2 * p + e
            r = (h % 4) * MLA_ROPE
            qa = jnp.concatenate([nope_t[e * HALF:(e + 1) * HALF], rope_t[h // 4][r:r + MLA_ROPE], pad_rows], axis=0)
            qa_ref[0, h] = (qa * qscale).astype(BF16)
            _store_vt(va_ref, h, v_t[e * HALF:(e + 1) * HALF], t)
    for h in range(MLA_HEADS):
        ka_ref[0, h] = (group(kv2, 0, h) + krope).astype(BF16)

    relf = rel_ref[0].astype(F32)

    pq = _dot(nb, w_ref[:, O_DQ:O_DK])
    pk = _dot(nb, w_ref[:, O_DK:O_DV])
    pv = _dot(nb, w_ref[:, O_DV:O_MQ])
    dscale = DIFF_HD ** -0.5 * LOG2E
    ones_rows = _bias_rows(DIFF_MAP2 - DIFF_HD, tm)
    lane1 = _lane((1, LANES))
    qn_row = jnp.zeros((1, LANES), F32)
    kn_row = jnp.zeros((1, LANES), F32)
    for p in range(DIFF_HEADS // 2):
        q_t = group(pq, 0, p).T * dscale
        v_t = group(pv, 0, p).T
        for e in range(2):
            o = e * HALF
            qd_ref[0, 2 * p + e] = jnp.concatenate(
                [q_t[o:o + DIFF_HD], ones_rows, q_t[o + DIFF_HD:o + HALF], ones_rows], axis=0).astype(BF16)
            _store_vt(vd_ref, 2 * p + e, v_t[o:o + HALF], t)
            qsq = jnp.sum(q_t[o:o + HALF] * q_t[o:o + HALF], axis=0, keepdims=True)
            qn_row = jnp.where(lane1 == 2 * p + e, jnp.sqrt(jnp.max(qsq, axis=1, keepdims=True)), qn_row)
    for h in range(DIFF_HEADS):
        kb = group(pk, 0, h)
        ksq = jnp.sum(kb * kb, axis=1, keepdims=True)
        kn_row = jnp.where(lane1 == h, jnp.sqrt(jnp.max(ksq, axis=0, keepdims=True)), kn_row)
        for i, piece in enumerate(_split3(relf * (_alibi_slope(h) * LOG2E))):
            kb = kb + piece * _onehot_lanes([DIFF_BIAS + i, DIFF_MAP2 + DIFF_BIAS + i])
        kd_ref[0, h] = kb.astype(BF16)
    qnorm_ref[0, 0] = qn_row
    knorm_ref[0, 0] = kn_row

    pq = _dot(nb, w_ref[:, O_MQ:O_MK])
    pk = _dot(nb, w_ref[:, O_MK:O_MV])
    pv = _dot(nb, w_ref[:, O_MV:C_WIDE])
    for p in range(MOBA_HEADS // 2):
        q_t = group(pq, 0, p).T
        v_t = group(pv, 0, p).T
        for e in range(2):
            mq_ref[0, 2 * p + e] = q_t[e * HALF:(e + 1) * HALF]
            _store_vt(mv_ref, 2 * p + e, v_t[e * HALF:(e + 1) * HALF], t)
    row = lax.broadcasted_iota(jnp.int32, (tm, 1), 0)
    blk = (si * tm + row) // MOBA_BLOCK
    blk_onehot = jnp.where(_lane((tm, LANES)) == MOBA_SEL + blk, 1.0, 0.0)
    for h in range(MOBA_HEADS):
        kb = group(pk, 0, h) + blk_onehot
        for i, piece in enumerate(_split3(relf * (_alibi_slope(DIFF_HEADS + h) * LOG2E))):
            kb = kb + piece * _onehot_lanes([MOBA_BIAS + i])
        mk_ref[0, h] = kb.astype(BF16)
    for j in range(tm // MOBA_BLOCK):
        kmean_ref[0, j] = jnp.mean(pk[j * MOBA_BLOCK:(j + 1) * MOBA_BLOCK, :], axis=0, keepdims=True)


def _inproj(h, pos, rel, inv_lane, g, w_wide, qn, wuq, kvn, wukv, *, tm, t):
    B, S, _ = h.shape
    grid = (B, S // tm)
    tok = lambda b, i: (b, i, 0)
    const2 = lambda b, i: (0, 0)
    q_out = lambda nh: (jax.ShapeDtypeStruct((B, nh, LANES, S), BF16),
                        pl.BlockSpec((1, nh, LANES, tm), lambda b, i: (b, 0, 0, i)))
    k_out = lambda nh, dt: (jax.ShapeDtypeStruct((B, nh, S, LANES), dt),
                            pl.BlockSpec((1, nh, tm, LANES), lambda b, i: (b, 0, i, 0)))
    v_out = lambda nh: (jax.ShapeDtypeStruct((B, nh, S // t, V_ROWS, t), BF16),
                        pl.BlockSpec((1, nh, tm // t, V_ROWS, t), lambda b, i: (b, 0, i, 0, 0)))
    outs = [q_out(MLA_HEADS), k_out(MLA_HEADS, BF16), v_out(MLA_HEADS),
            q_out(DIFF_HEADS), k_out(DIFF_HEADS, BF16), v_out(DIFF_HEADS),
            (jax.ShapeDtypeStruct((B, MOBA_HEADS, MOBA_HD, S), F32),
             pl.BlockSpec((1, MOBA_HEADS, MOBA_HD, tm), lambda b, i: (b, 0, 0, i))),
            k_out(MOBA_HEADS, BF16), v_out(MOBA_HEADS),
            (jax.ShapeDtypeStruct((B, S // MOBA_BLOCK, 1, MOBA_HEADS * LANES), F32),
             pl.BlockSpec((1, tm // MOBA_BLOCK, 1, MOBA_HEADS * LANES), lambda b, i: (b, i, 0, 0)))]
    outs += [(jax.ShapeDtypeStruct((B, S // tm, 1, LANES), F32),
              pl.BlockSpec((1, 1, 1, LANES), lambda b, i: (b, i, 0, 0)))] * 2
    full = lambda a: pl.BlockSpec(a.shape, const2)
    return pl.pallas_call(
        functools.partial(_inproj_kernel, tm=tm, t=t),
        grid=grid,
        in_specs=[pl.BlockSpec((1, tm, D_MODEL), tok), pl.BlockSpec((1, 1, tm), lambda b, i: (b, 0, i)),
                  pl.BlockSpec((1, tm, 1), tok), full(inv_lane), full(g), full(w_wide), full(qn),
                  full(wuq), full(kvn), full(wukv)],
        out_specs=[o[1] for o in outs],
        out_shape=[o[0] for o in outs],
        compiler_params=pltpu.CompilerParams(dimension_semantics=("parallel", "parallel"),
                                             vmem_limit_bytes=VMEM_LIMIT),
        name="inproj",
    )(h, pos, rel, inv_lane, g, w_wide, qn, wuq, kvn, wukv)


def _moba_gate_kernel(mq_ref, km_ref, o_ref, *, tm):
    si = pl.program_id(1)
    nblk = LANES - MOBA_SEL
    blk = lax.broadcasted_iota(jnp.int32, (nblk, tm), 0)
    blkf = blk.astype(F32)
    own = (si * tm + _lane((1, tm))) // MOBA_BLOCK
    valid = blk < own
    ones_rows = _bias_rows(MOBA_SEL - MOBA_HD, tm)
    for h in range(MOBA_HEADS):
        qt = mq_ref[0, h]
        gate = jnp.dot(km_ref[0, h], qt, precision=lax.Precision.HIGHEST, preferred_element_type=F32)
        g = jnp.where(valid, gate, NEG)
        sel = blk == own
        for _ in range(MOBA_TOPK):
            mx = jnp.max(g, axis=0, keepdims=True)
            idx = jnp.min(jnp.where(g == mx, blkf, float(nblk)), axis=0, keepdims=True)
            pick = (blkf == idx) & (mx > 0.5 * NEG)
            sel = sel | pick
            g = jnp.where(pick, NEG, g)
        selbias = jnp.where(sel, 0.0, NEG)
        o_ref[0, h] = jnp.concatenate([qt * (MOBA_HD ** -0.5 * LOG2E), ones_rows, selbias],
                                      axis=0).astype(BF16)


def _moba_gate(mq, km_pad, *, tm):
    B, H, _, S = mq.shape
    return pl.pallas_call(
        functools.partial(_moba_gate_kernel, tm=tm),
        grid=(B, S // tm),
        in_specs=[pl.BlockSpec((1, H, MOBA_HD, tm), lambda b, i: (b, 0, 0, i)),
                  pl.BlockSpec((1, H, LANES - MOBA_SEL, MOBA_HD), lambda b, i: (b, 0, 0, 0))],
        out_specs=pl.BlockSpec((1, H, LANES, tm), lambda b, i: (b, 0, 0, i)),
        out_shape=jax.ShapeDtypeStruct((B, H, LANES, S), BF16),
        compiler_params=pltpu.CompilerParams(dimension_semantics=("parallel", "parallel"),
                                             vmem_limit_bytes=VMEM_LIMIT),
        name="moba_gate",
    )(mq, km_pad)


def _causal_flash(chains, k_ref, vt_ref, scratch, qi, j0, t, tk):
    assert t == 2 * tk
    s_a, s_b, mb_a, mb_b, m_scr, acc_scr = scratch
    nc = len(chains)

    every, first, second = slice(0, t), slice(0, tk), slice(tk, t)

    def scores(kt, s_dst, mb_dst, qs=every):
        tiles = {}
        for c, (hh, qt) in enumerate(chains):
            if hh not in tiles:
                tiles[hh] = k_ref[0, hh, pl.ds(pl.multiple_of(kt * tk, tk), tk), :]
            s = _dot(tiles[hh], qt[:, qs])
            s_dst[c, :, qs] = s
            mb_dst[c, :, qs] = jnp.max(s, axis=0, keepdims=True)

    def softmax_pv(kt, s_src, mb_src, qs=every, diagonal=False):
        for c, (hh, _) in enumerate(chains):
            s = s_src[c, :, qs]
            if diagonal:
                keep = (lax.broadcasted_iota(jnp.int32, (tk, tk), 0)
                        <= lax.broadcasted_iota(jnp.int32, (tk, tk), 1))
                s = jnp.where(keep, s, NEG)
                mb = jnp.max(s, axis=0, keepdims=True)
            else:
                mb = mb_src[c, :, qs]
            m = m_scr[c, :, qs]
            m_new = jnp.maximum(m, mb)
            p = jnp.exp2(s - m_new).astype(BF16)
            acc_scr[c, :, qs] = acc_scr[c, :, qs] * jnp.exp2(m - m_new) + _dot(vt_ref[0, hh, kt], p)
            m_scr[c, :, qs] = m_new

    for c in range(nc):
        m_scr[c] = jnp.full((1, t), NEG, F32)
        acc_scr[c] = jnp.zeros((V_ROWS, t), F32)
    scores(2 * j0, s_a, mb_a)

    def pair(j, carry):
        scores(2 * j + 1, s_b, mb_b)
        softmax_pv(2 * j, s_a, mb_a)
        scores(2 * j + 2, s_a, mb_a)
        softmax_pv(2 * j + 1, s_b, mb_b)
        return carry

    unroll = UNROLL_CHAIN_PAIRS // nc

    def pairs(i, carry):
        for u in range(unroll):
            pair(j0 + unroll * i + u, carry)
        return carry

    n_main = (qi - j0) // unroll
    lax.fori_loop(0, n_main, pairs, 0)
    lax.fori_loop(j0 + n_main * unroll, qi, pair, 0)
    scores(2 * qi + 1, s_b, mb_b, second)
    softmax_pv(2 * qi, s_a, mb_a, first, diagonal=True)
    softmax_pv(2 * qi, s_a, mb_a, second)
    softmax_pv(2 * qi + 1, s_b, mb_b, second, diagonal=True)
    outs = []
    for c in range(nc):
        acc = acc_scr[c]
        outs.append(acc[:V_DIM] * (1.0 / acc[V_ONE:V_ONE + 1]))
    return outs


def _flash_scratch(nc, t, tk):
    return ([pltpu.VMEM((nc, tk, t), F32)] * 2 + [pltpu.VMEM((nc, 1, t), F32)] * 3
            + [pltpu.VMEM((nc, V_ROWS, t), F32)])


def _first_pair(j0_ref):
    b, p, i = pl.program_id(0), pl.program_id(1), pl.program_id(2)
    return j0_ref[(b * pl.num_programs(1) + p) * pl.num_programs(2) + i]


def _mla_attn_kernel(j0_ref, q_ref, k_ref, vt_ref, o_ref, *scratch, t, tk):
    o0, o1 = _causal_flash([(0, q_ref[0, 0]), (1, q_ref[0, 1])], k_ref, vt_ref, scratch,
                           pl.program_id(2), _first_pair(j0_ref), t, tk)
    o_ref[0] = jnp.concatenate([o0, o1], axis=0).astype(o_ref.dtype)


def _diff_attn_kernel(j0_ref, lam_ref, gain_ref, q_ref, k_ref, vt_ref, o_ref, *scratch, t, tk, lam_init):
    lv = lam_ref[...]
    lam = (jnp.exp(jnp.sum(lv[0:1] * lv[1:2], axis=-1, keepdims=True))
           - jnp.exp(jnp.sum(lv[2:3] * lv[3:4], axis=-1, keepdims=True)) + lam_init)
    feat = lax.broadcasted_iota(jnp.int32, (LANES, t), 0)
    chains = []
    for hh in range(2):
        q = q_ref[0, hh]
        chains.append((hh, jnp.where(feat < DIFF_MAP2, q, jnp.zeros_like(q))))
        chains.append((hh, jnp.where(feat >= DIFF_MAP2, q, jnp.zeros_like(q))))
    a0, b0, a1, b1 = _causal_flash(chains, k_ref, vt_ref, scratch, pl.program_id(2), _first_pair(j0_ref), t, tk)
    outs = []
    for a, b in ((a0, b0), (a1, b1)):
        o = a - lam * b
        outs.append(o * lax.rsqrt(jnp.mean(o * o, axis=0, keepdims=True) + EPS))
    o_ref[0] = (jnp.concatenate(outs, axis=0) * gain_ref[...] * (1.0 - lam_init)).astype(o_ref.dtype)


def _pair_attention(kernel_fn, q, k, vt, extra=(), first_pair=None, *, t, tk, chains, name):
    B, H, _, S = q.shape
    if first_pair is None:
        first_pair = jnp.zeros((B, H // 2, S // t), jnp.int32)
    qspec = pl.BlockSpec((1, 2, LANES, t), lambda b, p, i, j0: (b, p, 0, i))
    kspec = pl.BlockSpec((1, 2, S, LANES), lambda b, p, i, j0: (b, p, 0, 0))
    vspec = pl.BlockSpec((1, 2, S // tk, V_ROWS, tk), lambda b, p, i, j0: (b, p, 0, 0, 0))
    xspecs = [pl.BlockSpec(a.shape, lambda b, p, i, j0: (0, 0)) for a in extra]
    return pl.pallas_call(
        functools.partial(kernel_fn, t=t, tk=tk),
        grid_spec=pltpu.PrefetchScalarGridSpec(
            num_scalar_prefetch=1,
            grid=(B, H // 2, S // t),
            in_specs=xspecs + [qspec, kspec, vspec],
            out_specs=pl.BlockSpec((1, LANES, t), lambda b, p, i, j0: (b, p, i)),
            scratch_shapes=_flash_scratch(chains, t, tk)),
        out_shape=jax.ShapeDtypeStruct((B, (H // 2) * LANES, S), BF16),
        compiler_params=pltpu.CompilerParams(dimension_semantics=("parallel", "parallel", "parallel"),
                                             vmem_limit_bytes=VMEM_LIMIT),
        name=name,
    )(first_pair.reshape(-1), *extra, q, k, vt)


def _memkv_kernel(x_ref, g_ref, w_ref, o_ref):
    o_ref[0] = _dot(_rms(x_ref[0], g_ref[...]).astype(BF16), w_ref[...]).astype(o_ref.dtype)


def _memkv(mem, g, wkv):
    B, M, _ = mem.shape
    return pl.pallas_call(
        _memkv_kernel,
        grid=(B,),
        in_specs=[pl.BlockSpec((1, M, D_MODEL), lambda b: (b, 0, 0)), pl.BlockSpec(g.shape, lambda b: (0, 0)),
                  pl.BlockSpec(wkv.shape, lambda b: (0, 0))],
        out_specs=pl.BlockSpec((1, M, 2 * D_MODEL), lambda b: (b, 0, 0)),
        out_shape=jax.ShapeDtypeStruct((B, M, 2 * D_MODEL), BF16),
        compiler_params=pltpu.CompilerParams(dimension_semantics=("parallel",), vmem_limit_bytes=VMEM_LIMIT),
        name="memkv",
    )(mem, g, wkv)


def _mix_cross_kernel(h_ref, oa_ref, ob_ref, oc_ref, wout_ref, g_ref, wq_ref, kv_ref, wo_ref, o_ref):
    na, nb = oa_ref.shape[1], ob_ref.shape[1]
    h1 = (h_ref[0] + _dot_tn(oa_ref[0], wout_ref[0:na]) + _dot_tn(ob_ref[0], wout_ref[na:na + nb])
          + _dot_tn(oc_ref[0], wout_ref[na + nb:]))
    n = _rms(h1, g_ref[...]).astype(BF16)
    q = (_dot(n, wq_ref[...]) * (CROSS_HD ** -0.5 * LOG2E)).astype(BF16)
    ctx = []
    for h in range(CROSS_HEADS):
        sl = slice(h * CROSS_HD, (h + 1) * CROSS_HD)
        s = _dot_nt(q[:, sl], kv_ref[0, :, sl])
        p = jnp.exp2(s - jnp.max(s, axis=-1, keepdims=True))
        l = jnp.sum(p, axis=-1, keepdims=True)
        c = _dot(p.astype(BF16), kv_ref[0, :, D_MODEL + h * CROSS_HD:D_MODEL + (h + 1) * CROSS_HD])
        ctx.append((c * (1.0 / l)).astype(BF16))
    o_ref[0] = h1 + _dot(jnp.concatenate(ctx, axis=-1), wo_ref[...])


def _mix_cross(h, oa, ob, oc, wout, g, wq, memkv, wo, *, tm):
    B, S, _ = h.shape
    tok = lambda b, i: (b, i, 0)
    const2 = lambda b, i: (0, 0)
    full = lambda a: pl.BlockSpec(a.shape, const2)
    return pl.pallas_call(
        _mix_cross_kernel,
        grid=(B, S // tm),
        in_specs=[pl.BlockSpec((1, tm, D_MODEL), tok)]
                 + [pl.BlockSpec((1, o.shape[1], tm), lambda b, i: (b, 0, i)) for o in (oa, ob, oc)] + [
                  full(wout), full(g), full(wq),
                  pl.BlockSpec((1,) + memkv.shape[1:], lambda b, i: (b, 0, 0)), full(wo)],
        out_specs=pl.BlockSpec((1, tm, D_MODEL), tok),
        out_shape=jax.ShapeDtypeStruct(h.shape, F32),
        compiler_params=pltpu.CompilerParams(dimension_semantics=("parallel", "parallel"),
                                             vmem_limit_bytes=VMEM_LIMIT),
        name="mix_cross",
    )(h, oa, ob, oc, wout, g, wq, memkv, wo)


def _mlp_kernel(h_ref, g_ref, w1_ref, w2_ref, gf_ref, o_ref, *, final, chunk):
    h = h_ref[0]
    n = _rms(h, g_ref[...]).astype(BF16)
    acc = h
    for c in range(D_FF // chunk):
        a = jnp.maximum(_dot(n, w1_ref[:, c * chunk:(c + 1) * chunk]), 0.0)
        acc = acc + _dot((a * a).astype(BF16), w2_ref[c * chunk:(c + 1) * chunk, :])
    o_ref[0] = _rms(acc, gf_ref[...]) if final else acc


def _mlp(h, g, w1, w2, gf, *, tm, final):
    B, S, _ = h.shape
    tok = lambda b, i: (b, i, 0)
    const2 = lambda b, i: (0, 0)
    resident = lambda a: pl.BlockSpec(a.shape, const2, pipeline_mode=pl.Buffered(1))
    return pl.pallas_call(
        functools.partial(_mlp_kernel, final=final, chunk=D_MODEL),
        grid=(B, S // tm),
        in_specs=[pl.BlockSpec((1, tm, D_MODEL), tok), pl.BlockSpec(g.shape, const2), resident(w1), resident(w2),
                  pl.BlockSpec(gf.shape, const2)],
        out_specs=pl.BlockSpec((1, tm, D_MODEL), tok),
        out_shape=jax.ShapeDtypeStruct(h.shape, F32),
        compiler_params=pltpu.CompilerParams(dimension_semantics=("parallel", "parallel"),
                                             vmem_limit_bytes=VMEM_LIMIT),
        name="mlp",
    )(h, g, w1, w2, gf)


def _rot_pairs(w):
    half = w.shape[-1] // 2
    return jnp.concatenate([-w[..., half:], w[..., :half]], axis=-1)


def _pad_lanes(w, lo, width=LANES):
    pad = [(0, 0)] * (w.ndim - 1) + [(lo, width - lo - w.shape[-1])]
    return jnp.pad(w, pad)


def _head_groups(w, nheads, hd, lo=0, width=LANES):
    K = w.shape[0]
    return _pad_lanes(w.reshape(K, nheads, hd), lo, width).reshape(K, nheads * width)


def _widen_w_in(w):
    cuts = np.cumsum([0, MLA_Q_RANK, MLA_KV_RANK, MLA_ROPE, C_DIFF, C_DIFF, C_DIFF, C_MOBA, C_MOBA, C_MOBA])
    cq, ckv, kr, dq, dk, dv, mq, mk, mv = [w[:, int(a):int(b)] for a, b in zip(cuts[:-1], cuts[1:])]
    two_maps = lambda x: _head_groups(x, 2 * DIFF_HEADS, DIFF_HD, 0, HALF)
    out = jnp.concatenate([
        cq, ckv, _pad_lanes(kr, MLA_NOPE), _pad_lanes(_rot_pairs(kr), MLA_NOPE),
        dq, two_maps(dk), dv, mq, _head_groups(mk, MOBA_HEADS, MOBA_HD), mv], axis=1).astype(BF16)
    assert out.shape[1] == C_WIDE
    return out


def _widen_w_uq(w):
    K = w.shape[0]
    e = w.reshape(K, MLA_HEADS, MLA_NOPE + MLA_ROPE)
    nope = e[..., :MLA_NOPE].reshape(K, MLA_HEADS * MLA_NOPE)
    rope = _pad_lanes(e[..., MLA_NOPE:].reshape(K, MLA_HEADS * MLA_ROPE), 0, 2 * LANES)
    rot = _pad_lanes(_rot_pairs(e[..., MLA_NOPE:]).reshape(K, MLA_HEADS * MLA_ROPE), 0, 2 * LANES)
    out = jnp.concatenate([nope, rope, rot], axis=1).astype(BF16)
    assert out.shape[1] == UQ_COLS
    return out


def _widen_w_ukv(w):
    K = w.shape[0]
    e = w.reshape(K, MLA_HEADS, MLA_NOPE + MLA_V)
    kn = _pad_lanes(e[..., :MLA_NOPE], 0).reshape(K, MLA_HEADS * LANES)
    vv = e[..., MLA_NOPE:].reshape(K, MLA_HEADS * MLA_V)
    out = jnp.concatenate([kn, vv], axis=1).astype(BF16)
    assert out.shape[1] == UKV_COLS
    return out


def _diff_first_pairs(qnorm, knorm, rel, t, tk):
    B, nq = qnorm.shape[:2]
    qn = qnorm[:, :, 0, :DIFF_HEADS]
    kn = jnp.max(knorm[:, :, 0, :DIFF_HEADS], axis=1)
    spread = NORM_MARGIN * 2.0 * qn * kn[:, None, :]
    relf = rel[..., 0].astype(F32)
    dist = relf[:, ::t][:, :, None] - relf[:, tk - 1::tk][:, None, :]
    c = jnp.asarray([_alibi_slope(h) * LOG2E for h in range(DIFF_HEADS)], F32)
    dead = c[None, None, :, None] * dist[:, :, None, :] > spread[..., None] + UNDERFLOW_BITS
    tiles = jnp.sum(dead.astype(jnp.int32), axis=-1)
    pairs = jnp.min(tiles.reshape(B, nq, DIFF_HEADS // 2, 2), axis=-1) // 2
    return pairs.transpose(0, 2, 1)


def kernel(x, mem, positions, attn_norm, w_in, mla_q_norm, mla_w_uq, mla_kv_norm, mla_w_ukv, diff_lambda_q1, diff_lambda_k1, diff_lambda_q2, diff_lambda_k2, diff_sub_norm, w_out, cross_norm, mem_norm, cross_wq, cross_wkv, cross_wo, mlp_norm, mlp_w1, mlp_w2, final_norm):
    B, S, _ = x.shape
    depth = w_in.shape[0]
    tm = 512
    t = 512
    tk = t // 2
    assert S % tm == 0 and tm == t and tk % MOBA_BLOCK == 0 and S // MOBA_BLOCK <= LANES - MOBA_SEL

    pos = positions.astype(jnp.int32)[:, None, :]
    rel = (positions - positions[:, :1]).astype(jnp.int32)[..., None]
    half = MLA_ROPE // 2
    inv_lane = (ROPE_THETA ** (-jnp.arange(half, dtype=F32) / half))[:, None]
    row = lambda v: v.astype(F32)[None, :]

    h = x
    for l in range(depth):
        outs = _inproj(h, pos, rel, inv_lane, row(attn_norm[l]), _widen_w_in(w_in[l]), row(mla_q_norm[l]),
                       _widen_w_uq(mla_w_uq[l]), row(mla_kv_norm[l]), _widen_w_ukv(mla_w_ukv[l]), tm=tm, t=tk)
        qa, ka, va, qd, kd, vd, mq, mk, mv, kmean, qnorm, knorm = outs
        km = kmean.reshape(B, S // MOBA_BLOCK, MOBA_HEADS, LANES)[..., :MOBA_HD].transpose(0, 2, 1, 3)
        km_pad = jnp.pad(km, ((0, 0), (0, 0), (0, LANES - MOBA_SEL - S // MOBA_BLOCK), (0, 0)))
        mq_aug = _moba_gate(mq, km_pad, tm=tm)

        attn = functools.partial(_pair_attention, t=t, tk=tk)
        o_a = attn(_mla_attn_kernel, qa, ka, va, chains=2, name="mla_attn")
        lam_rows = jnp.stack([diff_lambda_q1[l], diff_lambda_k1[l], diff_lambda_q2[l], diff_lambda_k2[l]])
        lam_rows = jnp.pad(lam_rows.astype(F32), ((0, 4), (0, LANES - DIFF_HD)))
        gain = jnp.tile(diff_sub_norm[l].astype(F32), 2)[:, None]
        lam_init = 0.8 - 0.6 * math.exp(-0.3 * l)
        o_b = attn(functools.partial(_diff_attn_kernel, lam_init=lam_init), qd, kd, vd,
                   extra=(lam_rows, gain), first_pair=_diff_first_pairs(qnorm, knorm, rel, t, tk),
                   chains=4, name="diff_attn")
        o_c = attn(_mla_attn_kernel, mq_aug, mk, mv, chains=2, name="moba_attn")

        memkv = _memkv(mem, row(mem_norm[l]), cross_wkv[l].astype(BF16))
        h = _mix_cross(h, o_a, o_b, o_c, w_out[l].astype(BF16), row(cross_norm[l]), cross_wq[l].astype(BF16),
                       memkv, cross_wo[l].astype(BF16), tm=tm)
        h = _mlp(h, row(mlp_norm[l]), mlp_w1[l].astype(BF16), mlp_w2[l].astype(BF16), row(final_norm),
                 tm=tm, final=(l == depth - 1))
    return h
```

```python
import functools
import math

import jax
import jax.numpy as jnp
import numpy as np
from jax import lax
from jax.experimental import pallas as pl
from jax.experimental.pallas import tpu as pltpu

D_MODEL = 1024
MLA_HEADS = 6
MLA_NOPE = 64
MLA_ROPE = 32
MLA_V = 64
MLA_Q_RANK = 256
MLA_KV_RANK = 128
ROPE_THETA = 10000.0
DIFF_HEADS = 6
DIFF_HD = 32
MOBA_HEADS = 4
MOBA_HD = 64
MOBA_BLOCK = 256
MOBA_TOPK = 3
CROSS_HEADS = 4
CROSS_HD = D_MODEL // CROSS_HEADS
D_FF = 4 * D_MODEL
EPS = 1e-6
NEG = -1e30
N_ALIBI = DIFF_HEADS + MOBA_HEADS
C_DIFF = DIFF_HEADS * 2 * DIFF_HD
C_MOBA = MOBA_HEADS * MOBA_HD

LANES = 128
HALF = LANES // 2
LOG2E = 1.4426950408889634
VMEM_LIMIT = 56 * 1024 * 1024

F32 = jnp.float32
BF16 = jnp.bfloat16

V_DIM = 64
V_ONE = V_DIM
V_ROWS = 80
DIFF_MAP2 = HALF
DIFF_BIAS = DIFF_HD
MOBA_BIAS = MOBA_HD
MOBA_SEL = 96
N_SPLIT = 3
UNROLL_CHAIN_PAIRS = 12
UNDERFLOW_BITS = 152.0
NORM_MARGIN = 1.02

O_CQ = 0
O_CKV = O_CQ + MLA_Q_RANK
O_KR = O_CKV + MLA_KV_RANK
O_KRS = O_KR + LANES
O_DQ = O_KRS + LANES
O_DK = O_DQ + C_DIFF
O_DV = O_DK + DIFF_HEADS * LANES
O_MQ = O_DV + C_DIFF
O_MK = O_MQ + C_MOBA
O_MV = O_MK + MOBA_HEADS * LANES
C_WIDE = O_MV + C_MOBA
UQ_NOPE = 0
UQ_ROPE = MLA_HEADS * MLA_NOPE
UQ_ROT = UQ_ROPE + 2 * LANES
UQ_COLS = UQ_ROT + 2 * LANES
UKV_V = MLA_HEADS * LANES
UKV_COLS = UKV_V + MLA_HEADS * MLA_V


def _alibi_slope(h):
    return 2.0 ** (-8.0 * (h + 1) / N_ALIBI)


def _rms(x, g):
    return x * lax.rsqrt(jnp.mean(x * x, axis=-1, keepdims=True) + EPS) * g


def _dot(a, b):
    return jnp.dot(a, b, preferred_element_type=F32)


def _dot_nt(a, b):
    return lax.dot_general(a, b, (((1,), (1,)), ((), ())), preferred_element_type=F32)


def _dot_tn(a, b):
    return lax.dot_general(a, b, (((0,), (0,)), ((), ())), preferred_element_type=F32)


def _lane(shape):
    return lax.broadcasted_iota(jnp.int32, shape, len(shape) - 1)


def _onehot_lanes(lanes, value=1.0):
    l = _lane((1, LANES))
    out = jnp.zeros((1, LANES), F32)
    for i in lanes:
        out = jnp.where(l == i, value, out)
    return out


def _split3(x):
    hi = x.astype(BF16).astype(F32)
    r = x - hi
    mid = r.astype(BF16).astype(F32)
    lo = (r - mid).astype(BF16).astype(F32)
    return hi, mid, lo


def _store_vt(ref, h, vt, t):
    tm = vt.shape[1]
    tail = jnp.where(lax.broadcasted_iota(jnp.int32, (V_ROWS - V_DIM, tm), 0) == 0, 1.0, 0.0)
    x = jnp.concatenate([vt, tail], axis=0).astype(BF16)
    for j in range(tm // t):
        ref[0, h, j] = x[:, j * t:(j + 1) * t]


def _bias_rows(n, tm):
    return jnp.where(lax.broadcasted_iota(jnp.int32, (n, tm), 0) < N_SPLIT, 1.0, 0.0)


def _inproj_kernel(h_ref, pos_ref, rel_ref, inv_ref, g_ref, w_ref, qn_ref, wuq_ref, kvn_ref, wukv_ref,
                   qa_ref, ka_ref, va_ref, qd_ref, kd_ref, vd_ref, mq_ref, mk_ref, mv_ref, kmean_ref,
                   qnorm_ref, knorm_ref, *, tm, t):
    si = pl.program_id(1)
    nb = _rms(h_ref[0], g_ref[...]).astype(BF16)
    group = lambda x, o, i: x[:, o + i * LANES:o + (i + 1) * LANES]

    pm = _dot(nb, w_ref[:, O_CQ:O_DQ])
    cq = _rms(pm[:, O_CQ:O_CKV], qn_ref[...]).astype(BF16)
    ckv = _rms(pm[:, O_CKV:O_KR], kvn_ref[...]).astype(BF16)
    q2 = _dot(cq, wuq_ref[...])
    kv2 = _dot(ckv, wukv_ref[...])
    ang = inv_ref[...] * pos_ref[0].astype(F32)
    reps = LANES // ang.shape[0]
    cs = jnp.concatenate([jnp.cos(ang)] * reps, axis=0).T
    sn = jnp.concatenate([jnp.sin(ang)] * reps, axis=0).T
    krope = pm[:, O_KR:O_KRS] * cs + pm[:, O_KRS:O_DQ] * sn
    qscale = (MLA_NOPE + MLA_ROPE) ** -0.5 * LOG2E
    rope_t = [(group(q2, UQ_ROPE, i) * cs + group(q2, UQ_ROT, i) * sn).T for i in range(2)]
    pad_rows = jnp.zeros((LANES - MLA_NOPE - MLA_ROPE, tm), F32)
    for p in range(MLA_HEADS // 2):
        nope_t = group(q2, UQ_NOPE, p).T
        v_t = group(kv2, UKV_V, p).T
        for e in range(2):
            h = 2 * p + e
            r = (h % 4) * MLA_ROPE
            qa = jnp.concatenate([nope_t[e * HALF:(e + 1) * HALF], rope_t[h // 4][r:r + MLA_ROPE], pad_rows], axis=0)
            qa_ref[0, h] = (qa * qscale).astype(BF16)
            _store_vt(va_ref, h, v_t[e * HALF:(e + 1) * HALF], t)
    for h in range(MLA_HEADS):
        ka_ref[0, h] = (group(kv2, 0, h) + krope).astype(BF16)

    relf = rel_ref[0].astype(F32)

    pq = _dot(nb, w_ref[:, O_DQ:O_DK])
    pk = _dot(nb, w_ref[:, O_DK:O_DV])
    pv = _dot(nb, w_ref[:, O_DV:O_MQ])
    dscale = DIFF_HD ** -0.5 * LOG2E
    ones_rows = _bias_rows(DIFF_MAP2 - DIFF_HD, tm)
    lane1 = _lane((1, LANES))
    qn_row = jnp.zeros((1, LANES), F32)
    kn_row = jnp.zeros((1, LANES), F32)
    for p in range(DIFF_HEADS // 2):
        q_t = group(pq, 0, p).T * dscale
        v_t = group(pv, 0, p).T
        for e in range(2):
            o = e * HALF
            qd_ref[0, 2 * p + e] = jnp.concatenate(
                [q_t[o:o + DIFF_HD], ones_rows, q_t[o + DIFF_HD:o + HALF], ones_rows], axis=0).astype(BF16)
            _store_vt(vd_ref, 2 * p + e, v_t[o:o + HALF], t)
            for mp in range(2):
                qm = q_t[o + mp * DIFF_HD:o + (mp + 1) * DIFF_HD]
                qsq = jnp.max(jnp.sum(qm * qm, axis=0, keepdims=True), axis=1, keepdims=True)
                qn_row = jnp.where(lane1 == 2 * (2 * p + e) + mp, jnp.sqrt(qsq), qn_row)
    in_map1 = _lane((tm, LANES)) < DIFF_MAP2
    for h in range(DIFF_HEADS):
        kb = group(pk, 0, h)
        for mp in range(2):
            mine = in_map1 if mp == 0 else jnp.logical_not(in_map1)
            ksq = jnp.sum(jnp.where(mine, kb * kb, 0.0), axis=1, keepdims=True)
            kn_row = jnp.where(lane1 == 2 * h + mp, jnp.sqrt(jnp.max(ksq, axis=0, keepdims=True)), kn_row)
        for i, piece in enumerate(_split3(relf * (_alibi_slope(h) * LOG2E))):
            kb = kb + piece * _onehot_lanes([DIFF_BIAS + i, DIFF_MAP2 + DIFF_BIAS + i])
        kd_ref[0, h] = kb.astype(BF16)
    qnorm_ref[0, 0] = qn_row
    knorm_ref[0, 0] = kn_row

    pq = _dot(nb, w_ref[:, O_MQ:O_MK])
    pk = _dot(nb, w_ref[:, O_MK:O_MV])
    pv = _dot(nb, w_ref[:, O_MV:C_WIDE])
    for p in range(MOBA_HEADS // 2):
        q_t = group(pq, 0, p).T
        v_t = group(pv, 0, p).T
        for e in range(2):
            mq_ref[0, 2 * p + e] = q_t[e * HALF:(e + 1) * HALF]
            _store_vt(mv_ref, 2 * p + e, v_t[e * HALF:(e + 1) * HALF], t)
    row = lax.broadcasted_iota(jnp.int32, (tm, 1), 0)
    blk = (si * tm + row) // MOBA_BLOCK
    blk_onehot = jnp.where(_lane((tm, LANES)) == MOBA_SEL + blk, 1.0, 0.0)
    for h in range(MOBA_HEADS):
        kb = group(pk, 0, h) + blk_onehot
        for i, piece in enumerate(_split3(relf * (_alibi_slope(DIFF_HEADS + h) * LOG2E))):
            kb = kb + piece * _onehot_lanes([MOBA_BIAS + i])
        mk_ref[0, h] = kb.astype(BF16)
    for j in range(tm // MOBA_BLOCK):
        kmean_ref[0, j] = jnp.mean(pk[j * MOBA_BLOCK:(j + 1) * MOBA_BLOCK, :], axis=0, keepdims=True)


def _inproj(h, pos, rel, inv_lane, g, w_wide, qn, wuq, kvn, wukv, *, tm, t):
    B, S, _ = h.shape
    grid = (B, S // tm)
    tok = lambda b, i: (b, i, 0)
    const2 = lambda b, i: (0, 0)
    q_out = lambda nh: (jax.ShapeDtypeStruct((B, nh, LANES, S), BF16),
                        pl.BlockSpec((1, nh, LANES, tm), lambda b, i: (b, 0, 0, i)))
    k_out = lambda nh, dt: (jax.ShapeDtypeStruct((B, nh, S, LANES), dt),
                            pl.BlockSpec((1, nh, tm, LANES), lambda b, i: (b, 0, i, 0)))
    v_out = lambda nh: (jax.ShapeDtypeStruct((B, nh, S // t, V_ROWS, t), BF16),
                        pl.BlockSpec((1, nh, tm // t, V_ROWS, t), lambda b, i: (b, 0, i, 0, 0)))
    outs = [q_out(MLA_HEADS), k_out(MLA_HEADS, BF16), v_out(MLA_HEADS),
            q_out(DIFF_HEADS), k_out(DIFF_HEADS, BF16), v_out(DIFF_HEADS),
            (jax.ShapeDtypeStruct((B, MOBA_HEADS, MOBA_HD, S), F32),
             pl.BlockSpec((1, MOBA_HEADS, MOBA_HD, tm), lambda b, i: (b, 0, 0, i))),
            k_out(MOBA_HEADS, BF16), v_out(MOBA_HEADS),
            (jax.ShapeDtypeStruct((B, S // MOBA_BLOCK, 1, MOBA_HEADS * LANES), F32),
             pl.BlockSpec((1, tm // MOBA_BLOCK, 1, MOBA_HEADS * LANES), lambda b, i: (b, i, 0, 0)))]
    outs += [(jax.ShapeDtypeStruct((B, S // tm, 1, LANES), F32),
              pl.BlockSpec((1, 1, 1, LANES), lambda b, i: (b, i, 0, 0)))] * 2
    full = lambda a: pl.BlockSpec(a.shape, const2)
    return pl.pallas_call(
        functools.partial(_inproj_kernel, tm=tm, t=t),
        grid=grid,
        in_specs=[pl.BlockSpec((1, tm, D_MODEL), tok), pl.BlockSpec((1, 1, tm), lambda b, i: (b, 0, i)),
                  pl.BlockSpec((1, tm, 1), tok), full(inv_lane), full(g), full(w_wide), full(qn),
                  full(wuq), full(kvn), full(wukv)],
        out_specs=[o[1] for o in outs],
        out_shape=[o[0] for o in outs],
        compiler_params=pltpu.CompilerParams(dimension_semantics=("parallel", "parallel"),
                                             vmem_limit_bytes=VMEM_LIMIT),
        name="inproj",
    )(h, pos, rel, inv_lane, g, w_wide, qn, wuq, kvn, wukv)


def _moba_gate_kernel(mq_ref, km_ref, o_ref, *, tm):
    si = pl.program_id(1)
    nblk = LANES - MOBA_SEL
    blk = lax.broadcasted_iota(jnp.int32, (nblk, tm), 0)
    blkf = blk.astype(F32)
    own = (si * tm + _lane((1, tm))) // MOBA_BLOCK
    valid = blk < own
    ones_rows = _bias_rows(MOBA_SEL - MOBA_HD, tm)
    for h in range(MOBA_HEADS):
        qt = mq_ref[0, h]
        gate = jnp.dot(km_ref[0, h], qt, precision=lax.Precision.HIGHEST, preferred_element_type=F32)
        g = jnp.where(valid, gate, NEG)
        sel = blk == own
        for _ in range(MOBA_TOPK):
            mx = jnp.max(g, axis=0, keepdims=True)
            idx = jnp.min(jnp.where(g == mx, blkf, float(nblk)), axis=0, keepdims=True)
            pick = (blkf == idx) & (mx > 0.5 * NEG)
            sel = sel | pick
            g = jnp.where(pick, NEG, g)
        selbias = jnp.where(sel, 0.0, NEG)
        o_ref[0, h] = jnp.concatenate([qt * (MOBA_HD ** -0.5 * LOG2E), ones_rows, selbias],
                                      axis=0).astype(BF16)


def _moba_gate(mq, km_pad, *, tm):
    B, H, _, S = mq.shape
    return pl.pallas_call(
        functools.partial(_moba_gate_kernel, tm=tm),
        grid=(B, S // tm),
        in_specs=[pl.BlockSpec((1, H, MOBA_HD, tm), lambda b, i: (b, 0, 0, i)),
                  pl.BlockSpec((1, H, LANES - MOBA_SEL, MOBA_HD), lambda b, i: (b, 0, 0, 0))],
        out_specs=pl.BlockSpec((1, H, LANES, tm), lambda b, i: (b, 0, 0, i)),
        out_shape=jax.ShapeDtypeStruct((B, H, LANES, S), BF16),
        compiler_params=pltpu.CompilerParams(dimension_semantics=("parallel", "parallel"),
                                             vmem_limit_bytes=VMEM_LIMIT),
        name="moba_gate",
    )(mq, km_pad)


def _causal_flash(chains, next_chains, k_ref, vt_ref, scratch, qi, j0, j0_next, t, tk):
    assert t == 2 * tk
    s_a, s_b, mb_a, mb_b, m_scr, acc_scr = scratch
    nc = len(chains)

    every, first, second = slice(0, t), slice(0, tk), slice(tk, t)

    def scores(kt, s_dst, mb_dst, qs=every, chains=chains):
        tiles = {}
        for c, (hh, qt) in enumerate(chains):
            if hh not in tiles:
                tiles[hh] = k_ref[0, hh, pl.ds(pl.multiple_of(kt * tk, tk), tk), :]
            s = _dot(tiles[hh], qt[:, qs])
            s_dst[c, :, qs] = s
            mb_dst[c, :, qs] = jnp.max(s, axis=0, keepdims=True)

    def softmax_pv(kt, s_src, mb_src, qs=every, diagonal=False):
        for c, (hh, _) in enumerate(chains):
            s = s_src[c, :, qs]
            if diagonal:
                keep = (lax.broadcasted_iota(jnp.int32, (tk, tk), 0)
                        <= lax.broadcasted_iota(jnp.int32, (tk, tk), 1))
                s = jnp.where(keep, s, NEG)
                mb = jnp.max(s, axis=0, keepdims=True)
            else:
                mb = mb_src[c, :, qs]
            m = m_scr[c, :, qs]
            m_new = jnp.maximum(m, mb)
            p = jnp.exp2(s - m_new).astype(BF16)
            acc_scr[c, :, qs] = acc_scr[c, :, qs] * jnp.exp2(m - m_new) + _dot(vt_ref[0, hh, kt], p)
            m_scr[c, :, qs] = m_new

    for c in range(nc):
        m_scr[c] = jnp.full((1, t), NEG, F32)
        acc_scr[c] = jnp.zeros((V_ROWS, t), F32)

    @pl.when(qi == 0)
    def _():
        scores(2 * j0, s_a, mb_a)

    def pair(j, carry):
        scores(2 * j + 1, s_b, mb_b)
        softmax_pv(2 * j, s_a, mb_a)
        scores(2 * j + 2, s_a, mb_a)
        softmax_pv(2 * j + 1, s_b, mb_b)
        return carry

    unroll = UNROLL_CHAIN_PAIRS // nc

    def pairs(i, carry):
        for u in range(unroll):
            pair(j0 + unroll * i + u, carry)
        return carry

    n_main = (qi - j0) // unroll
    lax.fori_loop(0, n_main, pairs, 0)
    lax.fori_loop(j0 + n_main * unroll, qi, pair, 0)
    scores(2 * qi + 1, s_b, mb_b, second)
    softmax_pv(2 * qi, s_a, mb_a, first, diagonal=True)
    softmax_pv(2 * qi, s_a, mb_a, second)
    scores(2 * j0_next, s_a, mb_a, chains=next_chains)
    softmax_pv(2 * qi + 1, s_b, mb_b, second, diagonal=True)
    outs = []
    for c in range(nc):
        acc = acc_scr[c]
        outs.append(acc[:V_DIM] * (1.0 / acc[V_ONE:V_ONE + 1]))
    return outs


def _flash_scratch(nc, t, tk):
    return ([pltpu.VMEM((nc, tk, t), F32)] * 2 + [pltpu.VMEM((nc, 1, t), F32)] * 3
            + [pltpu.VMEM((nc, V_ROWS, t), F32)])


def _first_pairs(j0_ref):
    b, p, i = pl.program_id(0), pl.program_id(1), pl.program_id(2)
    step = (b * pl.num_programs(1) + p) * pl.num_programs(2) + i
    last = pl.num_programs(0) * pl.num_programs(1) * pl.num_programs(2) - 1
    return j0_ref[step], j0_ref[jnp.minimum(step + 1, last)]


def _mla_attn_kernel(j0_ref, q_ref, qnext_ref, k_ref, vt_ref, o_ref, *scratch, t, tk):
    heads = range(q_ref.shape[1])
    outs = _causal_flash([(hh, q_ref[0, hh]) for hh in heads], [(hh, qnext_ref[0, hh]) for hh in heads],
                         k_ref, vt_ref, scratch, pl.program_id(2), *_first_pairs(j0_ref), t, tk)
    o_ref[0] = jnp.concatenate(outs, axis=0).astype(o_ref.dtype)


def _diff_attn_kernel(j0_ref, lam_ref, gain_ref, q_ref, qnext_ref, k_ref, vt_ref, o_ref, *scratch, t, tk, lam_init):
    lv = lam_ref[...]
    lam = (jnp.exp(jnp.sum(lv[0:1] * lv[1:2], axis=-1, keepdims=True))
           - jnp.exp(jnp.sum(lv[2:3] * lv[3:4], axis=-1, keepdims=True)) + lam_init)
    feat = lax.broadcasted_iota(jnp.int32, (LANES, t), 0)

    def map_chains(ref):
        chains = []
        for hh in range(2):
            q = ref[0, hh]
            chains.append((hh, jnp.where(feat < DIFF_MAP2, q, jnp.zeros_like(q))))
            chains.append((hh, jnp.where(feat >= DIFF_MAP2, q, jnp.zeros_like(q))))
        return chains

    a0, b0, a1, b1 = _causal_flash(map_chains(q_ref), map_chains(qnext_ref), k_ref, vt_ref, scratch,
                                   pl.program_id(2), *_first_pairs(j0_ref), t, tk)
    outs = []
    for a, b in ((a0, b0), (a1, b1)):
        o = a - lam * b
        outs.append(o * lax.rsqrt(jnp.mean(o * o, axis=0, keepdims=True) + EPS))
    o_ref[0] = (jnp.concatenate(outs, axis=0) * gain_ref[...] * (1.0 - lam_init)).astype(o_ref.dtype)


def _group_attention(kernel_fn, q, k, vt, extra=(), first_pair=None, *, t, tk, group, chains, name):
    B, H, _, S = q.shape
    if first_pair is None:
        first_pair = jnp.zeros((B, H // group, S // t), jnp.int32)
    nq = S // t
    qspec = pl.BlockSpec((1, group, LANES, t), lambda b, p, i, j0: (b, p, 0, i))
    qnext = pl.BlockSpec((1, group, LANES, t), lambda b, p, i, j0: (b, p, 0, jnp.minimum(i + 1, nq - 1)))
    kspec = pl.BlockSpec((1, group, S, LANES), lambda b, p, i, j0: (b, p, 0, 0))
    vspec = pl.BlockSpec((1, group, S // tk, V_ROWS, tk), lambda b, p, i, j0: (b, p, 0, 0, 0))
    xspecs = [pl.BlockSpec(a.shape, lambda b, p, i, j0: (0, 0)) for a in extra]
    return pl.pallas_call(
        functools.partial(kernel_fn, t=t, tk=tk),
        grid_spec=pltpu.PrefetchScalarGridSpec(
            num_scalar_prefetch=1,
            grid=(B, H // group, S // t),
            in_specs=xspecs + [qspec, qnext, kspec, vspec],
            out_specs=pl.BlockSpec((1, group * V_DIM, t), lambda b, p, i, j0: (b, p, i)),
            scratch_shapes=_flash_scratch(chains, t, tk)),
        out_shape=jax.ShapeDtypeStruct((B, H * V_DIM, S), BF16),
        compiler_params=pltpu.CompilerParams(dimension_semantics=("parallel", "parallel", "arbitrary"),
                                             vmem_limit_bytes=VMEM_LIMIT),
        name=name,
    )(first_pair.reshape(-1), *extra, q, q, k, vt)


def _memkv_kernel(x_ref, g_ref, w_ref, o_ref):
    o_ref[0] = _dot(_rms(x_ref[0], g_ref[...]).astype(BF16), w_ref[...]).astype(o_ref.dtype)


def _memkv(mem, g, wkv):
    B, M, _ = mem.shape
    return pl.pallas_call(
        _memkv_kernel,
        grid=(B,),
        in_specs=[pl.BlockSpec((1, M, D_MODEL), lambda b: (b, 0, 0)), pl.BlockSpec(g.shape, lambda b: (0, 0)),
                  pl.BlockSpec(wkv.shape, lambda b: (0, 0))],
        out_specs=pl.BlockSpec((1, M, 2 * D_MODEL), lambda b: (b, 0, 0)),
        out_shape=jax.ShapeDtypeStruct((B, M, 2 * D_MODEL), BF16),
        compiler_params=pltpu.CompilerParams(dimension_semantics=("parallel",), vmem_limit_bytes=VMEM_LIMIT),
        name="memkv",
    )(mem, g, wkv)


def _mix_cross_kernel(h_ref, oa_ref, ob_ref, oc_ref, wout_ref, g_ref, wq_ref, kv_ref, wo_ref, o_ref):
    na, nb = oa_ref.shape[1], ob_ref.shape[1]
    h1 = (h_ref[0] + _dot_tn(oa_ref[0], wout_ref[0:na]) + _dot_tn(ob_ref[0], wout_ref[na:na + nb])
          + _dot_tn(oc_ref[0], wout_ref[na + nb:]))
    n = _rms(h1, g_ref[...]).astype(BF16)
    q = (_dot(n, wq_ref[...]) * (CROSS_HD ** -0.5 * LOG2E)).astype(BF16)
    ctx = []
    for h in range(CROSS_HEADS):
        sl = slice(h * CROSS_HD, (h + 1) * CROSS_HD)
        s = _dot_nt(q[:, sl], kv_ref[0, :, sl])
        p = jnp.exp2(s - jnp.max(s, axis=-1, keepdims=True))
        l = jnp.sum(p, axis=-1, keepdims=True)
        c = _dot(p.astype(BF16), kv_ref[0, :, D_MODEL + h * CROSS_HD:D_MODEL + (h + 1) * CROSS_HD])
        ctx.append((c * (1.0 / l)).astype(BF16))
    o_ref[0] = h1 + _dot(jnp.concatenate(ctx, axis=-1), wo_ref[...])


def _mix_cross(h, oa, ob, oc, wout, g, wq, memkv, wo, *, tm):
    B, S, _ = h.shape
    tok = lambda b, i: (b, i, 0)
    const2 = lambda b, i: (0, 0)
    full = lambda a: pl.BlockSpec(a.shape, const2)
    return pl.pallas_call(
        _mix_cross_kernel,
        grid=(B, S // tm),
        in_specs=[pl.BlockSpec((1, tm, D_MODEL), tok)]
                 + [pl.BlockSpec((1, o.shape[1], tm), lambda b, i: (b, 0, i)) for o in (oa, ob, oc)] + [
                  full(wout), full(g), full(wq),
                  pl.BlockSpec((1,) + memkv.shape[1:], lambda b, i: (b, 0, 0)), full(wo)],
        out_specs=pl.BlockSpec((1, tm, D_MODEL), tok),
        out_shape=jax.ShapeDtypeStruct(h.shape, F32),
        compiler_params=pltpu.CompilerParams(dimension_semantics=("parallel", "parallel"),
                                             vmem_limit_bytes=VMEM_LIMIT),
        name="mix_cross",
    )(h, oa, ob, oc, wout, g, wq, memkv, wo)


def _mlp_kernel(h_ref, g_ref, w1_ref, w2_ref, gf_ref, o_ref, *, final, chunk):
    h = h_ref[0]
    n = _rms(h, g_ref[...]).astype(BF16)
    acc = h
    for c in range(D_FF // chunk):
        a = jnp.maximum(_dot(n, w1_ref[:, c * chunk:(c + 1) * chunk]), 0.0)
        acc = acc + _dot((a * a).astype(BF16), w2_ref[c * chunk:(c + 1) * chunk, :])
    o_ref[0] = _rms(acc, gf_ref[...]) if final else acc


def _mlp(h, g, w1, w2, gf, *, tm, final):
    B, S, _ = h.shape
    tok = lambda b, i: (b, i, 0)
    const2 = lambda b, i: (0, 0)
    resident = lambda a: pl.BlockSpec(a.shape, const2, pipeline_mode=pl.Buffered(1))
    return pl.pallas_call(
        functools.partial(_mlp_kernel, final=final, chunk=D_MODEL),
        grid=(B, S // tm),
        in_specs=[pl.BlockSpec((1, tm, D_MODEL), tok), pl.BlockSpec(g.shape, const2), resident(w1), resident(w2),
                  pl.BlockSpec(gf.shape, const2)],
        out_specs=pl.BlockSpec((1, tm, D_MODEL), tok),
        out_shape=jax.ShapeDtypeStruct(h.shape, F32),
        compiler_params=pltpu.CompilerParams(dimension_semantics=("parallel", "parallel"),
                                             vmem_limit_bytes=VMEM_LIMIT),
        name="mlp",
    )(h, g, w1, w2, gf)


def _rot_pairs(w):
    half = w.shape[-1] // 2
    return jnp.concatenate([-w[..., half:], w[..., :half]], axis=-1)


def _pad_lanes(w, lo, width=LANES):
    pad = [(0, 0)] * (w.ndim - 1) + [(lo, width - lo - w.shape[-1])]
    return jnp.pad(w, pad)


def _head_groups(w, nheads, hd, lo=0, width=LANES):
    K = w.shape[0]
    return _pad_lanes(w.reshape(K, nheads, hd), lo, width).reshape(K, nheads * width)


def _widen_w_in(w):
    cuts = np.cumsum([0, MLA_Q_RANK, MLA_KV_RANK, MLA_ROPE, C_DIFF, C_DIFF, C_DIFF, C_MOBA, C_MOBA, C_MOBA])
    cq, ckv, kr, dq, dk, dv, mq, mk, mv = [w[:, int(a):int(b)] for a, b in zip(cuts[:-1], cuts[1:])]
    two_maps = lambda x: _head_groups(x, 2 * DIFF_HEADS, DIFF_HD, 0, HALF)
    out = jnp.concatenate([
        cq, ckv, _pad_lanes(kr, MLA_NOPE), _pad_lanes(_rot_pairs(kr), MLA_NOPE),
        dq, two_maps(dk), dv, mq, _head_groups(mk, MOBA_HEADS, MOBA_HD), mv], axis=1).astype(BF16)
    assert out.shape[1] == C_WIDE
    return out


def _widen_w_uq(w):
    K = w.shape[0]
    e = w.reshape(K, MLA_HEADS, MLA_NOPE + MLA_ROPE)
    nope = e[..., :MLA_NOPE].reshape(K, MLA_HEADS * MLA_NOPE)
    rope = _pad_lanes(e[..., MLA_NOPE:].reshape(K, MLA_HEADS * MLA_ROPE), 0, 2 * LANES)
    rot = _pad_lanes(_rot_pairs(e[..., MLA_NOPE:]).reshape(K, MLA_HEADS * MLA_ROPE), 0, 2 * LANES)
    out = jnp.concatenate([nope, rope, rot], axis=1).astype(BF16)
    assert out.shape[1] == UQ_COLS
    return out


def _widen_w_ukv(w):
    K = w.shape[0]
    e = w.reshape(K, MLA_HEADS, MLA_NOPE + MLA_V)
    kn = _pad_lanes(e[..., :MLA_NOPE], 0).reshape(K, MLA_HEADS * LANES)
    vv = e[..., MLA_NOPE:].reshape(K, MLA_HEADS * MLA_V)
    out = jnp.concatenate([kn, vv], axis=1).astype(BF16)
    assert out.shape[1] == UKV_COLS
    return out


def _diff_first_pairs(qnorm, knorm, rel, t, tk):
    B, nq = qnorm.shape[:2]
    nm = 2 * DIFF_HEADS
    qn = qnorm[:, :, 0, :nm]
    kn = jnp.max(knorm[:, :, 0, :nm], axis=1)
    spread = NORM_MARGIN * 2.0 * qn * kn[:, None, :]
    relf = rel[..., 0].astype(F32)
    dist = relf[:, ::t][:, :, None] - relf[:, tk - 1::tk][:, None, :]
    c = jnp.asarray([_alibi_slope(i // 2) * LOG2E for i in range(nm)], F32)
    dead = c[None, None, :, None] * dist[:, :, None, :] > spread[..., None] + UNDERFLOW_BITS
    tiles = jnp.sum(dead.astype(jnp.int32), axis=-1)
    pairs = jnp.min(tiles.reshape(B, nq, DIFF_HEADS // 2, 4), axis=-1) // 2
    return pairs.transpose(0, 2, 1)


def kernel(x, mem, positions, attn_norm, w_in, mla_q_norm, mla_w_uq, mla_kv_norm, mla_w_ukv, diff_lambda_q1, diff_lambda_k1, diff_lambda_q2, diff_lambda_k2, diff_sub_norm, w_out, cross_norm, mem_norm, cross_wq, cross_wkv, cross_wo, mlp_norm, mlp_w1, mlp_w2, final_norm):
    B, S, _ = x.shape
    depth = w_in.shape[0]
    tm = 512
    t = 512
    tk = t // 2
    assert S % tm == 0 and tm == t and tk % MOBA_BLOCK == 0 and S // MOBA_BLOCK <= LANES - MOBA_SEL

    pos = positions.astype(jnp.int32)[:, None, :]
    rel = (positions - positions[:, :1]).astype(jnp.int32)[..., None]
    half = MLA_ROPE // 2
    inv_lane = (ROPE_THETA ** (-jnp.arange(half, dtype=F32) / half))[:, None]
    row = lambda v: v.astype(F32)[None, :]

    h = x
    for l in range(depth):
        outs = _inproj(h, pos, rel, inv_lane, row(attn_norm[l]), _widen_w_in(w_in[l]), row(mla_q_norm[l]),
                       _widen_w_uq(mla_w_uq[l]), row(mla_kv_norm[l]), _widen_w_ukv(mla_w_ukv[l]), tm=tm, t=tk)
        qa, ka, va, qd, kd, vd, mq, mk, mv, kmean, qnorm, knorm = outs
        km = kmean.reshape(B, S // MOBA_BLOCK, MOBA_HEADS, LANES)[..., :MOBA_HD].transpose(0, 2, 1, 3)
        km_pad = jnp.pad(km, ((0, 0), (0, 0), (0, LANES - MOBA_SEL - S // MOBA_BLOCK), (0, 0)))
        mq_aug = _moba_gate(mq, km_pad, tm=tm)

        attn = functools.partial(_group_attention, t=t, tk=tk)
        o_a = attn(_mla_attn_kernel, qa, ka, va, group=3, chains=3, name="mla_attn")
        lam_rows = jnp.stack([diff_lambda_q1[l], diff_lambda_k1[l], diff_lambda_q2[l], diff_lambda_k2[l]])
        lam_rows = jnp.pad(lam_rows.astype(F32), ((0, 4), (0, LANES - DIFF_HD)))
        gain = jnp.tile(diff_sub_norm[l].astype(F32), 2)[:, None]
        lam_init = 0.8 - 0.6 * math.exp(-0.3 * l)
        o_b = attn(functools.partial(_diff_attn_kernel, lam_init=lam_init), qd, kd, vd,
                   extra=(lam_rows, gain), first_pair=_diff_first_pairs(qnorm, knorm, rel, t, tk),
                   group=2, chains=4, name="diff_attn")
        o_c = attn(_mla_attn_kernel, mq_aug, mk, mv, group=4, chains=4, name="moba_attn")

        memkv = _memkv(mem, row(mem_norm[l]), cross_wkv[l].astype(BF16))
        h = _mix_cross(h, o_a, o_b, o_c, w_out[l].astype(BF16), row(cross_norm[l]), cross_wq[l].astype(BF16),
                       memkv, cross_wo[l].astype(BF16), tm=tm)
        h = _mlp(h, row(mlp_norm[l]), mlp_w1[l].astype(BF16), mlp_w2[l].astype(BF16), row(final_norm),
                 tm=tm, final=(l == depth - 1))
    return h
```

```python
import functools
import math

import jax
import jax.numpy as jnp
import numpy as np
from jax import lax
from jax.experimental import pallas as pl
from jax.experimental.pallas import tpu as pltpu

D_MODEL = 1024
MLA_HEADS = 6
MLA_NOPE = 64
MLA_ROPE = 32
MLA_V = 64
MLA_Q_RANK = 256
MLA_KV_RANK = 128
ROPE_THETA = 10000.0
DIFF_HEADS = 6
DIFF_HD = 32
MOBA_HEADS = 4
MOBA_HD = 64
MOBA_BLOCK = 256
MOBA_TOPK = 3
CROSS_HEADS = 4
CROSS_HD = D_MODEL // CROSS_HEADS
D_FF = 4 * D_MODEL
EPS = 1e-6
NEG = -1e30
N_ALIBI = DIFF_HEADS + MOBA_HEADS
C_DIFF = DIFF_HEADS * 2 * DIFF_HD
C_MOBA = MOBA_HEADS * MOBA_HD

LANES = 128
HALF = LANES // 2
LOG2E = 1.4426950408889634
VMEM_LIMIT = 56 * 1024 * 1024

F32 = jnp.float32
BF16 = jnp.bfloat16

V_DIM = 64
V_ONE = V_DIM
V_ROWS = 80
DIFF_MAP2 = HALF
DIFF_BIAS = DIFF_HD
MOBA_BIAS = MOBA_HD
MOBA_SEL = 96
N_SPLIT = 3
UNROLL_CHAIN_PAIRS = 12
UNDERFLOW_BITS = 152.0
NORM_MARGIN = 1.02

O_CQ = 0
O_CKV = O_CQ + MLA_Q_RANK
O_KR = O_CKV + MLA_KV_RANK
O_KRS = O_KR + LANES
O_DQ = O_KRS + LANES
O_DK = O_DQ + C_DIFF
O_DV = O_DK + DIFF_HEADS * LANES
O_MQ = O_DV + C_DIFF
O_MK = O_MQ + C_MOBA
O_MV = O_MK + MOBA_HEADS * LANES
C_WIDE = O_MV + C_MOBA
UQ_NOPE = 0
UQ_ROPE = MLA_HEADS * MLA_NOPE
UQ_ROT = UQ_ROPE + 2 * LANES
UQ_COLS = UQ_ROT + 2 * LANES
UKV_V = MLA_HEADS * LANES
UKV_COLS = UKV_V + MLA_HEADS * MLA_V


def _alibi_slope(h):
    return 2.0 ** (-8.0 * (h + 1) / N_ALIBI)


def _rms(x, g):
    return x * lax.rsqrt(jnp.mean(x * x, axis=-1, keepdims=True) + EPS) * g


def _dot(a, b):
    return jnp.dot(a, b, preferred_element_type=F32)


def _dot_nt(a, b):
    return lax.dot_general(a, b, (((1,), (1,)), ((), ())), preferred_element_type=F32)


def _dot_tn(a, b):
    return lax.dot_general(a, b, (((0,), (0,)), ((), ())), preferred_element_type=F32)


def _lane(shape):
    return lax.broadcasted_iota(jnp.int32, shape, len(shape) - 1)


def _onehot_lanes(lanes, value=1.0):
    l = _lane((1, LANES))
    out = jnp.zeros((1, LANES), F32)
    for i in lanes:
        out = jnp.where(l == i, value, out)
    return out


def _split3(x):
    hi = x.astype(BF16).astype(F32)
    r = x - hi
    mid = r.astype(BF16).astype(F32)
    lo = (r - mid).astype(BF16).astype(F32)
    return hi, mid, lo


def _store_vt(ref, h, vt, t):
    tm = vt.shape[1]
    tail = jnp.where(lax.broadcasted_iota(jnp.int32, (V_ROWS - V_DIM, tm), 0) == 0, 1.0, 0.0)
    x = jnp.concatenate([vt, tail], axis=0).astype(BF16)
    for j in range(tm // t):
        ref[0, h, j] = x[:, j * t:(j + 1) * t]


def _bias_rows(n, tm):
    return jnp.where(lax.broadcasted_iota(jnp.int32, (n, tm), 0) < N_SPLIT, 1.0, 0.0)


def _inproj_kernel(h_ref, pos_ref, rel_ref, inv_ref, g_ref, w_ref, qn_ref, wuq_ref, kvn_ref, wukv_ref,
                   qa_ref, ka_ref, va_ref, qd_ref, kd_ref, vd_ref, mq_ref, mk_ref, mv_ref, kmean_ref,
                   qnorm_ref, knorm_ref, *, tm, t):
    si = pl.program_id(1)
    nb = _rms(h_ref[0], g_ref[...]).astype(BF16)
    group = lambda x, o, i: x[:, o + i * LANES:o + (i + 1) * LANES]

    pm = _dot(nb, w_ref[:, O_CQ:O_DQ])
    cq = _rms(pm[:, O_CQ:O_CKV], qn_ref[...]).astype(BF16)
    ckv = _rms(pm[:, O_CKV:O_KR], kvn_ref[...]).astype(BF16)
    q2 = _dot(cq, wuq_ref[...])
    kv2 = _dot(ckv, wukv_ref[...])
    ang = inv_ref[...] * pos_ref[0].astype(F32)
    reps = LANES // ang.shape[0]
    cs = jnp.concatenate([jnp.cos(ang)] * reps, axis=0).T
    sn = jnp.concatenate([jnp.sin(ang)] * reps, axis=0).T
    krope = pm[:, O_KR:O_KRS] * cs + pm[:, O_KRS:O_DQ] * sn
    qscale = (MLA_NOPE + MLA_ROPE) ** -0.5 * LOG2E
    rope_t = [(group(q2, UQ_ROPE, i) * cs + group(q2, UQ_ROT, i) * sn).T for i in range(2)]
    pad_rows = jnp.zeros((LANES - MLA_NOPE - MLA_ROPE, tm), F32)
    for p in range(MLA_HEADS // 2):
        nope_t = group(q2, UQ_NOPE, p).T
        v_t = group(kv2, UKV_V, p).T
        for e in range(2):
            h = 2 * p + e
            r = (h % 4) * MLA_ROPE
            qa = jnp.concatenate([nope_t[e * HALF:(e + 1) * HALF], rope_t[h // 4][r:r + MLA_ROPE], pad_rows], axis=0)
            qa_ref[0, h] = (qa * qscale).astype(BF16)
            _store_vt(va_ref, h, v_t[e * HALF:(e + 1) * HALF], t)
    for h in range(MLA_HEADS):
        ka_ref[0, h] = (group(kv2, 0, h) + krope).astype(BF16)

    relf = rel_ref[0].astype(F32)

    pq = _dot(nb, w_ref[:, O_DQ:O_DK])
    pk = _dot(nb, w_ref[:, O_DK:O_DV])
    pv = _dot(nb, w_ref[:, O_DV:O_MQ])
    dscale = DIFF_HD ** -0.5 * LOG2E
    ones_rows = _bias_rows(DIFF_MAP2 - DIFF_HD, tm)
    lane1 = _lane((1, LANES))
    qn_row = jnp.zeros((1, LANES), F32)
    kn_row = jnp.zeros((1, LANES), F32)
    for p in range(DIFF_HEADS // 2):
        q_t = group(pq, 0, p).T * dscale
        v_t = group(pv, 0, p).T
        for e in range(2):
            o = e * HALF
            qd_ref[0, 2 * p + e] = jnp.concatenate(
                [q_t[o:o + DIFF_HD], ones_rows, q_t[o + DIFF_HD:o + HALF], ones_rows], axis=0).astype(BF16)
            _store_vt(vd_ref, 2 * p + e, v_t[o:o + HALF], t)
            for mp in range(2):
                qm = q_t[o + mp * DIFF_HD:o + (mp + 1) * DIFF_HD]
                qsq = jnp.max(jnp.sum(qm * qm, axis=0, keepdims=True), axis=1, keepdims=True)
                qn_row = jnp.where(lane1 == 2 * (2 * p + e) + mp, jnp.sqrt(qsq), qn_row)
    in_map1 = _lane((tm, LANES)) < DIFF_MAP2
    for h in range(DIFF_HEADS):
        kb = group(pk, 0, h)
        for mp in range(2):
            mine = in_map1 if mp == 0 else jnp.logical_not(in_map1)
            ksq = jnp.sum(jnp.where(mine, kb * kb, 0.0), axis=1, keepdims=True)
            kn_row = jnp.where(lane1 == 2 * h + mp, jnp.sqrt(jnp.max(ksq, axis=0, keepdims=True)), kn_row)
        for i, piece in enumerate(_split3(relf * (_alibi_slope(h) * LOG2E))):
            kb = kb + piece * _onehot_lanes([DIFF_BIAS + i, DIFF_MAP2 + DIFF_BIAS + i])
        kd_ref[0, h] = kb.astype(BF16)
    qnorm_ref[0, 0] = qn_row
    knorm_ref[0, 0] = kn_row

    pq = _dot(nb, w_ref[:, O_MQ:O_MK])
    pk = _dot(nb, w_ref[:, O_MK:O_MV])
    pv = _dot(nb, w_ref[:, O_MV:C_WIDE])
    for p in range(MOBA_HEADS // 2):
        q_t = group(pq, 0, p).T
        v_t = group(pv, 0, p).T
        for e in range(2):
            mq_ref[0, 2 * p + e] = q_t[e * HALF:(e + 1) * HALF]
            _store_vt(mv_ref, 2 * p + e, v_t[e * HALF:(e + 1) * HALF], t)
    row = lax.broadcasted_iota(jnp.int32, (tm, 1), 0)
    blk = (si * tm + row) // MOBA_BLOCK
    blk_onehot = jnp.where(_lane((tm, LANES)) == MOBA_SEL + blk, 1.0, 0.0)
    for h in range(MOBA_HEADS):
        kb = group(pk, 0, h) + blk_onehot
        for i, piece in enumerate(_split3(relf * (_alibi_slope(DIFF_HEADS + h) * LOG2E))):
            kb = kb + piece * _onehot_lanes([MOBA_BIAS + i])
        mk_ref[0, h] = kb.astype(BF16)
    for j in range(tm // MOBA_BLOCK):
        kmean_ref[0, j] = jnp.mean(pk[j * MOBA_BLOCK:(j + 1) * MOBA_BLOCK, :], axis=0, keepdims=True)


def _inproj(h, pos, rel, inv_lane, g, w_wide, qn, wuq, kvn, wukv, *, tm, t):
    B, S, _ = h.shape
    grid = (B, S // tm)
    tok = lambda b, i: (b, i, 0)
    const2 = lambda b, i: (0, 0)
    q_out = lambda nh: (jax.ShapeDtypeStruct((B, nh, LANES, S), BF16),
                        pl.BlockSpec((1, nh, LANES, tm), lambda b, i: (b, 0, 0, i)))
    k_out = lambda nh, dt: (jax.ShapeDtypeStruct((B, nh, S, LANES), dt),
                            pl.BlockSpec((1, nh, tm, LANES), lambda b, i: (b, 0, i, 0)))
    v_out = lambda nh: (jax.ShapeDtypeStruct((B, nh, S // t, V_ROWS, t), BF16),
                        pl.BlockSpec((1, nh, tm // t, V_ROWS, t), lambda b, i: (b, 0, i, 0, 0)))
    outs = [q_out(MLA_HEADS), k_out(MLA_HEADS, BF16), v_out(MLA_HEADS),
            q_out(DIFF_HEADS), k_out(DIFF_HEADS, BF16), v_out(DIFF_HEADS),
            (jax.ShapeDtypeStruct((B, MOBA_HEADS, MOBA_HD, S), F32),
             pl.BlockSpec((1, MOBA_HEADS, MOBA_HD, tm), lambda b, i: (b, 0, 0, i))),
            k_out(MOBA_HEADS, BF16), v_out(MOBA_HEADS),
            (jax.ShapeDtypeStruct((B, S // MOBA_BLOCK, 1, MOBA_HEADS * LANES), F32),
             pl.BlockSpec((1, tm // MOBA_BLOCK, 1, MOBA_HEADS * LANES), lambda b, i: (b, i, 0, 0)))]
    outs += [(jax.ShapeDtypeStruct((B, S // tm, 1, LANES), F32),
              pl.BlockSpec((1, 1, 1, LANES), lambda b, i: (b, i, 0, 0)))] * 2
    full = lambda a: pl.BlockSpec(a.shape, const2)
    return pl.pallas_call(
        functools.partial(_inproj_kernel, tm=tm, t=t),
        grid=grid,
        in_specs=[pl.BlockSpec((1, tm, D_MODEL), tok), pl.BlockSpec((1, 1, tm), lambda b, i: (b, 0, i)),
                  pl.BlockSpec((1, tm, 1), tok), full(inv_lane), full(g), full(w_wide), full(qn),
                  full(wuq), full(kvn), full(wukv)],
        out_specs=[o[1] for o in outs],
        out_shape=[o[0] for o in outs],
        compiler_params=pltpu.CompilerParams(dimension_semantics=("parallel", "parallel"),
                                             vmem_limit_bytes=VMEM_LIMIT),
        name="inproj",
    )(h, pos, rel, inv_lane, g, w_wide, qn, wuq, kvn, wukv)


def _moba_gate_kernel(mq_ref, km_ref, o_ref, *, tm):
    si = pl.program_id(1)
    nblk = LANES - MOBA_SEL
    blk = lax.broadcasted_iota(jnp.int32, (nblk, tm), 0)
    blkf = blk.astype(F32)
    own = (si * tm + _lane((1, tm))) // MOBA_BLOCK
    valid = blk < own
    ones_rows = _bias_rows(MOBA_SEL - MOBA_HD, tm)
    for h in range(MOBA_HEADS):
        qt = mq_ref[0, h]
        gate = jnp.dot(km_ref[0, h], qt, precision=lax.Precision.HIGHEST, preferred_element_type=F32)
        g = jnp.where(valid, gate, NEG)
        sel = blk == own
        for _ in range(MOBA_TOPK):
            mx = jnp.max(g, axis=0, keepdims=True)
            idx = jnp.min(jnp.where(g == mx, blkf, float(nblk)), axis=0, keepdims=True)
            pick = (blkf == idx) & (mx > 0.5 * NEG)
            sel = sel | pick
            g = jnp.where(pick, NEG, g)
        selbias = jnp.where(sel, 0.0, NEG)
        o_ref[0, h] = jnp.concatenate([qt * (MOBA_HD ** -0.5 * LOG2E), ones_rows, selbias],
                                      axis=0).astype(BF16)


def _moba_gate(mq, km_pad, *, tm):
    B, H, _, S = mq.shape
    return pl.pallas_call(
        functools.partial(_moba_gate_kernel, tm=tm),
        grid=(B, S // tm),
        in_specs=[pl.BlockSpec((1, H, MOBA_HD, tm), lambda b, i: (b, 0, 0, i)),
                  pl.BlockSpec((1, H, LANES - MOBA_SEL, MOBA_HD), lambda b, i: (b, 0, 0, 0))],
        out_specs=pl.BlockSpec((1, H, LANES, tm), lambda b, i: (b, 0, 0, i)),
        out_shape=jax.ShapeDtypeStruct((B, H, LANES, S), BF16),
        compiler_params=pltpu.CompilerParams(dimension_semantics=("parallel", "parallel"),
                                             vmem_limit_bytes=VMEM_LIMIT),
        name="moba_gate",
    )(mq, km_pad)


def _causal_flash(chains, next_chains, k_ref, vt_ref, scratch, qi, j0, j0_next, t, tk):
    assert t == 2 * tk
    s_a, s_b, mb_a, mb_b, m_scr, acc_scr = scratch
    nc = len(chains)

    every, first, second = slice(0, t), slice(0, tk), slice(tk, t)

    def scores(kt, s_dst, mb_dst, qs=every, chains=chains):
        tiles = {}
        for c, (hh, qt) in enumerate(chains):
            if hh not in tiles:
                tiles[hh] = k_ref[0, hh, pl.ds(pl.multiple_of(kt * tk, tk), tk), :]
            s = _dot(tiles[hh], qt[:, qs])
            s_dst[c, :, qs] = s
            mb_dst[c, :, qs] = jnp.max(s, axis=0, keepdims=True)

    def softmax_pv(kt, s_src, mb_src, qs=every, diagonal=False):
        for c, (hh, _) in enumerate(chains):
            s = s_src[c, :, qs]
            if diagonal:
                keep = (lax.broadcasted_iota(jnp.int32, (tk, tk), 0)
                        <= lax.broadcasted_iota(jnp.int32, (tk, tk), 1))
                s = jnp.where(keep, s, NEG)
                mb = jnp.max(s, axis=0, keepdims=True)
            else:
                mb = mb_src[c, :, qs]
            m = m_scr[c, :, qs]
            m_new = jnp.maximum(m, mb)
            p = jnp.exp2(s - m_new).astype(BF16)
            acc_scr[c, :, qs] = acc_scr[c, :, qs] * jnp.exp2(m - m_new) + _dot(vt_ref[0, hh, kt], p)
            m_scr[c, :, qs] = m_new

    for c in range(nc):
        m_scr[c] = jnp.full((1, t), NEG, F32)
        acc_scr[c] = jnp.zeros((V_ROWS, t), F32)

    @pl.when(qi == 0)
    def _():
        scores(2 * j0, s_a, mb_a)

    def pair(j, carry):
        scores(2 * j + 1, s_b, mb_b)
        softmax_pv(2 * j, s_a, mb_a)
        scores(2 * j + 2, s_a, mb_a)
        softmax_pv(2 * j + 1, s_b, mb_b)
        return carry

    unroll = UNROLL_CHAIN_PAIRS // nc

    def pairs(i, carry):
        for u in range(unroll):
            pair(j0 + unroll * i + u, carry)
        return carry

    n_main = (qi - j0) // unroll
    lax.fori_loop(0, n_main, pairs, 0)
    lax.fori_loop(j0 + n_main * unroll, qi, pair, 0)
    scores(2 * qi + 1, s_b, mb_b, second)
    softmax_pv(2 * qi, s_a, mb_a, first, diagonal=True)
    softmax_pv(2 * qi, s_a, mb_a, second)
    scores(2 * j0_next, s_a, mb_a, chains=next_chains)
    softmax_pv(2 * qi + 1, s_b, mb_b, second, diagonal=True)
    outs = []
    for c in range(nc):
        acc = acc_scr[c]
        outs.append(acc[:V_DIM] * (1.0 / acc[V_ONE:V_ONE + 1]))
    return outs


def _flash_scratch(nc, t, tk):
    return ([pltpu.VMEM((nc, tk, t), F32)] * 2 + [pltpu.VMEM((nc, 1, t), F32)] * 3
            + [pltpu.VMEM((nc, V_ROWS, t), F32)])


def _first_pairs(j0_ref):
    b, p, i = pl.program_id(0), pl.program_id(1), pl.program_id(2)
    step = (b * pl.num_programs(1) + p) * pl.num_programs(2) + i
    last = pl.num_programs(0) * pl.num_programs(1) * pl.num_programs(2) - 1
    return j0_ref[step], j0_ref[jnp.minimum(step + 1, last)]


def _mla_attn_kernel(j0_ref, q_ref, qnext_ref, k_ref, vt_ref, o_ref, *scratch, t, tk):
    heads = range(q_ref.shape[1])
    outs = _causal_flash([(hh, q_ref[0, hh]) for hh in heads], [(hh, qnext_ref[0, hh]) for hh in heads],
                         k_ref, vt_ref, scratch, pl.program_id(2), *_first_pairs(j0_ref), t, tk)
    o_ref[0] = jnp.concatenate(outs, axis=0).astype(o_ref.dtype)


def _diff_attn_kernel(j0_ref, lam_ref, gain_ref, q_ref, qnext_ref, k_ref, vt_ref, o_ref, *scratch, t, tk, lam_init):
    lv = lam_ref[...]
    lam = (jnp.exp(jnp.sum(lv[0:1] * lv[1:2], axis=-1, keepdims=True))
           - jnp.exp(jnp.sum(lv[2:3] * lv[3:4], axis=-1, keepdims=True)) + lam_init)
    feat = lax.broadcasted_iota(jnp.int32, (LANES, t), 0)

    def map_chains(ref):
        chains = []
        for hh in range(2):
            q = ref[0, hh]
            chains.append((hh, jnp.where(feat < DIFF_MAP2, q, jnp.zeros_like(q))))
            chains.append((hh, jnp.where(feat >= DIFF_MAP2, q, jnp.zeros_like(q))))
        return chains

    a0, b0, a1, b1 = _causal_flash(map_chains(q_ref), map_chains(qnext_ref), k_ref, vt_ref, scratch,
                                   pl.program_id(2), *_first_pairs(j0_ref), t, tk)
    outs = []
    for a, b in ((a0, b0), (a1, b1)):
        o = a - lam * b
        outs.append(o * lax.rsqrt(jnp.mean(o * o, axis=0, keepdims=True) + EPS))
    o_ref[0] = (jnp.concatenate(outs, axis=0) * gain_ref[...] * (1.0 - lam_init)).astype(o_ref.dtype)


def _group_attention(kernel_fn, q, k, vt, extra=(), first_pair=None, *, t, tk, group, chains, name):
    B, H, _, S = q.shape
    if first_pair is None:
        first_pair = jnp.zeros((B, H // group, S // t), jnp.int32)
    nq = S // t
    qspec = pl.BlockSpec((1, group, LANES, t), lambda b, p, i, j0: (b, p, 0, i))
    qnext = pl.BlockSpec((1, group, LANES, t), lambda b, p, i, j0: (b, p, 0, jnp.minimum(i + 1, nq - 1)))
    kspec = pl.BlockSpec((1, group, S, LANES), lambda b, p, i, j0: (b, p, 0, 0))
    vspec = pl.BlockSpec((1, group, S // tk, V_ROWS, tk), lambda b, p, i, j0: (b, p, 0, 0, 0))
    xspecs = [pl.BlockSpec(a.shape, lambda b, p, i, j0: (0, 0)) for a in extra]
    return pl.pallas_call(
        functools.partial(kernel_fn, t=t, tk=tk),
        grid_spec=pltpu.PrefetchScalarGridSpec(
            num_scalar_prefetch=1,
            grid=(B, H // group, S // t),
            in_specs=xspecs + [qspec, qnext, kspec, vspec],
            out_specs=pl.BlockSpec((1, group * V_DIM, t), lambda b, p, i, j0: (b, p, i)),
            scratch_shapes=_flash_scratch(chains, t, tk)),
        out_shape=jax.ShapeDtypeStruct((B, H * V_DIM, S), BF16),
        compiler_params=pltpu.CompilerParams(dimension_semantics=("parallel", "parallel", "arbitrary"),
                                             vmem_limit_bytes=VMEM_LIMIT),
        name=name,
    )(first_pair.reshape(-1), *extra, q, q, k, vt)


def _memkv_kernel(x_ref, g_ref, w_ref, o_ref):
    o_ref[0] = _dot(_rms(x_ref[0], g_ref[...]).astype(BF16), w_ref[...]).astype(o_ref.dtype)


def _memkv(mem, g, wkv):
    B, M, _ = mem.shape
    return pl.pallas_call(
        _memkv_kernel,
        grid=(B,),
        in_specs=[pl.BlockSpec((1, M, D_MODEL), lambda b: (b, 0, 0)), pl.BlockSpec(g.shape, lambda b: (0, 0)),
                  pl.BlockSpec(wkv.shape, lambda b: (0, 0))],
        out_specs=pl.BlockSpec((1, M, 2 * D_MODEL), lambda b: (b, 0, 0)),
        out_shape=jax.ShapeDtypeStruct((B, M, 2 * D_MODEL), BF16),
        compiler_params=pltpu.CompilerParams(dimension_semantics=("parallel",), vmem_limit_bytes=VMEM_LIMIT),
        name="memkv",
    )(mem, g, wkv)


def _post_kernel(h_ref, oa_ref, ob_ref, oc_ref, wout_ref, g_ref, wq_ref, kv_ref, wo_ref,
                 g2_ref, w1_ref, w2_ref, gf_ref, o_ref, *, final, chunk):
    na, nb = oa_ref.shape[1], ob_ref.shape[1]
    h1 = (h_ref[0] + _dot_tn(oa_ref[0], wout_ref[0:na]) + _dot_tn(ob_ref[0], wout_ref[na:na + nb])
          + _dot_tn(oc_ref[0], wout_ref[na + nb:]))
    n = _rms(h1, g_ref[...]).astype(BF16)
    q = (_dot(n, wq_ref[...]) * (CROSS_HD ** -0.5 * LOG2E)).astype(BF16)
    ctx = []
    for h in range(CROSS_HEADS):
        sl = slice(h * CROSS_HD, (h + 1) * CROSS_HD)
        s = _dot_nt(q[:, sl], kv_ref[0, :, sl])
        p = jnp.exp2(s - jnp.max(s, axis=-1, keepdims=True))
        l = jnp.sum(p, axis=-1, keepdims=True)
        c = _dot(p.astype(BF16), kv_ref[0, :, D_MODEL + h * CROSS_HD:D_MODEL + (h + 1) * CROSS_HD])
        ctx.append((c * (1.0 / l)).astype(BF16))
    h2 = h1 + _dot(jnp.concatenate(ctx, axis=-1), wo_ref[...])
    n = _rms(h2, g2_ref[...]).astype(BF16)
    acc = h2
    for c in range(D_FF // chunk):
        a = jnp.maximum(_dot(n, w1_ref[:, c * chunk:(c + 1) * chunk]), 0.0)
        acc = acc + _dot((a * a).astype(BF16), w2_ref[c * chunk:(c + 1) * chunk, :])
    o_ref[0] = _rms(acc, gf_ref[...]) if final else acc


def _post(h, oa, ob, oc, wout, g, wq, memkv, wo, g2, w1, w2, gf, *, tm, final):
    B, S, _ = h.shape
    tok = lambda b, i: (b, i, 0)
    const2 = lambda b, i: (0, 0)
    small = lambda a: pl.BlockSpec(a.shape, const2)
    resident = lambda a: pl.BlockSpec(a.shape, const2, pipeline_mode=pl.Buffered(1))
    return pl.pallas_call(
        functools.partial(_post_kernel, final=final, chunk=D_MODEL),
        grid=(B, S // tm),
        in_specs=[pl.BlockSpec((1, tm, D_MODEL), tok)]
                 + [pl.BlockSpec((1, o.shape[1], tm), lambda b, i: (b, 0, i)) for o in (oa, ob, oc)] + [
                  resident(wout), small(g), resident(wq),
                  pl.BlockSpec((1,) + memkv.shape[1:], lambda b, i: (b, 0, 0)), resident(wo),
                  small(g2), resident(w1), resident(w2), small(gf)],
        out_specs=pl.BlockSpec((1, tm, D_MODEL), tok),
        out_shape=jax.ShapeDtypeStruct(h.shape, F32),
        compiler_params=pltpu.CompilerParams(dimension_semantics=("parallel", "parallel"),
                                             vmem_limit_bytes=VMEM_LIMIT),
        name="post",
    )(h, oa, ob, oc, wout, g, wq, memkv, wo, g2, w1, w2, gf)


def _rot_pairs(w):
    half = w.shape[-1] // 2
    return jnp.concatenate([-w[..., half:], w[..., :half]], axis=-1)


def _pad_lanes(w, lo, width=LANES):
    pad = [(0, 0)] * (w.ndim - 1) + [(lo, width - lo - w.shape[-1])]
    return jnp.pad(w, pad)


def _head_groups(w, nheads, hd, lo=0, width=LANES):
    K = w.shape[0]
    return _pad_lanes(w.reshape(K, nheads, hd), lo, width).reshape(K, nheads * width)


def _widen_w_in(w):
    cuts = np.cumsum([0, MLA_Q_RANK, MLA_KV_RANK, MLA_ROPE, C_DIFF, C_DIFF, C_DIFF, C_MOBA, C_MOBA, C_MOBA])
    cq, ckv, kr, dq, dk, dv, mq, mk, mv = [w[:, int(a):int(b)] for a, b in zip(cuts[:-1], cuts[1:])]
    two_maps = lambda x: _head_groups(x, 2 * DIFF_HEADS, DIFF_HD, 0, HALF)
    out = jnp.concatenate([
        cq, ckv, _pad_lanes(kr, MLA_NOPE), _pad_lanes(_rot_pairs(kr), MLA_NOPE),
        dq, two_maps(dk), dv, mq, _head_groups(mk, MOBA_HEADS, MOBA_HD), mv], axis=1).astype(BF16)
    assert out.shape[1] == C_WIDE
    return out


def _widen_w_uq(w):
    K = w.shape[0]
    e = w.reshape(K, MLA_HEADS, MLA_NOPE + MLA_ROPE)
    nope = e[..., :MLA_NOPE].reshape(K, MLA_HEADS * MLA_NOPE)
    rope = _pad_lanes(e[..., MLA_NOPE:].reshape(K, MLA_HEADS * MLA_ROPE), 0, 2 * LANES)
    rot = _pad_lanes(_rot_pairs(e[..., MLA_NOPE:]).reshape(K, MLA_HEADS * MLA_ROPE), 0, 2 * LANES)
    out = jnp.concatenate([nope, rope, rot], axis=1).astype(BF16)
    assert out.shape[1] == UQ_COLS
    return out


def _widen_w_ukv(w):
    K = w.shape[0]
    e = w.reshape(K, MLA_HEADS, MLA_NOPE + MLA_V)
    kn = _pad_lanes(e[..., :MLA_NOPE], 0).reshape(K, MLA_HEADS * LANES)
    vv = e[..., MLA_NOPE:].reshape(K, MLA_HEADS * MLA_V)
    out = jnp.concatenate([kn, vv], axis=1).astype(BF16)
    assert out.shape[1] == UKV_COLS
    return out


def _diff_first_pairs(qnorm, knorm, rel, t, tk):
    B, nq = qnorm.shape[:2]
    nm = 2 * DIFF_HEADS
    qn = qnorm[:, :, 0, :nm]
    kn = jnp.max(knorm[:, :, 0, :nm], axis=1)
    spread = NORM_MARGIN * 2.0 * qn * kn[:, None, :]
    relf = rel[..., 0].astype(F32)
    dist = relf[:, ::t][:, :, None] - relf[:, tk - 1::tk][:, None, :]
    c = jnp.asarray([_alibi_slope(i // 2) * LOG2E for i in range(nm)], F32)
    dead = c[None, None, :, None] * dist[:, :, None, :] > spread[..., None] + UNDERFLOW_BITS
    tiles = jnp.sum(dead.astype(jnp.int32), axis=-1)
    pairs = jnp.min(tiles.reshape(B, nq, DIFF_HEADS // 2, 4), axis=-1) // 2
    return pairs.transpose(0, 2, 1)


def kernel(x, mem, positions, attn_norm, w_in, mla_q_norm, mla_w_uq, mla_kv_norm, mla_w_ukv, diff_lambda_q1, diff_lambda_k1, diff_lambda_q2, diff_lambda_k2, diff_sub_norm, w_out, cross_norm, mem_norm, cross_wq, cross_wkv, cross_wo, mlp_norm, mlp_w1, mlp_w2, final_norm):
    B, S, _ = x.shape
    depth = w_in.shape[0]
    tm = 512
    t = 512
    tk = t // 2
    assert S % tm == 0 and tm == t and tk % MOBA_BLOCK == 0 and S // MOBA_BLOCK <= LANES - MOBA_SEL

    pos = positions.astype(jnp.int32)[:, None, :]
    rel = (positions - positions[:, :1]).astype(jnp.int32)[..., None]
    half = MLA_ROPE // 2
    inv_lane = (ROPE_THETA ** (-jnp.arange(half, dtype=F32) / half))[:, None]
    row = lambda v: v.astype(F32)[None, :]
    wout_b, wq_b, wkv_b, wo_b, w1_b, w2_b = (w.astype(BF16) for w in (w_out, cross_wq, cross_wkv, cross_wo,
                                                                       mlp_w1, mlp_w2))

    h = x
    for l in range(depth):
        outs = _inproj(h, pos, rel, inv_lane, row(attn_norm[l]), _widen_w_in(w_in[l]), row(mla_q_norm[l]),
                       _widen_w_uq(mla_w_uq[l]), row(mla_kv_norm[l]), _widen_w_ukv(mla_w_ukv[l]), tm=tm, t=tk)
        qa, ka, va, qd, kd, vd, mq, mk, mv, kmean, qnorm, knorm = outs
        km = kmean.reshape(B, S // MOBA_BLOCK, MOBA_HEADS, LANES)[..., :MOBA_HD].transpose(0, 2, 1, 3)
        km_pad = jnp.pad(km, ((0, 0), (0, 0), (0, LANES - MOBA_SEL - S // MOBA_BLOCK), (0, 0)))
        mq_aug = _moba_gate(mq, km_pad, tm=tm)

        attn = functools.partial(_group_attention, t=t, tk=tk)
        o_a = attn(_mla_attn_kernel, qa, ka, va, group=3, chains=3, name="mla_attn")
        lam_rows = jnp.stack([diff_lambda_q1[l], diff_lambda_k1[l], diff_lambda_q2[l], diff_lambda_k2[l]])
        lam_rows = jnp.pad(lam_rows.astype(F32), ((0, 4), (0, LANES - DIFF_HD)))
        gain = jnp.tile(diff_sub_norm[l].astype(F32), 2)[:, None]
        lam_init = 0.8 - 0.6 * math.exp(-0.3 * l)
        o_b = attn(functools.partial(_diff_attn_kernel, lam_init=lam_init), qd, kd, vd,
                   extra=(lam_rows, gain), first_pair=_diff_first_pairs(qnorm, knorm, rel, t, tk),
                   group=2, chains=4, name="diff_attn")
        o_c = attn(_mla_attn_kernel, mq_aug, mk, mv, group=4, chains=4, name="moba_attn")

        memkv = _memkv(mem, row(mem_norm[l]), wkv_b[l])
        h = _post(h, o_a, o_b, o_c, wout_b[l], row(cross_norm[l]), wq_b[l], memkv, wo_b[l],
                  row(mlp_norm[l]), w1_b[l], w2_b[l], row(final_norm), tm=tm, final=(l == depth - 1))
    return h
```

```python
import functools
import math

import jax
import jax.numpy as jnp
import numpy as np
from jax import lax
from jax.experimental import pallas as pl
from jax.experimental.pallas import tpu as pltpu

D_MODEL = 1024
MLA_HEADS = 6
MLA_NOPE = 64
MLA_ROPE = 32
MLA_V = 64
MLA_Q_RANK = 256
MLA_KV_RANK = 128
ROPE_THETA = 10000.0
DIFF_HEADS = 6
DIFF_HD = 32
MOBA_HEADS = 4
MOBA_HD = 64
MOBA_BLOCK = 256
MOBA_TOPK = 3
CROSS_HEADS = 4
CROSS_HD = D_MODEL // CROSS_HEADS
D_FF = 4 * D_MODEL
EPS = 1e-6
NEG = -1e30
N_ALIBI = DIFF_HEADS + MOBA_HEADS
C_DIFF = DIFF_HEADS * 2 * DIFF_HD
C_MOBA = MOBA_HEADS * MOBA_HD

LANES = 128
HALF = LANES // 2
LOG2E = 1.4426950408889634
VMEM_LIMIT = 56 * 1024 * 1024

F32 = jnp.float32
BF16 = jnp.bfloat16

V_DIM = 64
V_ONE = V_DIM
V_ROWS = 80
DIFF_MAP2 = HALF
DIFF_BIAS = DIFF_HD
MOBA_BIAS = MOBA_HD
MOBA_SEL = 96
N_SPLIT = 3
UNROLL_CHAIN_PAIRS = 12
MAX_UNROLL = 4
DIFF_GROUP = 2
UNDERFLOW_BITS = 152.0
NORM_MARGIN = 1.02

O_CQ = 0
O_CKV = O_CQ + MLA_Q_RANK
O_KR = O_CKV + MLA_KV_RANK
O_KRS = O_KR + LANES
O_DQ = O_KRS + LANES
O_DK = O_DQ + C_DIFF
O_DV = O_DK + DIFF_HEADS * LANES
O_MQ = O_DV + C_DIFF
O_MK = O_MQ + C_MOBA
O_MV = O_MK + MOBA_HEADS * LANES
C_WIDE = O_MV + C_MOBA
UQ_NOPE = 0
UQ_ROPE = MLA_HEADS * MLA_NOPE
UQ_ROT = UQ_ROPE + 2 * LANES
UQ_COLS = UQ_ROT + 2 * LANES
UKV_V = MLA_HEADS * LANES
UKV_COLS = UKV_V + MLA_HEADS * MLA_V


def _alibi_slope(h):
    return 2.0 ** (-8.0 * (h + 1) / N_ALIBI)


def _rms(x, g):
    return x * lax.rsqrt(jnp.mean(x * x, axis=-1, keepdims=True) + EPS) * g


def _dot(a, b):
    return jnp.dot(a, b, preferred_element_type=F32)


def _dot_nt(a, b):
    return lax.dot_general(a, b, (((1,), (1,)), ((), ())), preferred_element_type=F32)


def _dot_tn(a, b):
    return lax.dot_general(a, b, (((0,), (0,)), ((), ())), preferred_element_type=F32)


def _lane(shape):
    return lax.broadcasted_iota(jnp.int32, shape, len(shape) - 1)


def _onehot_lanes(lanes, value=1.0):
    l = _lane((1, LANES))
    out = jnp.zeros((1, LANES), F32)
    for i in lanes:
        out = jnp.where(l == i, value, out)
    return out


def _split3(x):
    hi = x.astype(BF16).astype(F32)
    r = x - hi
    mid = r.astype(BF16).astype(F32)
    lo = (r - mid).astype(BF16).astype(F32)
    return hi, mid, lo


def _store_vt(ref, h, vt, t):
    tm = vt.shape[1]
    tail = jnp.where(lax.broadcasted_iota(jnp.int32, (V_ROWS - V_DIM, tm), 0) == 0, 1.0, 0.0)
    x = jnp.concatenate([vt, tail], axis=0).astype(BF16)
    for j in range(tm // t):
        ref[0, h, j] = x[:, j * t:(j + 1) * t]


def _bias_rows(n, tm):
    return jnp.where(lax.broadcasted_iota(jnp.int32, (n, tm), 0) < N_SPLIT, 1.0, 0.0)


def _inproj_kernel(h_ref, pos_ref, rel_ref, inv_ref, g_ref, w_ref, qn_ref, wuq_ref, kvn_ref, wukv_ref,
                   qa_ref, ka_ref, va_ref, qd_ref, kd_ref, vd_ref, mq_ref, mk_ref, mv_ref, kmean_ref,
                   qnorm_ref, knorm_ref, *, tm, t):
    si = pl.program_id(1)
    nb = _rms(h_ref[0], g_ref[...]).astype(BF16)
    group = lambda x, o, i: x[:, o + i * LANES:o + (i + 1) * LANES]

    pm = _dot(nb, w_ref[:, O_CQ:O_DQ])
    cq = _rms(pm[:, O_CQ:O_CKV], qn_ref[...]).astype(BF16)
    ckv = _rms(pm[:, O_CKV:O_KR], kvn_ref[...]).astype(BF16)
    q2 = _dot(cq, wuq_ref[...])
    kv2 = _dot(ckv, wukv_ref[...])
    ang = inv_ref[...] * pos_ref[0].astype(F32)
    reps = LANES // ang.shape[0]
    cs = jnp.concatenate([jnp.cos(ang)] * reps, axis=0).T
    sn = jnp.concatenate([jnp.sin(ang)] * reps, axis=0).T
    krope = pm[:, O_KR:O_KRS] * cs + pm[:, O_KRS:O_DQ] * sn
    qscale = (MLA_NOPE + MLA_ROPE) ** -0.5 * LOG2E
    rope_t = [(group(q2, UQ_ROPE, i) * cs + group(q2, UQ_ROT, i) * sn).T for i in range(2)]
    pad_rows = jnp.zeros((LANES - MLA_NOPE - MLA_ROPE, tm), F32)
    for p in range(MLA_HEADS // 2):
        nope_t = group(q2, UQ_NOPE, p).T
        v_t = group(kv2, UKV_V, p).T
        for e in range(2):
            h = 2 * p + e
            r = (h % 4) * MLA_ROPE
            qa = jnp.concatenate([nope_t[e * HALF:(e + 1) * HALF], rope_t[h // 4][r:r + MLA_ROPE], pad_rows], axis=0)
            qa_ref[0, h] = (qa * qscale).astype(BF16)
            _store_vt(va_ref, h, v_t[e * HALF:(e + 1) * HALF], t)
    for h in range(MLA_HEADS):
        ka_ref[0, h] = (group(kv2, 0, h) + krope).astype(BF16)

    relf = rel_ref[0].astype(F32)

    pq = _dot(nb, w_ref[:, O_DQ:O_DK])
    pk = _dot(nb, w_ref[:, O_DK:O_DV])
    pv = _dot(nb, w_ref[:, O_DV:O_MQ])
    dscale = DIFF_HD ** -0.5 * LOG2E
    ones_rows = _bias_rows(DIFF_MAP2 - DIFF_HD, tm)
    lane1 = _lane((1, LANES))
    qn_row = jnp.zeros((1, LANES), F32)
    kn_row = jnp.zeros((1, LANES), F32)
    for p in range(DIFF_HEADS // 2):
        q_t = group(pq, 0, p).T * dscale
        v_t = group(pv, 0, p).T
        for e in range(2):
            o = e * HALF
            qd_ref[0, 2 * p + e] = jnp.concatenate(
                [q_t[o:o + DIFF_HD], ones_rows, q_t[o + DIFF_HD:o + HALF], ones_rows], axis=0).astype(BF16)
            _store_vt(vd_ref, 2 * p + e, v_t[o:o + HALF], t)
            for mp in range(2):
                qm = q_t[o + mp * DIFF_HD:o + (mp + 1) * DIFF_HD]
                qsq = jnp.max(jnp.sum(qm * qm, axis=0, keepdims=True), axis=1, keepdims=True)
                qn_row = jnp.where(lane1 == 2 * (2 * p + e) + mp, jnp.sqrt(qsq), qn_row)
    in_map1 = _lane((tm, LANES)) < DIFF_MAP2
    for h in range(DIFF_HEADS):
        kb = group(pk, 0, h)
        for mp in range(2):
            mine = in_map1 if mp == 0 else jnp.logical_not(in_map1)
            ksq = jnp.sum(jnp.where(mine, kb * kb, 0.0), axis=1, keepdims=True)
            kn_row = jnp.where(lane1 == 2 * h + mp, jnp.sqrt(jnp.max(ksq, axis=0, keepdims=True)), kn_row)
        for i, piece in enumerate(_split3(relf * (_alibi_slope(h) * LOG2E))):
            kb = kb + piece * _onehot_lanes([DIFF_BIAS + i, DIFF_MAP2 + DIFF_BIAS + i])
        kd_ref[0, h] = kb.astype(BF16)
    qnorm_ref[0, 0] = qn_row
    knorm_ref[0, 0] = kn_row

    pq = _dot(nb, w_ref[:, O_MQ:O_MK])
    pk = _dot(nb, w_ref[:, O_MK:O_MV])
    pv = _dot(nb, w_ref[:, O_MV:C_WIDE])
    for p in range(MOBA_HEADS // 2):
        q_t = group(pq, 0, p).T
        v_t = group(pv, 0, p).T
        for e in range(2):
            mq_ref[0, 2 * p + e] = q_t[e * HALF:(e + 1) * HALF]
            _store_vt(mv_ref, 2 * p + e, v_t[e * HALF:(e + 1) * HALF], t)
    row = lax.broadcasted_iota(jnp.int32, (tm, 1), 0)
    blk = (si * tm + row) // MOBA_BLOCK
    blk_onehot = jnp.where(_lane((tm, LANES)) == MOBA_SEL + blk, 1.0, 0.0)
    for h in range(MOBA_HEADS):
        kb = group(pk, 0, h) + blk_onehot
        for i, piece in enumerate(_split3(relf * (_alibi_slope(DIFF_HEADS + h) * LOG2E))):
            kb = kb + piece * _onehot_lanes([MOBA_BIAS + i])
        mk_ref[0, h] = kb.astype(BF16)
    for j in range(tm // MOBA_BLOCK):
        kmean_ref[0, j] = jnp.mean(pk[j * MOBA_BLOCK:(j + 1) * MOBA_BLOCK, :], axis=0, keepdims=True)


def _layer_spec(a, l, single_buffer=False):
    zeros = (0,) * (a.ndim - 1)
    mode = dict(pipeline_mode=pl.Buffered(1)) if single_buffer else {}
    return pl.BlockSpec((None,) + a.shape[1:], lambda *_: (l,) + zeros, **mode)


def _inproj(h, pos, rel, inv_lane, g, w_wide, qn, wuq, kvn, wukv, *, l, tm, t):
    B, S, _ = h.shape
    grid = (B, S // tm)
    tok = lambda b, i: (b, i, 0)
    const2 = lambda b, i: (0, 0)
    q_out = lambda nh: (jax.ShapeDtypeStruct((B, nh, LANES, S), BF16),
                        pl.BlockSpec((1, nh, LANES, tm), lambda b, i: (b, 0, 0, i)))
    k_out = lambda nh, dt: (jax.ShapeDtypeStruct((B, nh, S, LANES), dt),
                            pl.BlockSpec((1, nh, tm, LANES), lambda b, i: (b, 0, i, 0)))
    v_out = lambda nh: (jax.ShapeDtypeStruct((B, nh, S // t, V_ROWS, t), BF16),
                        pl.BlockSpec((1, nh, tm // t, V_ROWS, t), lambda b, i: (b, 0, i, 0, 0)))
    outs = [q_out(MLA_HEADS), k_out(MLA_HEADS, BF16), v_out(MLA_HEADS),
            q_out(DIFF_HEADS), k_out(DIFF_HEADS, BF16), v_out(DIFF_HEADS),
            (jax.ShapeDtypeStruct((B, MOBA_HEADS, MOBA_HD, S), F32),
             pl.BlockSpec((1, MOBA_HEADS, MOBA_HD, tm), lambda b, i: (b, 0, 0, i))),
            k_out(MOBA_HEADS, BF16), v_out(MOBA_HEADS),
            (jax.ShapeDtypeStruct((B, S // MOBA_BLOCK, 1, MOBA_HEADS * LANES), F32),
             pl.BlockSpec((1, tm // MOBA_BLOCK, 1, MOBA_HEADS * LANES), lambda b, i: (b, i, 0, 0)))]
    outs += [(jax.ShapeDtypeStruct((B, S // tm, 1, LANES), F32),
              pl.BlockSpec((1, 1, 1, LANES), lambda b, i: (b, i, 0, 0)))] * 2
    full = lambda a: pl.BlockSpec(a.shape, const2)
    return pl.pallas_call(
        functools.partial(_inproj_kernel, tm=tm, t=t),
        grid=grid,
        in_specs=[pl.BlockSpec((1, tm, D_MODEL), tok), pl.BlockSpec((1, 1, tm), lambda b, i: (b, 0, i)),
                  pl.BlockSpec((1, tm, 1), tok), full(inv_lane)]
                 + [_layer_spec(a, l) for a in (g, w_wide, qn, wuq, kvn, wukv)],
        out_specs=[o[1] for o in outs],
        out_shape=[o[0] for o in outs],
        compiler_params=pltpu.CompilerParams(dimension_semantics=("parallel", "parallel"),
                                             vmem_limit_bytes=VMEM_LIMIT),
        name="inproj",
    )(h, pos, rel, inv_lane, g, w_wide, qn, wuq, kvn, wukv)


def _moba_gate_kernel(mq_ref, km_ref, o_ref, *, tm):
    si = pl.program_id(1)
    nblk = LANES - MOBA_SEL
    blk = lax.broadcasted_iota(jnp.int32, (nblk, tm), 0)
    blkf = blk.astype(F32)
    own = (si * tm + _lane((1, tm))) // MOBA_BLOCK
    valid = blk < own
    ones_rows = _bias_rows(MOBA_SEL - MOBA_HD, tm)
    for h in range(MOBA_HEADS):
        qt = mq_ref[0, h]
        gate = jnp.dot(km_ref[0, h], qt, precision=lax.Precision.HIGHEST, preferred_element_type=F32)
        g = jnp.where(valid, gate, NEG)
        sel = blk == own
        for _ in range(MOBA_TOPK):
            mx = jnp.max(g, axis=0, keepdims=True)
            idx = jnp.min(jnp.where(g == mx, blkf, float(nblk)), axis=0, keepdims=True)
            pick = (blkf == idx) & (mx > 0.5 * NEG)
            sel = sel | pick
            g = jnp.where(pick, NEG, g)
        selbias = jnp.where(sel, 0.0, NEG)
        o_ref[0, h] = jnp.concatenate([qt * (MOBA_HD ** -0.5 * LOG2E), ones_rows, selbias],
                                      axis=0).astype(BF16)


def _moba_gate(mq, km_pad, *, tm):
    B, H, _, S = mq.shape
    return pl.pallas_call(
        functools.partial(_moba_gate_kernel, tm=tm),
        grid=(B, S // tm),
        in_specs=[pl.BlockSpec((1, H, MOBA_HD, tm), lambda b, i: (b, 0, 0, i)),
                  pl.BlockSpec((1, H, LANES - MOBA_SEL, MOBA_HD), lambda b, i: (b, 0, 0, 0))],
        out_specs=pl.BlockSpec((1, H, LANES, tm), lambda b, i: (b, 0, 0, i)),
        out_shape=jax.ShapeDtypeStruct((B, H, LANES, S), BF16),
        compiler_params=pltpu.CompilerParams(dimension_semantics=("parallel", "parallel"),
                                             vmem_limit_bytes=VMEM_LIMIT),
        name="moba_gate",
    )(mq, km_pad)


def _causal_flash(chains, next_chains, k_ref, vt_ref, scratch, qi, j0, j0_next, t, tk):
    assert t == 2 * tk
    s_a, s_b, mb_a, mb_b, m_scr, acc_scr = scratch
    nc = len(chains)

    every, first, second = slice(0, t), slice(0, tk), slice(tk, t)

    def scores(kt, s_dst, mb_dst, qs=every, chains=chains):
        tiles = {}
        for c, (hh, qt) in enumerate(chains):
            if hh not in tiles:
                tiles[hh] = k_ref[0, hh, pl.ds(pl.multiple_of(kt * tk, tk), tk), :]
            s = _dot(tiles[hh], qt[:, qs])
            s_dst[c, :, qs] = s
            mb_dst[c, :, qs] = jnp.max(s, axis=0, keepdims=True)

    def softmax_pv(kt, s_src, mb_src, qs=every, diagonal=False):
        for c, (hh, _) in enumerate(chains):
            s = s_src[c, :, qs]
            if diagonal:
                keep = (lax.broadcasted_iota(jnp.int32, (tk, tk), 0)
                        <= lax.broadcasted_iota(jnp.int32, (tk, tk), 1))
                s = jnp.where(keep, s, NEG)
                mb = jnp.max(s, axis=0, keepdims=True)
            else:
                mb = mb_src[c, :, qs]
            m = m_scr[c, :, qs]
            m_new = jnp.maximum(m, mb)
            p = jnp.exp2(s - m_new).astype(BF16)
            acc_scr[c, :, qs] = acc_scr[c, :, qs] * jnp.exp2(m - m_new) + _dot(vt_ref[0, hh, kt], p)
            m_scr[c, :, qs] = m_new

    for c in range(nc):
        m_scr[c] = jnp.full((1, t), NEG, F32)
        acc_scr[c] = jnp.zeros((V_ROWS, t), F32)

    @pl.when(qi == 0)
    def _():
        scores(2 * j0, s_a, mb_a)

    def pair(j, carry):
        scores(2 * j + 1, s_b, mb_b)
        softmax_pv(2 * j, s_a, mb_a)
        scores(2 * j + 2, s_a, mb_a)
        softmax_pv(2 * j + 1, s_b, mb_b)
        return carry

    unroll = min(MAX_UNROLL, UNROLL_CHAIN_PAIRS // nc)

    def pairs(i, carry):
        for u in range(unroll):
            pair(j0 + unroll * i + u, carry)
        return carry

    n_main = (qi - j0) // unroll
    lax.fori_loop(0, n_main, pairs, 0)
    lax.fori_loop(j0 + n_main * unroll, qi, pair, 0)
    scores(2 * qi + 1, s_b, mb_b, second)
    softmax_pv(2 * qi, s_a, mb_a, first, diagonal=True)
    softmax_pv(2 * qi, s_a, mb_a, second)
    scores(2 * j0_next, s_a, mb_a, chains=next_chains)
    softmax_pv(2 * qi + 1, s_b, mb_b, second, diagonal=True)
    outs = []
    for c in range(nc):
        acc = acc_scr[c]
        outs.append(acc[:V_DIM] * (1.0 / acc[V_ONE:V_ONE + 1]))
    return outs


def _flash_scratch(nc, t, tk):
    return ([pltpu.VMEM((nc, tk, t), F32)] * 2 + [pltpu.VMEM((nc, 1, t), F32)] * 3
            + [pltpu.VMEM((nc, V_ROWS, t), F32)])


def _first_pairs(j0_ref):
    b, p, i = pl.program_id(0), pl.program_id(1), pl.program_id(2)
    step = (b * pl.num_programs(1) + p) * pl.num_programs(2) + i
    last = pl.num_programs(0) * pl.num_programs(1) * pl.num_programs(2) - 1
    return j0_ref[step], j0_ref[jnp.minimum(step + 1, last)]


def _mla_attn_kernel(j0_ref, q_ref, qnext_ref, k_ref, vt_ref, o_ref, *scratch, t, tk):
    heads = range(q_ref.shape[1])
    outs = _causal_flash([(hh, q_ref[0, hh]) for hh in heads], [(hh, qnext_ref[0, hh]) for hh in heads],
                         k_ref, vt_ref, scratch, pl.program_id(2), *_first_pairs(j0_ref), t, tk)
    o_ref[0] = jnp.concatenate(outs, axis=0).astype(o_ref.dtype)


def _diff_attn_kernel(j0_ref, lam_ref, gain_ref, q_ref, qnext_ref, k_ref, vt_ref, o_ref, *scratch, t, tk, lam_init):
    lv = lam_ref[...]
    lam = (jnp.exp(jnp.sum(lv[0:1] * lv[1:2], axis=-1, keepdims=True))
           - jnp.exp(jnp.sum(lv[2:3] * lv[3:4], axis=-1, keepdims=True)) + lam_init)
    feat = lax.broadcasted_iota(jnp.int32, (LANES, t), 0)

    def map_chains(ref):
        chains = []
        for hh in range(ref.shape[1]):
            q = ref[0, hh]
            chains.append((hh, jnp.where(feat < DIFF_MAP2, q, jnp.zeros_like(q))))
            chains.append((hh, jnp.where(feat >= DIFF_MAP2, q, jnp.zeros_like(q))))
        return chains

    res = _causal_flash(map_chains(q_ref), map_chains(qnext_ref), k_ref, vt_ref, scratch,
                        pl.program_id(2), *_first_pairs(j0_ref), t, tk)
    outs = []
    for a, b in zip(res[0::2], res[1::2]):
        o = a - lam * b
        outs.append(o * lax.rsqrt(jnp.mean(o * o, axis=0, keepdims=True) + EPS))
    o_ref[0] = (jnp.concatenate(outs, axis=0) * gain_ref[...] * (1.0 - lam_init)).astype(o_ref.dtype)


def _group_attention(kernel_fn, q, k, vt, extra=(), first_pair=None, *, t, tk, group, chains, name):
    B, H, _, S = q.shape
    if first_pair is None:
        first_pair = jnp.zeros((B, H // group, S // t), jnp.int32)
    nq = S // t
    qspec = pl.BlockSpec((1, group, LANES, t), lambda b, p, i, j0: (b, p, 0, i))
    qnext = pl.BlockSpec((1, group, LANES, t), lambda b, p, i, j0: (b, p, 0, jnp.minimum(i + 1, nq - 1)))
    kspec = pl.BlockSpec((1, group, S, LANES), lambda b, p, i, j0: (b, p, 0, 0))
    vspec = pl.BlockSpec((1, group, S // tk, V_ROWS, tk), lambda b, p, i, j0: (b, p, 0, 0, 0))
    xspecs = [pl.BlockSpec(a.shape, lambda b, p, i, j0: (0, 0)) for a in extra]
    return pl.pallas_call(
        functools.partial(kernel_fn, t=t, tk=tk),
        grid_spec=pltpu.PrefetchScalarGridSpec(
            num_scalar_prefetch=1,
            grid=(B, H // group, S // t),
            in_specs=xspecs + [qspec, qnext, kspec, vspec],
            out_specs=pl.BlockSpec((1, group * V_DIM, t), lambda b, p, i, j0: (b, p, i)),
            scratch_shapes=_flash_scratch(chains, t, tk)),
        out_shape=jax.ShapeDtypeStruct((B, H * V_DIM, S), BF16),
        compiler_params=pltpu.CompilerParams(dimension_semantics=("parallel", "parallel", "arbitrary"),
                                             vmem_limit_bytes=VMEM_LIMIT),
        name=name,
    )(first_pair.reshape(-1), *extra, q, q, k, vt)


def _memkv_kernel(x_ref, g_ref, w_ref, o_ref):
    o_ref[0] = _dot(_rms(x_ref[0], g_ref[...]).astype(BF16), w_ref[...]).astype(o_ref.dtype)


def _memkv(mem, g, wkv, *, l):
    B, M, _ = mem.shape
    return pl.pallas_call(
        _memkv_kernel,
        grid=(B,),
        in_specs=[pl.BlockSpec((1, M, D_MODEL), lambda b: (b, 0, 0)), _layer_spec(g, l), _layer_spec(wkv, l)],
        out_specs=pl.BlockSpec((1, M, 2 * D_MODEL), lambda b: (b, 0, 0)),
        out_shape=jax.ShapeDtypeStruct((B, M, 2 * D_MODEL), BF16),
        compiler_params=pltpu.CompilerParams(dimension_semantics=("parallel",), vmem_limit_bytes=VMEM_LIMIT),
        name="memkv",
    )(mem, g, wkv)


def _post_kernel(h_ref, oa_ref, ob_ref, oc_ref, wout_ref, g_ref, wq_ref, kv_ref, wo_ref,
                 g2_ref, w1_ref, w2_ref, gf_ref, o_ref, *, final, chunk):
    na, nb = oa_ref.shape[1], ob_ref.shape[1]
    h1 = (h_ref[0] + _dot_tn(oa_ref[0], wout_ref[0:na]) + _dot_tn(ob_ref[0], wout_ref[na:na + nb])
          + _dot_tn(oc_ref[0], wout_ref[na + nb:]))
    n = _rms(h1, g_ref[...]).astype(BF16)
    q = (_dot(n, wq_ref[...]) * (CROSS_HD ** -0.5 * LOG2E)).astype(BF16)
    ctx = []
    for h in range(CROSS_HEADS):
        sl = slice(h * CROSS_HD, (h + 1) * CROSS_HD)
        s = _dot_nt(q[:, sl], kv_ref[0, :, sl])
        p = jnp.exp2(s - jnp.max(s, axis=-1, keepdims=True))
        l = jnp.sum(p, axis=-1, keepdims=True)
        c = _dot(p.astype(BF16), kv_ref[0, :, D_MODEL + h * CROSS_HD:D_MODEL + (h + 1) * CROSS_HD])
        ctx.append((c * (1.0 / l)).astype(BF16))
    h2 = h1 + _dot(jnp.concatenate(ctx, axis=-1), wo_ref[...])
    n = _rms(h2, g2_ref[...]).astype(BF16)
    acc = h2
    for c in range(D_FF // chunk):
        a = jnp.maximum(_dot(n, w1_ref[:, c * chunk:(c + 1) * chunk]), 0.0)
        acc = acc + _dot((a * a).astype(BF16), w2_ref[c * chunk:(c + 1) * chunk, :])
    o_ref[0] = _rms(acc, gf_ref[...]) if final else acc


def _post(h, oa, ob, oc, wout, g, wq, memkv, wo, g2, w1, w2, gf, *, l, tm, final):
    B, S, _ = h.shape
    tok = lambda b, i: (b, i, 0)
    const2 = lambda b, i: (0, 0)
    small = lambda a: pl.BlockSpec(a.shape, const2)
    resident = lambda a: _layer_spec(a, l, single_buffer=True)
    return pl.pallas_call(
        functools.partial(_post_kernel, final=final, chunk=D_MODEL),
        grid=(B, S // tm),
        in_specs=[pl.BlockSpec((1, tm, D_MODEL), tok)]
                 + [pl.BlockSpec((1, o.shape[1], tm), lambda b, i: (b, 0, i)) for o in (oa, ob, oc)] + [
                  resident(wout), _layer_spec(g, l), resident(wq),
                  pl.BlockSpec((1,) + memkv.shape[1:], lambda b, i: (b, 0, 0)), resident(wo),
                  _layer_spec(g2, l), resident(w1), resident(w2), small(gf)],
        out_specs=pl.BlockSpec((1, tm, D_MODEL), tok),
        out_shape=jax.ShapeDtypeStruct(h.shape, F32),
        compiler_params=pltpu.CompilerParams(dimension_semantics=("parallel", "parallel"),
                                             vmem_limit_bytes=VMEM_LIMIT),
        name="post",
    )(h, oa, ob, oc, wout, g, wq, memkv, wo, g2, w1, w2, gf)


def _rot_pairs(w):
    half = w.shape[-1] // 2
    return jnp.concatenate([-w[..., half:], w[..., :half]], axis=-1)


def _pad_lanes(w, lo, width=LANES):
    pad = [(0, 0)] * (w.ndim - 1) + [(lo, width - lo - w.shape[-1])]
    return jnp.pad(w, pad)


def _head_groups(w, nheads, hd, lo=0, width=LANES):
    K = w.shape[0]
    return _pad_lanes(w.reshape(K, nheads, hd), lo, width).reshape(K, nheads * width)


def _widen_w_in(w):
    cuts = np.cumsum([0, MLA_Q_RANK, MLA_KV_RANK, MLA_ROPE, C_DIFF, C_DIFF, C_DIFF, C_MOBA, C_MOBA, C_MOBA])
    cq, ckv, kr, dq, dk, dv, mq, mk, mv = [w[:, int(a):int(b)] for a, b in zip(cuts[:-1], cuts[1:])]
    two_maps = lambda x: _head_groups(x, 2 * DIFF_HEADS, DIFF_HD, 0, HALF)
    out = jnp.concatenate([
        cq, ckv, _pad_lanes(kr, MLA_NOPE), _pad_lanes(_rot_pairs(kr), MLA_NOPE),
        dq, two_maps(dk), dv, mq, _head_groups(mk, MOBA_HEADS, MOBA_HD), mv], axis=1).astype(BF16)
    assert out.shape[1] == C_WIDE
    return out


def _widen_w_uq(w):
    K = w.shape[0]
    e = w.reshape(K, MLA_HEADS, MLA_NOPE + MLA_ROPE)
    nope = e[..., :MLA_NOPE].reshape(K, MLA_HEADS * MLA_NOPE)
    rope = _pad_lanes(e[..., MLA_NOPE:].reshape(K, MLA_HEADS * MLA_ROPE), 0, 2 * LANES)
    rot = _pad_lanes(_rot_pairs(e[..., MLA_NOPE:]).reshape(K, MLA_HEADS * MLA_ROPE), 0, 2 * LANES)
    out = jnp.concatenate([nope, rope, rot], axis=1).astype(BF16)
    assert out.shape[1] == UQ_COLS
    return out


def _widen_w_ukv(w):
    K = w.shape[0]
    e = w.reshape(K, MLA_HEADS, MLA_NOPE + MLA_V)
    kn = _pad_lanes(e[..., :MLA_NOPE], 0).reshape(K, MLA_HEADS * LANES)
    vv = e[..., MLA_NOPE:].reshape(K, MLA_HEADS * MLA_V)
    out = jnp.concatenate([kn, vv], axis=1).astype(BF16)
    assert out.shape[1] == UKV_COLS
    return out


def _diff_first_pairs(qnorm, knorm, rel, t, tk, group):
    B, nq = qnorm.shape[:2]
    nm = 2 * DIFF_HEADS
    qn = qnorm[:, :, 0, :nm]
    kn = jnp.max(knorm[:, :, 0, :nm], axis=1)
    spread = NORM_MARGIN * 2.0 * qn * kn[:, None, :]
    relf = rel[..., 0].astype(F32)
    dist = relf[:, ::t][:, :, None] - relf[:, tk - 1::tk][:, None, :]
    c = jnp.asarray([_alibi_slope(i // 2) * LOG2E for i in range(nm)], F32)
    dead = c[None, None, :, None] * dist[:, :, None, :] > spread[..., None] + UNDERFLOW_BITS
    tiles = jnp.sum(dead.astype(jnp.int32), axis=-1)
    pairs = jnp.min(tiles.reshape(B, nq, DIFF_HEADS // group, 2 * group), axis=-1) // 2
    return pairs.transpose(0, 2, 1)


def kernel(x, mem, positions, attn_norm, w_in, mla_q_norm, mla_w_uq, mla_kv_norm, mla_w_ukv, diff_lambda_q1, diff_lambda_k1, diff_lambda_q2, diff_lambda_k2, diff_sub_norm, w_out, cross_norm, mem_norm, cross_wq, cross_wkv, cross_wo, mlp_norm, mlp_w1, mlp_w2, final_norm):
    B, S, _ = x.shape
    depth = w_in.shape[0]
    tm = 512
    t = 512
    tk = t // 2
    assert S % tm == 0 and tm == t and tk % MOBA_BLOCK == 0 and S // MOBA_BLOCK <= LANES - MOBA_SEL

    pos = positions.astype(jnp.int32)[:, None, :]
    rel = (positions - positions[:, :1]).astype(jnp.int32)[..., None]
    half = MLA_ROPE // 2
    inv_lane = (ROPE_THETA ** (-jnp.arange(half, dtype=F32) / half))[:, None]
    rows = lambda v: v.astype(F32)[:, None, :]
    wout_b, wq_b, wkv_b, wo_b, w1_b, w2_b = (w.astype(BF16) for w in (w_out, cross_wq, cross_wkv, cross_wo,
                                                                       mlp_w1, mlp_w2))
    w_wide, wuq, wukv = jax.vmap(_widen_w_in)(w_in), jax.vmap(_widen_w_uq)(mla_w_uq), jax.vmap(_widen_w_ukv)(mla_w_ukv)
    g_attn, g_q, g_kv, g_cross, g_mem, g_mlp = (rows(v) for v in (attn_norm, mla_q_norm, mla_kv_norm, cross_norm,
                                                                  mem_norm, mlp_norm))

    h = x
    for l in range(depth):
        outs = _inproj(h, pos, rel, inv_lane, g_attn, w_wide, g_q, wuq, g_kv, wukv, l=l, tm=tm, t=tk)
        qa, ka, va, qd, kd, vd, mq, mk, mv, kmean, qnorm, knorm = outs
        km = kmean.reshape(B, S // MOBA_BLOCK, MOBA_HEADS, LANES)[..., :MOBA_HD].transpose(0, 2, 1, 3)
        km_pad = jnp.pad(km, ((0, 0), (0, 0), (0, LANES - MOBA_SEL - S // MOBA_BLOCK), (0, 0)))
        mq_aug = _moba_gate(mq, km_pad, tm=tm)

        attn = functools.partial(_group_attention, t=t, tk=tk)
        o_a = attn(_mla_attn_kernel, qa, ka, va, group=3, chains=3, name="mla_attn")
        lam_rows = jnp.stack([diff_lambda_q1[l], diff_lambda_k1[l], diff_lambda_q2[l], diff_lambda_k2[l]])
        lam_rows = jnp.pad(lam_rows.astype(F32), ((0, 4), (0, LANES - DIFF_HD)))
        gain = jnp.tile(diff_sub_norm[l].astype(F32), DIFF_GROUP)[:, None]
        lam_init = 0.8 - 0.6 * math.exp(-0.3 * l)
        o_b = attn(functools.partial(_diff_attn_kernel, lam_init=lam_init), qd, kd, vd,
                   extra=(lam_rows, gain), first_pair=_diff_first_pairs(qnorm, knorm, rel, t, tk, DIFF_GROUP),
                   group=DIFF_GROUP, chains=2 * DIFF_GROUP, name="diff_attn")
        o_c = attn(_mla_attn_kernel, mq_aug, mk, mv, group=4, chains=4, name="moba_attn")

        memkv = _memkv(mem, g_mem, wkv_b, l=l)
        h = _post(h, o_a, o_b, o_c, wout_b, g_cross, wq_b, memkv, wo_b, g_mlp, w1_b, w2_b,
                  final_norm.astype(F32)[None, :], l=l, tm=tm, final=(l == depth - 1))
    return h
```

```python
import functools
import math

import jax
import jax.numpy as jnp
import numpy as np
from jax import lax
from jax.experimental import pallas as pl
from jax.experimental.pallas import tpu as pltpu

D_MODEL = 1024
MLA_HEADS = 6
MLA_NOPE = 64
MLA_ROPE = 32
MLA_V = 64
MLA_Q_RANK = 256
MLA_KV_RANK = 128
ROPE_THETA = 10000.0
DIFF_HEADS = 6
DIFF_HD = 32
MOBA_HEADS = 4
MOBA_HD = 64
MOBA_BLOCK = 256
MOBA_TOPK = 3
CROSS_HEADS = 4
CROSS_HD = D_MODEL // CROSS_HEADS
D_FF = 4 * D_MODEL
EPS = 1e-6
NEG = -1e30
N_ALIBI = DIFF_HEADS + MOBA_HEADS
C_DIFF = DIFF_HEADS * 2 * DIFF_HD
C_MOBA = MOBA_HEADS * MOBA_HD

LANES = 128
HALF = LANES // 2
LOG2E = 1.4426950408889634
VMEM_LIMIT = 56 * 1024 * 1024

F32 = jnp.float32
BF16 = jnp.bfloat16

V_DIM = 64
V_ONE = V_DIM
V_ROWS = 80
DIFF_MAP2 = HALF
DIFF_BIAS = DIFF_HD
MOBA_BIAS = MOBA_HD
MOBA_SEL = 96
N_SPLIT = 3
UNROLL_CHAIN_PAIRS = 12
GATE_TILE = 2048
MAX_UNROLL = 4
DIFF_GROUP = 2
UNDERFLOW_BITS = 152.0
NORM_MARGIN = 1.02

O_CQ = 0
O_CKV = O_CQ + MLA_Q_RANK
O_KR = O_CKV + MLA_KV_RANK
O_KRS = O_KR + LANES
O_DQ = O_KRS + LANES
O_DK = O_DQ + C_DIFF
O_DV = O_DK + DIFF_HEADS * LANES
O_MQ = O_DV + C_DIFF
O_MK = O_MQ + C_MOBA
O_MV = O_MK + MOBA_HEADS * LANES
C_WIDE = O_MV + C_MOBA
UQ_NOPE = 0
UQ_ROPE = MLA_HEADS * MLA_NOPE
UQ_ROT = UQ_ROPE + 2 * LANES
UQ_COLS = UQ_ROT + 2 * LANES
UKV_V = MLA_HEADS * LANES
UKV_COLS = UKV_V + MLA_HEADS * MLA_V


def _alibi_slope(h):
    return 2.0 ** (-8.0 * (h + 1) / N_ALIBI)


def _rms(x, g):
    return x * lax.rsqrt(jnp.mean(x * x, axis=-1, keepdims=True) + EPS) * g


def _dot(a, b):
    return jnp.dot(a, b, preferred_element_type=F32)


def _dot_nt(a, b):
    return lax.dot_general(a, b, (((1,), (1,)), ((), ())), preferred_element_type=F32)


def _dot_tn(a, b):
    return lax.dot_general(a, b, (((0,), (0,)), ((), ())), preferred_element_type=F32)


def _lane(shape):
    return lax.broadcasted_iota(jnp.int32, shape, len(shape) - 1)


def _onehot_lanes(lanes, value=1.0):
    l = _lane((1, LANES))
    out = jnp.zeros((1, LANES), F32)
    for i in lanes:
        out = jnp.where(l == i, value, out)
    return out


def _split3(x):
    hi = x.astype(BF16).astype(F32)
    r = x - hi
    mid = r.astype(BF16).astype(F32)
    lo = (r - mid).astype(BF16).astype(F32)
    return hi, mid, lo


def _store_vt(ref, h, vt, t):
    tm = vt.shape[1]
    tail = jnp.where(lax.broadcasted_iota(jnp.int32, (V_ROWS - V_DIM, tm), 0) == 0, 1.0, 0.0)
    x = jnp.concatenate([vt, tail], axis=0).astype(BF16)
    for j in range(tm // t):
        ref[0, h, j] = x[:, j * t:(j + 1) * t]


def _bias_rows(n, tm):
    return jnp.where(lax.broadcasted_iota(jnp.int32, (n, tm), 0) < N_SPLIT, 1.0, 0.0)


def _inproj_kernel(h_ref, pos_ref, rel_ref, inv_ref, g_ref, w_ref, qn_ref, wuq_ref, kvn_ref, wukv_ref,
                   qa_ref, ka_ref, va_ref, qd_ref, kd_ref, vd_ref, mq_ref, mk_ref, mv_ref, kmean_ref,
                   qnorm_ref, knorm_ref, *, tm, t):
    si = pl.program_id(1)
    nb = _rms(h_ref[0], g_ref[...]).astype(BF16)
    group = lambda x, o, i: x[:, o + i * LANES:o + (i + 1) * LANES]

    pm = _dot(nb, w_ref[:, O_CQ:O_DQ])
    cq = _rms(pm[:, O_CQ:O_CKV], qn_ref[...]).astype(BF16)
    ckv = _rms(pm[:, O_CKV:O_KR], kvn_ref[...]).astype(BF16)
    q2 = _dot(cq, wuq_ref[...])
    kv2 = _dot(ckv, wukv_ref[...])
    ang = inv_ref[...] * pos_ref[0].astype(F32)
    reps = LANES // ang.shape[0]
    cs = jnp.concatenate([jnp.cos(ang)] * reps, axis=0).T
    sn = jnp.concatenate([jnp.sin(ang)] * reps, axis=0).T
    krope = pm[:, O_KR:O_KRS] * cs + pm[:, O_KRS:O_DQ] * sn
    qscale = (MLA_NOPE + MLA_ROPE) ** -0.5 * LOG2E
    rope_t = [(group(q2, UQ_ROPE, i) * cs + group(q2, UQ_ROT, i) * sn).T for i in range(2)]
    pad_rows = jnp.zeros((LANES - MLA_NOPE - MLA_ROPE, tm), F32)
    for p in range(MLA_HEADS // 2):
        nope_t = group(q2, UQ_NOPE, p).T
        v_t = group(kv2, UKV_V, p).T
        for e in range(2):
            h = 2 * p + e
            r = (h % 4) * MLA_ROPE
            qa = jnp.concatenate([nope_t[e * HALF:(e + 1) * HALF], rope_t[h // 4][r:r + MLA_ROPE], pad_rows], axis=0)
            qa_ref[0, h] = (qa * qscale).astype(BF16)
            _store_vt(va_ref, h, v_t[e * HALF:(e + 1) * HALF], t)
    for h in range(MLA_HEADS):
        ka_ref[0, h] = (group(kv2, 0, h) + krope).astype(BF16)

    relf = rel_ref[0].astype(F32)

    pq = _dot(nb, w_ref[:, O_DQ:O_DK])
    pk = _dot(nb, w_ref[:, O_DK:O_DV])
    pv = _dot(nb, w_ref[:, O_DV:O_MQ])
    dscale = DIFF_HD ** -0.5 * LOG2E
    ones_rows = _bias_rows(DIFF_MAP2 - DIFF_HD, tm)
    lane1 = _lane((1, LANES))
    qn_row = jnp.zeros((1, LANES), F32)
    kn_row = jnp.zeros((1, LANES), F32)
    for p in range(DIFF_HEADS // 2):
        q_t = group(pq, 0, p).T * dscale
        v_t = group(pv, 0, p).T
        for e in range(2):
            o = e * HALF
            qd_ref[0, 2 * p + e] = jnp.concatenate(
                [q_t[o:o + DIFF_HD], ones_rows, q_t[o + DIFF_HD:o + HALF], ones_rows], axis=0).astype(BF16)
            _store_vt(vd_ref, 2 * p + e, v_t[o:o + HALF], t)
            for mp in range(2):
                qm = q_t[o + mp * DIFF_HD:o + (mp + 1) * DIFF_HD]
                qsq = jnp.max(jnp.sum(qm * qm, axis=0, keepdims=True), axis=1, keepdims=True)
                qn_row = jnp.where(lane1 == 2 * (2 * p + e) + mp, jnp.sqrt(qsq), qn_row)
    in_map1 = _lane((tm, LANES)) < DIFF_MAP2
    for h in range(DIFF_HEADS):
        kb = group(pk, 0, h)
        for mp in range(2):
            mine = in_map1 if mp == 0 else jnp.logical_not(in_map1)
            ksq = jnp.sum(jnp.where(mine, kb * kb, 0.0), axis=1, keepdims=True)
            kn_row = jnp.where(lane1 == 2 * h + mp, jnp.sqrt(jnp.max(ksq, axis=0, keepdims=True)), kn_row)
        for i, piece in enumerate(_split3(relf * (_alibi_slope(h) * LOG2E))):
            kb = kb + piece * _onehot_lanes([DIFF_BIAS + i, DIFF_MAP2 + DIFF_BIAS + i])
        kd_ref[0, h] = kb.astype(BF16)
    qnorm_ref[0, 0] = qn_row
    knorm_ref[0, 0] = kn_row

    pq = _dot(nb, w_ref[:, O_MQ:O_MK])
    pk = _dot(nb, w_ref[:, O_MK:O_MV])
    pv = _dot(nb, w_ref[:, O_MV:C_WIDE])
    for p in range(MOBA_HEADS // 2):
        q_t = group(pq, 0, p).T
        v_t = group(pv, 0, p).T
        for e in range(2):
            mq_ref[0, 2 * p + e] = q_t[e * HALF:(e + 1) * HALF]
            _store_vt(mv_ref, 2 * p + e, v_t[e * HALF:(e + 1) * HALF], t)
    row = lax.broadcasted_iota(jnp.int32, (tm, 1), 0)
    blk = (si * tm + row) // MOBA_BLOCK
    blk_onehot = jnp.where(_lane((tm, LANES)) == MOBA_SEL + blk, 1.0, 0.0)
    for h in range(MOBA_HEADS):
        kb = group(pk, 0, h) + blk_onehot
        for i, piece in enumerate(_split3(relf * (_alibi_slope(DIFF_HEADS + h) * LOG2E))):
            kb = kb + piece * _onehot_lanes([MOBA_BIAS + i])
        mk_ref[0, h] = kb.astype(BF16)
    for j in range(tm // MOBA_BLOCK):
        kmean_ref[0, j] = jnp.mean(pk[j * MOBA_BLOCK:(j + 1) * MOBA_BLOCK, :], axis=0, keepdims=True)


def _layer_spec(a, l, single_buffer=False):
    zeros = (0,) * (a.ndim - 1)
    mode = dict(pipeline_mode=pl.Buffered(1)) if single_buffer else {}
    return pl.BlockSpec((None,) + a.shape[1:], lambda *_: (l,) + zeros, **mode)


def _inproj(h, pos, rel, inv_lane, g, w_wide, qn, wuq, kvn, wukv, *, l, tm, t):
    B, S, _ = h.shape
    grid = (B, S // tm)
    tok = lambda b, i: (b, i, 0)
    const2 = lambda b, i: (0, 0)
    q_out = lambda nh: (jax.ShapeDtypeStruct((B, nh, LANES, S), BF16),
                        pl.BlockSpec((1, nh, LANES, tm), lambda b, i: (b, 0, 0, i)))
    k_out = lambda nh, dt: (jax.ShapeDtypeStruct((B, nh, S, LANES), dt),
                            pl.BlockSpec((1, nh, tm, LANES), lambda b, i: (b, 0, i, 0)))
    v_out = lambda nh: (jax.ShapeDtypeStruct((B, nh, S // t, V_ROWS, t), BF16),
                        pl.BlockSpec((1, nh, tm // t, V_ROWS, t), lambda b, i: (b, 0, i, 0, 0)))
    outs = [q_out(MLA_HEADS), k_out(MLA_HEADS, BF16), v_out(MLA_HEADS),
            q_out(DIFF_HEADS), k_out(DIFF_HEADS, BF16), v_out(DIFF_HEADS),
            (jax.ShapeDtypeStruct((B, MOBA_HEADS, MOBA_HD, S), F32),
             pl.BlockSpec((1, MOBA_HEADS, MOBA_HD, tm), lambda b, i: (b, 0, 0, i))),
            k_out(MOBA_HEADS, BF16), v_out(MOBA_HEADS),
            (jax.ShapeDtypeStruct((B, S // MOBA_BLOCK, 1, MOBA_HEADS * LANES), F32),
             pl.BlockSpec((1, tm // MOBA_BLOCK, 1, MOBA_HEADS * LANES), lambda b, i: (b, i, 0, 0)))]
    outs += [(jax.ShapeDtypeStruct((B, S // tm, 1, LANES), F32),
              pl.BlockSpec((1, 1, 1, LANES), lambda b, i: (b, i, 0, 0)))] * 2
    full = lambda a: pl.BlockSpec(a.shape, const2)
    return pl.pallas_call(
        functools.partial(_inproj_kernel, tm=tm, t=t),
        grid=grid,
        in_specs=[pl.BlockSpec((1, tm, D_MODEL), tok), pl.BlockSpec((1, 1, tm), lambda b, i: (b, 0, i)),
                  pl.BlockSpec((1, tm, 1), tok), full(inv_lane)]
                 + [_layer_spec(a, l) for a in (g, w_wide, qn, wuq, kvn, wukv)],
        out_specs=[o[1] for o in outs],
        out_shape=[o[0] for o in outs],
        compiler_params=pltpu.CompilerParams(dimension_semantics=("parallel", "parallel"),
                                             vmem_limit_bytes=VMEM_LIMIT),
        name="inproj",
    )(h, pos, rel, inv_lane, g, w_wide, qn, wuq, kvn, wukv)


def _moba_gate_kernel(mq_ref, km_ref, o_ref, *, tm):
    si = pl.program_id(1)
    nblk = LANES - MOBA_SEL
    blk = lax.broadcasted_iota(jnp.int32, (nblk, tm), 0)
    blkf = blk.astype(F32)
    own = (si * tm + _lane((1, tm))) // MOBA_BLOCK
    valid = blk < own
    ones_rows = _bias_rows(MOBA_SEL - MOBA_HD, tm)
    for h in range(MOBA_HEADS):
        qt = mq_ref[0, h]
        gate = jnp.dot(km_ref[0, h], qt, precision=lax.Precision.HIGHEST, preferred_element_type=F32)
        g = jnp.where(valid, gate, NEG)
        sel = blk == own
        for _ in range(MOBA_TOPK):
            mx = jnp.max(g, axis=0, keepdims=True)
            idx = jnp.min(jnp.where(g == mx, blkf, float(nblk)), axis=0, keepdims=True)
            pick = (blkf == idx) & (mx > 0.5 * NEG)
            sel = sel | pick
            g = jnp.where(pick, NEG, g)
        selbias = jnp.where(sel, 0.0, NEG)
        o_ref[0, h] = jnp.concatenate([qt * (MOBA_HD ** -0.5 * LOG2E), ones_rows, selbias],
                                      axis=0).astype(BF16)


def _moba_gate(mq, km_pad, *, tm):
    B, H, _, S = mq.shape
    return pl.pallas_call(
        functools.partial(_moba_gate_kernel, tm=tm),
        grid=(B, S // tm),
        in_specs=[pl.BlockSpec((1, H, MOBA_HD, tm), lambda b, i: (b, 0, 0, i)),
                  pl.BlockSpec((1, H, LANES - MOBA_SEL, MOBA_HD), lambda b, i: (b, 0, 0, 0))],
        out_specs=pl.BlockSpec((1, H, LANES, tm), lambda b, i: (b, 0, 0, i)),
        out_shape=jax.ShapeDtypeStruct((B, H, LANES, S), BF16),
        compiler_params=pltpu.CompilerParams(dimension_semantics=("parallel", "parallel"),
                                             vmem_limit_bytes=VMEM_LIMIT),
        name="moba_gate",
    )(mq, km_pad)


def _causal_flash(chains, next_chains, k_ref, vt_ref, scratch, qi, j0, j0_next, t, tk):
    assert t == 2 * tk
    s_a, s_b, mb_a, mb_b, m_scr, acc_scr = scratch
    nc = len(chains)

    every, first, second = slice(0, t), slice(0, tk), slice(tk, t)

    def scores(kt, s_dst, mb_dst, qs=every, chains=chains):
        tiles = {}
        for c, (hh, qt) in enumerate(chains):
            if hh not in tiles:
                tiles[hh] = k_ref[0, hh, pl.ds(pl.multiple_of(kt * tk, tk), tk), :]
            s = _dot(tiles[hh], qt[:, qs])
            s_dst[c, :, qs] = s
            mb_dst[c, :, qs] = jnp.max(s, axis=0, keepdims=True)

    def softmax_pv(kt, s_src, mb_src, qs=every, diagonal=False):
        for c, (hh, _) in enumerate(chains):
            s = s_src[c, :, qs]
            if diagonal:
                keep = (lax.broadcasted_iota(jnp.int32, (tk, tk), 0)
                        <= lax.broadcasted_iota(jnp.int32, (tk, tk), 1))
                s = jnp.where(keep, s, NEG)
                mb = jnp.max(s, axis=0, keepdims=True)
            else:
                mb = mb_src[c, :, qs]
            m = m_scr[c, :, qs]
            m_new = jnp.maximum(m, mb)
            p = jnp.exp2(s - m_new).astype(BF16)
            acc_scr[c, :, qs] = acc_scr[c, :, qs] * jnp.exp2(m - m_new) + _dot(vt_ref[0, hh, kt], p)
            m_scr[c, :, qs] = m_new

    for c in range(nc):
        m_scr[c] = jnp.full((1, t), NEG, F32)
        acc_scr[c] = jnp.zeros((V_ROWS, t), F32)

    @pl.when(qi == 0)
    def _():
        scores(2 * j0, s_a, mb_a)

    def pair(j, carry):
        scores(2 * j + 1, s_b, mb_b)
        softmax_pv(2 * j, s_a, mb_a)
        scores(2 * j + 2, s_a, mb_a)
        softmax_pv(2 * j + 1, s_b, mb_b)
        return carry

    unroll = min(MAX_UNROLL, UNROLL_CHAIN_PAIRS // nc)

    def pairs(i, carry):
        for u in range(unroll):
            pair(j0 + unroll * i + u, carry)
        return carry

    n_main = (qi - j0) // unroll
    lax.fori_loop(0, n_main, pairs, 0)
    lax.fori_loop(j0 + n_main * unroll, qi, pair, 0)
    scores(2 * qi + 1, s_b, mb_b, second)
    softmax_pv(2 * qi, s_a, mb_a, first, diagonal=True)
    softmax_pv(2 * qi, s_a, mb_a, second)
    scores(2 * j0_next, s_a, mb_a, chains=next_chains)
    softmax_pv(2 * qi + 1, s_b, mb_b, second, diagonal=True)
    outs = []
    for c in range(nc):
        acc = acc_scr[c]
        outs.append(acc[:V_DIM] * (1.0 / acc[V_ONE:V_ONE + 1]))
    return outs


def _flash_scratch(nc, t, tk):
    return ([pltpu.VMEM((nc, tk, t), F32)] * 2 + [pltpu.VMEM((nc, 1, t), F32)] * 3
            + [pltpu.VMEM((nc, V_ROWS, t), F32)])


def _first_pairs(j0_ref):
    b, p, i = pl.program_id(0), pl.program_id(1), pl.program_id(2)
    step = (b * pl.num_programs(1) + p) * pl.num_programs(2) + i
    last = pl.num_programs(0) * pl.num_programs(1) * pl.num_programs(2) - 1
    return j0_ref[step], j0_ref[jnp.minimum(step + 1, last)]


def _mla_attn_kernel(j0_ref, q_ref, qnext_ref, k_ref, vt_ref, o_ref, *scratch, t, tk):
    heads = range(q_ref.shape[1])
    outs = _causal_flash([(hh, q_ref[0, hh]) for hh in heads], [(hh, qnext_ref[0, hh]) for hh in heads],
                         k_ref, vt_ref, scratch, pl.program_id(2), *_first_pairs(j0_ref), t, tk)
    o_ref[0] = jnp.concatenate(outs, axis=0).astype(o_ref.dtype)


def _diff_attn_kernel(j0_ref, lam_ref, gain_ref, q_ref, qnext_ref, k_ref, vt_ref, o_ref, *scratch, t, tk, lam_init):
    lv = lam_ref[...]
    lam = (jnp.exp(jnp.sum(lv[0:1] * lv[1:2], axis=-1, keepdims=True))
           - jnp.exp(jnp.sum(lv[2:3] * lv[3:4], axis=-1, keepdims=True)) + lam_init)
    feat = lax.broadcasted_iota(jnp.int32, (LANES, t), 0)

    def map_chains(ref):
        chains = []
        for hh in range(ref.shape[1]):
            q = ref[0, hh]
            chains.append((hh, jnp.where(feat < DIFF_MAP2, q, jnp.zeros_like(q))))
            chains.append((hh, jnp.where(feat >= DIFF_MAP2, q, jnp.zeros_like(q))))
        return chains

    res = _causal_flash(map_chains(q_ref), map_chains(qnext_ref), k_ref, vt_ref, scratch,
                        pl.program_id(2), *_first_pairs(j0_ref), t, tk)
    outs = []
    for a, b in zip(res[0::2], res[1::2]):
        o = a - lam * b
        outs.append(o * lax.rsqrt(jnp.mean(o * o, axis=0, keepdims=True) + EPS))
    o_ref[0] = (jnp.concatenate(outs, axis=0) * gain_ref[...] * (1.0 - lam_init)).astype(o_ref.dtype)


def _group_attention(kernel_fn, q, k, vt, extra=(), first_pair=None, *, t, tk, group, chains, name):
    B, H, _, S = q.shape
    if first_pair is None:
        first_pair = jnp.zeros((B, H // group, S // t), jnp.int32)
    nq = S // t
    qspec = pl.BlockSpec((1, group, LANES, t), lambda b, p, i, j0: (b, p, 0, i))
    qnext = pl.BlockSpec((1, group, LANES, t), lambda b, p, i, j0: (b, p, 0, jnp.minimum(i + 1, nq - 1)))
    kspec = pl.BlockSpec((1, group, S, LANES), lambda b, p, i, j0: (b, p, 0, 0))
    vspec = pl.BlockSpec((1, group, S // tk, V_ROWS, tk), lambda b, p, i, j0: (b, p, 0, 0, 0))
    xspecs = [pl.BlockSpec(a.shape, lambda b, p, i, j0: (0, 0)) for a in extra]
    return pl.pallas_call(
        functools.partial(kernel_fn, t=t, tk=tk),
        grid_spec=pltpu.PrefetchScalarGridSpec(
            num_scalar_prefetch=1,
            grid=(B, H // group, S // t),
            in_specs=xspecs + [qspec, qnext, kspec, vspec],
            out_specs=pl.BlockSpec((1, group * V_DIM, t), lambda b, p, i, j0: (b, p, i)),
            scratch_shapes=_flash_scratch(chains, t, tk)),
        out_shape=jax.ShapeDtypeStruct((B, H * V_DIM, S), BF16),
        compiler_params=pltpu.CompilerParams(dimension_semantics=("parallel", "parallel", "arbitrary"),
                                             vmem_limit_bytes=VMEM_LIMIT),
        name=name,
    )(first_pair.reshape(-1), *extra, q, q, k, vt)


def _memkv_kernel(x_ref, g_ref, w_ref, o_ref):
    o_ref[0] = _dot(_rms(x_ref[0], g_ref[...]).astype(BF16), w_ref[...]).astype(o_ref.dtype)


def _memkv(mem, g, wkv, *, l):
    B, M, _ = mem.shape
    return pl.pallas_call(
        _memkv_kernel,
        grid=(B,),
        in_specs=[pl.BlockSpec((1, M, D_MODEL), lambda b: (b, 0, 0)), _layer_spec(g, l), _layer_spec(wkv, l)],
        out_specs=pl.BlockSpec((1, M, 2 * D_MODEL), lambda b: (b, 0, 0)),
        out_shape=jax.ShapeDtypeStruct((B, M, 2 * D_MODEL), BF16),
        compiler_params=pltpu.CompilerParams(dimension_semantics=("parallel",), vmem_limit_bytes=VMEM_LIMIT),
        name="memkv",
    )(mem, g, wkv)


def _post_kernel(h_ref, oa_ref, ob_ref, oc_ref, wout_ref, g_ref, wq_ref, kv_ref, wo_ref,
                 g2_ref, w1_ref, w2_ref, gf_ref, o_ref, *, final, chunk):
    na, nb = oa_ref.shape[1], ob_ref.shape[1]
    h1 = (h_ref[0] + _dot_tn(oa_ref[0], wout_ref[0:na]) + _dot_tn(ob_ref[0], wout_ref[na:na + nb])
          + _dot_tn(oc_ref[0], wout_ref[na + nb:]))
    n = _rms(h1, g_ref[...]).astype(BF16)
    q = (_dot(n, wq_ref[...]) * (CROSS_HD ** -0.5 * LOG2E)).astype(BF16)
    ctx = []
    for h in range(CROSS_HEADS):
        sl = slice(h * CROSS_HD, (h + 1) * CROSS_HD)
        s = _dot_nt(q[:, sl], kv_ref[0, :, sl])
        p = jnp.exp2(s - jnp.max(s, axis=-1, keepdims=True))
        l = jnp.sum(p, axis=-1, keepdims=True)
        c = _dot(p.astype(BF16), kv_ref[0, :, D_MODEL + h * CROSS_HD:D_MODEL + (h + 1) * CROSS_HD])
        ctx.append((c * (1.0 / l)).astype(BF16))
    h2 = h1 + _dot(jnp.concatenate(ctx, axis=-1), wo_ref[...])
    n = _rms(h2, g2_ref[...]).astype(BF16)
    acc = h2
    for c in range(D_FF // chunk):
        a = jnp.maximum(_dot(n, w1_ref[:, c * chunk:(c + 1) * chunk]), 0.0)
        acc = acc + _dot((a * a).astype(BF16), w2_ref[c * chunk:(c + 1) * chunk, :])
    o_ref[0] = _rms(acc, gf_ref[...]) if final else acc


def _post(h, oa, ob, oc, wout, g, wq, memkv, wo, g2, w1, w2, gf, *, l, tm, final):
    B, S, _ = h.shape
    tok = lambda b, i: (b, i, 0)
    const2 = lambda b, i: (0, 0)
    small = lambda a: pl.BlockSpec(a.shape, const2)
    resident = lambda a: _layer_spec(a, l, single_buffer=True)
    return pl.pallas_call(
        functools.partial(_post_kernel, final=final, chunk=D_MODEL),
        grid=(B, S // tm),
        in_specs=[pl.BlockSpec((1, tm, D_MODEL), tok)]
                 + [pl.BlockSpec((1, o.shape[1], tm), lambda b, i: (b, 0, i)) for o in (oa, ob, oc)] + [
                  resident(wout), _layer_spec(g, l), resident(wq),
                  pl.BlockSpec((1,) + memkv.shape[1:], lambda b, i: (b, 0, 0)), resident(wo),
                  _layer_spec(g2, l), resident(w1), resident(w2), small(gf)],
        out_specs=pl.BlockSpec((1, tm, D_MODEL), tok),
        out_shape=jax.ShapeDtypeStruct(h.shape, F32),
        compiler_params=pltpu.CompilerParams(dimension_semantics=("parallel", "parallel"),
                                             vmem_limit_bytes=VMEM_LIMIT),
        name="post",
    )(h, oa, ob, oc, wout, g, wq, memkv, wo, g2, w1, w2, gf)


def _rot_pairs(w):
    half = w.shape[-1] // 2
    return jnp.concatenate([-w[..., half:], w[..., :half]], axis=-1)


def _pad_lanes(w, lo, width=LANES):
    pad = [(0, 0)] * (w.ndim - 1) + [(lo, width - lo - w.shape[-1])]
    return jnp.pad(w, pad)


def _head_groups(w, nheads, hd, lo=0, width=LANES):
    K = w.shape[0]
    return _pad_lanes(w.reshape(K, nheads, hd), lo, width).reshape(K, nheads * width)


def _widen_w_in(w):
    cuts = np.cumsum([0, MLA_Q_RANK, MLA_KV_RANK, MLA_ROPE, C_DIFF, C_DIFF, C_DIFF, C_MOBA, C_MOBA, C_MOBA])
    cq, ckv, kr, dq, dk, dv, mq, mk, mv = [w[:, int(a):int(b)] for a, b in zip(cuts[:-1], cuts[1:])]
    two_maps = lambda x: _head_groups(x, 2 * DIFF_HEADS, DIFF_HD, 0, HALF)
    out = jnp.concatenate([
        cq, ckv, _pad_lanes(kr, MLA_NOPE), _pad_lanes(_rot_pairs(kr), MLA_NOPE),
        dq, two_maps(dk), dv, mq, _head_groups(mk, MOBA_HEADS, MOBA_HD), mv], axis=1).astype(BF16)
    assert out.shape[1] == C_WIDE
    return out


def _widen_w_uq(w):
    K = w.shape[0]
    e = w.reshape(K, MLA_HEADS, MLA_NOPE + MLA_ROPE)
    nope = e[..., :MLA_NOPE].reshape(K, MLA_HEADS * MLA_NOPE)
    rope = _pad_lanes(e[..., MLA_NOPE:].reshape(K, MLA_HEADS * MLA_ROPE), 0, 2 * LANES)
    rot = _pad_lanes(_rot_pairs(e[..., MLA_NOPE:]).reshape(K, MLA_HEADS * MLA_ROPE), 0, 2 * LANES)
    out = jnp.concatenate([nope, rope, rot], axis=1).astype(BF16)
    assert out.shape[1] == UQ_COLS
    return out


def _widen_w_ukv(w):
    K = w.shape[0]
    e = w.reshape(K, MLA_HEADS, MLA_NOPE + MLA_V)
    kn = _pad_lanes(e[..., :MLA_NOPE], 0).reshape(K, MLA_HEADS * LANES)
    vv = e[..., MLA_NOPE:].reshape(K, MLA_HEADS * MLA_V)
    out = jnp.concatenate([kn, vv], axis=1).astype(BF16)
    assert out.shape[1] == UKV_COLS
    return out


def _diff_first_pairs(qnorm, knorm, rel, t, tk, group):
    B, nq = qnorm.shape[:2]
    nm = 2 * DIFF_HEADS
    qn = qnorm[:, :, 0, :nm]
    kn = jnp.max(knorm[:, :, 0, :nm], axis=1)
    spread = NORM_MARGIN * 2.0 * qn * kn[:, None, :]
    relf = rel[..., 0].astype(F32)
    dist = relf[:, ::t][:, :, None] - relf[:, tk - 1::tk][:, None, :]
    c = jnp.asarray([_alibi_slope(i // 2) * LOG2E for i in range(nm)], F32)
    dead = c[None, None, :, None] * dist[:, :, None, :] > spread[..., None] + UNDERFLOW_BITS
    tiles = jnp.sum(dead.astype(jnp.int32), axis=-1)
    pairs = jnp.min(tiles.reshape(B, nq, DIFF_HEADS // group, 2 * group), axis=-1) // 2
    return pairs.transpose(0, 2, 1)


def kernel(x, mem, positions, attn_norm, w_in, mla_q_norm, mla_w_uq, mla_kv_norm, mla_w_ukv, diff_lambda_q1, diff_lambda_k1, diff_lambda_q2, diff_lambda_k2, diff_sub_norm, w_out, cross_norm, mem_norm, cross_wq, cross_wkv, cross_wo, mlp_norm, mlp_w1, mlp_w2, final_norm):
    B, S, _ = x.shape
    depth = w_in.shape[0]
    tm = 512
    t = 512
    tk = t // 2
    assert S % tm == 0 and tm == t and tk % MOBA_BLOCK == 0 and S // MOBA_BLOCK <= LANES - MOBA_SEL

    pos = positions.astype(jnp.int32)[:, None, :]
    rel = (positions - positions[:, :1]).astype(jnp.int32)[..., None]
    half = MLA_ROPE // 2
    inv_lane = (ROPE_THETA ** (-jnp.arange(half, dtype=F32) / half))[:, None]
    rows = lambda v: v.astype(F32)[:, None, :]
    wout_b, wq_b, wkv_b, wo_b, w1_b, w2_b = (w.astype(BF16) for w in (w_out, cross_wq, cross_wkv, cross_wo,
                                                                       mlp_w1, mlp_w2))
    w_wide, wuq, wukv = jax.vmap(_widen_w_in)(w_in), jax.vmap(_widen_w_uq)(mla_w_uq), jax.vmap(_widen_w_ukv)(mla_w_ukv)
    g_attn, g_q, g_kv, g_cross, g_mem, g_mlp = (rows(v) for v in (attn_norm, mla_q_norm, mla_kv_norm, cross_norm,
                                                                  mem_norm, mlp_norm))

    h = x
    for l in range(depth):
        outs = _inproj(h, pos, rel, inv_lane, g_attn, w_wide, g_q, wuq, g_kv, wukv, l=l, tm=tm, t=tk)
        qa, ka, va, qd, kd, vd, mq, mk, mv, kmean, qnorm, knorm = outs
        km = kmean.reshape(B, S // MOBA_BLOCK, MOBA_HEADS, LANES)[..., :MOBA_HD].transpose(0, 2, 1, 3)
        km_pad = jnp.pad(km, ((0, 0), (0, 0), (0, LANES - MOBA_SEL - S // MOBA_BLOCK), (0, 0)))
        mq_aug = _moba_gate(mq, km_pad, tm=math.gcd(S, GATE_TILE))

        attn = functools.partial(_group_attention, t=t, tk=tk)
        o_a = attn(_mla_attn_kernel, qa, ka, va, group=3, chains=3, name="mla_attn")
        lam_rows = jnp.stack([diff_lambda_q1[l], diff_lambda_k1[l], diff_lambda_q2[l], diff_lambda_k2[l]])
        lam_rows = jnp.pad(lam_rows.astype(F32), ((0, 4), (0, LANES - DIFF_HD)))
        gain = jnp.tile(diff_sub_norm[l].astype(F32), DIFF_GROUP)[:, None]
        lam_init = 0.8 - 0.6 * math.exp(-0.3 * l)
        o_b = attn(functools.partial(_diff_attn_kernel, lam_init=lam_init), qd, kd, vd,
                   extra=(lam_rows, gain), first_pair=_diff_first_pairs(qnorm, knorm, rel, t, tk, DIFF_GROUP),
                   group=DIFF_GROUP, chains=2 * DIFF_GROUP, name="diff_attn")
        o_c = attn(_mla_attn_kernel, mq_aug, mk, mv, group=4, chains=4, name="moba_attn")

        memkv = _memkv(mem, g_mem, wkv_b, l=l)
        h = _post(h, o_a, o_b, o_c, wout_b, g_cross, wq_b, memkv, wo_b, g_mlp, w1_b, w2_b,
                  final_norm.astype(F32)[None, :], l=l, tm=tm, final=(l == depth - 1))
    return h
```

```python
import functools
import math

import jax
import jax.numpy as jnp
import numpy as np
from jax import lax
from jax.experimental import pallas as pl
from jax.experimental.pallas import tpu as pltpu

D_MODEL = 1024
MLA_HEADS = 6
MLA_NOPE = 64
MLA_ROPE = 32
MLA_V = 64
MLA_Q_RANK = 256
MLA_KV_RANK = 128
ROPE_THETA = 10000.0
DIFF_HEADS = 6
DIFF_HD = 32
MOBA_HEADS = 4
MOBA_HD = 64
MOBA_BLOCK = 256
MOBA_TOPK = 3
CROSS_HEADS = 4
CROSS_HD = D_MODEL // CROSS_HEADS
D_FF = 4 * D_MODEL
EPS = 1e-6
NEG = -1e30
N_ALIBI = DIFF_HEADS + MOBA_HEADS
C_DIFF = DIFF_HEADS * 2 * DIFF_HD
C_MOBA = MOBA_HEADS * MOBA_HD

LANES = 128
HALF = LANES // 2
LOG2E = 1.4426950408889634
VMEM_LIMIT = 56 * 1024 * 1024

F32 = jnp.float32
BF16 = jnp.bfloat16

V_DIM = 64
V_ONE = V_DIM
V_ROWS = 80
DIFF_MAP2 = HALF
DIFF_BIAS = DIFF_HD
MOBA_BIAS = MOBA_HD
MOBA_SEL = 96
N_SPLIT = 3
UNROLL_CHAIN_PAIRS = 12
GATE_TILE = 2048
MAX_UNROLL = 4
DIFF_GROUP = 2
UNDERFLOW_BITS = 152.0
NORM_MARGIN = 1.02

O_CQ = 0
O_CKV = O_CQ + MLA_Q_RANK
O_KR = O_CKV + MLA_KV_RANK
O_KRS = O_KR + LANES
O_DQ = O_KRS + LANES
O_DK = O_DQ + C_DIFF
O_DV = O_DK + DIFF_HEADS * LANES
O_MQ = O_DV + C_DIFF
O_MK = O_MQ + C_MOBA
O_MV = O_MK + MOBA_HEADS * LANES
C_WIDE = O_MV + C_MOBA
UQ_NOPE = 0
UQ_ROPE = MLA_HEADS * MLA_NOPE
UQ_ROT = UQ_ROPE + 2 * LANES
UQ_COLS = UQ_ROT + 2 * LANES
UKV_V = MLA_HEADS * LANES
UKV_COLS = UKV_V + MLA_HEADS * MLA_V


def _alibi_slope(h):
    return 2.0 ** (-8.0 * (h + 1) / N_ALIBI)


def _rms(x, g):
    return x * lax.rsqrt(jnp.mean(x * x, axis=-1, keepdims=True) + EPS) * g


def _dot(a, b):
    return jnp.dot(a, b, preferred_element_type=F32)


def _dot_nt(a, b):
    return lax.dot_general(a, b, (((1,), (1,)), ((), ())), preferred_element_type=F32)


def _dot_tn(a, b):
    return lax.dot_general(a, b, (((0,), (0,)), ((), ())), preferred_element_type=F32)


def _lane(shape):
    return lax.broadcasted_iota(jnp.int32, shape, len(shape) - 1)


def _onehot_lanes(lanes, value=1.0):
    l = _lane((1, LANES))
    out = jnp.zeros((1, LANES), F32)
    for i in lanes:
        out = jnp.where(l == i, value, out)
    return out


def _split3(x):
    hi = x.astype(BF16).astype(F32)
    r = x - hi
    mid = r.astype(BF16).astype(F32)
    lo = (r - mid).astype(BF16).astype(F32)
    return hi, mid, lo


def _store_vt(ref, h, vt, t):
    tm = vt.shape[1]
    tail = jnp.where(lax.broadcasted_iota(jnp.int32, (V_ROWS - V_DIM, tm), 0) == 0, 1.0, 0.0)
    x = jnp.concatenate([vt, tail], axis=0).astype(BF16)
    for j in range(tm // t):
        ref[0, h, j] = x[:, j * t:(j + 1) * t]


def _bias_rows(n, tm):
    return jnp.where(lax.broadcasted_iota(jnp.int32, (n, tm), 0) < N_SPLIT, 1.0, 0.0)


def _inproj_kernel(h_ref, pos_ref, rel_ref, inv_ref, g_ref, w_ref, qn_ref, wuq_ref, kvn_ref, wukv_ref,
                   qa_ref, ka_ref, va_ref, qd_ref, kd_ref, vd_ref, mq_ref, mk_ref, mv_ref, kmean_ref,
                   qnorm_ref, knorm_ref, *, tm, t):
    si = pl.program_id(1)
    nb = _rms(h_ref[0], g_ref[...]).astype(BF16)
    group = lambda x, o, i: x[:, o + i * LANES:o + (i + 1) * LANES]

    pm = _dot(nb, w_ref[:, O_CQ:O_DQ])
    cq = _rms(pm[:, O_CQ:O_CKV], qn_ref[...]).astype(BF16)
    ckv = _rms(pm[:, O_CKV:O_KR], kvn_ref[...]).astype(BF16)
    q2 = _dot(cq, wuq_ref[...])
    kv2 = _dot(ckv, wukv_ref[...])
    ang = inv_ref[...] * pos_ref[0].astype(F32)
    reps = LANES // ang.shape[0]
    cs = jnp.concatenate([jnp.cos(ang)] * reps, axis=0).T
    sn = jnp.concatenate([jnp.sin(ang)] * reps, axis=0).T
    krope = pm[:, O_KR:O_KRS] * cs + pm[:, O_KRS:O_DQ] * sn
    qscale = (MLA_NOPE + MLA_ROPE) ** -0.5 * LOG2E
    rope_t = [(group(q2, UQ_ROPE, i) * cs + group(q2, UQ_ROT, i) * sn).T for i in range(2)]
    pad_rows = jnp.zeros((LANES - MLA_NOPE - MLA_ROPE, tm), F32)
    for p in range(MLA_HEADS // 2):
        nope_t = group(q2, UQ_NOPE, p).T
        v_t = group(kv2, UKV_V, p).T
        for e in range(2):
            h = 2 * p + e
            r = (h % 4) * MLA_ROPE
            qa = jnp.concatenate([nope_t[e * HALF:(e + 1) * HALF], rope_t[h // 4][r:r + MLA_ROPE], pad_rows], axis=0)
            qa_ref[0, h] = (qa * qscale).astype(BF16)
            _store_vt(va_ref, h, v_t[e * HALF:(e + 1) * HALF], t)
    for h in range(MLA_HEADS):
        ka_ref[0, h] = (group(kv2, 0, h) + krope).astype(BF16)

    relf = rel_ref[0].astype(F32)

    pq = _dot(nb, w_ref[:, O_DQ:O_DK])
    pk = _dot(nb, w_ref[:, O_DK:O_DV])
    pv = _dot(nb, w_ref[:, O_DV:O_MQ])
    dscale = DIFF_HD ** -0.5 * LOG2E
    ones_rows = _bias_rows(DIFF_MAP2 - DIFF_HD, tm)
    lane1 = _lane((1, LANES))
    qn_row = jnp.zeros((1, LANES), F32)
    kn_row = jnp.zeros((1, LANES), F32)
    for p in range(DIFF_HEADS // 2):
        q_t = group(pq, 0, p).T * dscale
        v_t = group(pv, 0, p).T
        for e in range(2):
            o = e * HALF
            qd_ref[0, 2 * p + e] = jnp.concatenate(
                [q_t[o:o + DIFF_HD], ones_rows, q_t[o + DIFF_HD:o + HALF], ones_rows], axis=0).astype(BF16)
            _store_vt(vd_ref, 2 * p + e, v_t[o:o + HALF], t)
            for mp in range(2):
                qm = q_t[o + mp * DIFF_HD:o + (mp + 1) * DIFF_HD]
                qsq = jnp.max(jnp.sum(qm * qm, axis=0, keepdims=True), axis=1, keepdims=True)
                qn_row = jnp.where(lane1 == 2 * (2 * p + e) + mp, jnp.sqrt(qsq), qn_row)
    in_map1 = _lane((tm, LANES)) < DIFF_MAP2
    for h in range(DIFF_HEADS):
        kb = group(pk, 0, h)
        for mp in range(2):
            mine = in_map1 if mp == 0 else jnp.logical_not(in_map1)
            ksq = jnp.sum(jnp.where(mine, kb * kb, 0.0), axis=1, keepdims=True)
            kn_row = jnp.where(lane1 == 2 * h + mp, jnp.sqrt(jnp.max(ksq, axis=0, keepdims=True)), kn_row)
        for i, piece in enumerate(_split3(relf * (_alibi_slope(h) * LOG2E))):
            kb = kb + piece * _onehot_lanes([DIFF_BIAS + i, DIFF_MAP2 + DIFF_BIAS + i])
        kd_ref[0, h] = kb.astype(BF16)
    qnorm_ref[0, 0] = qn_row
    knorm_ref[0, 0] = kn_row

    pq = _dot(nb, w_ref[:, O_MQ:O_MK])
    pk = _dot(nb, w_ref[:, O_MK:O_MV])
    pv = _dot(nb, w_ref[:, O_MV:C_WIDE])
    for p in range(MOBA_HEADS // 2):
        q_t = group(pq, 0, p).T
        v_t = group(pv, 0, p).T
        for e in range(2):
            mq_ref[0, 2 * p + e] = q_t[e * HALF:(e + 1) * HALF]
            _store_vt(mv_ref, 2 * p + e, v_t[e * HALF:(e + 1) * HALF], t)
    row = lax.broadcasted_iota(jnp.int32, (tm, 1), 0)
    blk = (si * tm + row) // MOBA_BLOCK
    blk_onehot = jnp.where(_lane((tm, LANES)) == MOBA_SEL + blk, 1.0, 0.0)
    for h in range(MOBA_HEADS):
        kb = group(pk, 0, h) + blk_onehot
        for i, piece in enumerate(_split3(relf * (_alibi_slope(DIFF_HEADS + h) * LOG2E))):
            kb = kb + piece * _onehot_lanes([MOBA_BIAS + i])
        mk_ref[0, h] = kb.astype(BF16)
    for j in range(tm // MOBA_BLOCK):
        kmean_ref[0, j] = jnp.mean(pk[j * MOBA_BLOCK:(j + 1) * MOBA_BLOCK, :], axis=0, keepdims=True)


def _layer_spec(a, l, single_buffer=False):
    zeros = (0,) * (a.ndim - 1)
    mode = dict(pipeline_mode=pl.Buffered(1)) if single_buffer else {}
    return pl.BlockSpec((None,) + a.shape[1:], lambda *_: (l,) + zeros, **mode)


def _inproj(h, pos, rel, inv_lane, g, w_wide, qn, wuq, kvn, wukv, *, l, tm, t):
    B, S, _ = h.shape
    grid = (B, S // tm)
    tok = lambda b, i: (b, i, 0)
    const2 = lambda b, i: (0, 0)
    q_out = lambda nh: (jax.ShapeDtypeStruct((B, nh, LANES, S), BF16),
                        pl.BlockSpec((1, nh, LANES, tm), lambda b, i: (b, 0, 0, i)))
    k_out = lambda nh, dt: (jax.ShapeDtypeStruct((B, nh, S, LANES), dt),
                            pl.BlockSpec((1, nh, tm, LANES), lambda b, i: (b, 0, i, 0)))
    v_out = lambda nh: (jax.ShapeDtypeStruct((B, nh, S // t, V_ROWS, t), BF16),
                        pl.BlockSpec((1, nh, tm // t, V_ROWS, t), lambda b, i: (b, 0, i, 0, 0)))
    outs = [q_out(MLA_HEADS), k_out(MLA_HEADS, BF16), v_out(MLA_HEADS),
            q_out(DIFF_HEADS), k_out(DIFF_HEADS, BF16), v_out(DIFF_HEADS),
            (jax.ShapeDtypeStruct((B, MOBA_HEADS, MOBA_HD, S), F32),
             pl.BlockSpec((1, MOBA_HEADS, MOBA_HD, tm), lambda b, i: (b, 0, 0, i))),
            k_out(MOBA_HEADS, BF16), v_out(MOBA_HEADS),
            (jax.ShapeDtypeStruct((B, S // MOBA_BLOCK, 1, MOBA_HEADS * LANES), F32),
             pl.BlockSpec((1, tm // MOBA_BLOCK, 1, MOBA_HEADS * LANES), lambda b, i: (b, i, 0, 0)))]
    outs += [(jax.ShapeDtypeStruct((B, S // tm, 1, LANES), F32),
              pl.BlockSpec((1, 1, 1, LANES), lambda b, i: (b, i, 0, 0)))] * 2
    full = lambda a: pl.BlockSpec(a.shape, const2)
    return pl.pallas_call(
        functools.partial(_inproj_kernel, tm=tm, t=t),
        grid=grid,
        in_specs=[pl.BlockSpec((1, tm, D_MODEL), tok), pl.BlockSpec((1, 1, tm), lambda b, i: (b, 0, i)),
                  pl.BlockSpec((1, tm, 1), tok), full(inv_lane)]
                 + [_layer_spec(a, l) for a in (g, w_wide, qn, wuq, kvn, wukv)],
        out_specs=[o[1] for o in outs],
        out_shape=[o[0] for o in outs],
        compiler_params=pltpu.CompilerParams(dimension_semantics=("parallel", "parallel"),
                                             vmem_limit_bytes=VMEM_LIMIT),
        name="inproj",
    )(h, pos, rel, inv_lane, g, w_wide, qn, wuq, kvn, wukv)


def _moba_gate_kernel(mq_ref, km_ref, o_ref, *, tm):
    si = pl.program_id(1)
    nblk = LANES - MOBA_SEL
    blk = lax.broadcasted_iota(jnp.int32, (nblk, tm), 0)
    blkf = blk.astype(F32)
    own = (si * tm + _lane((1, tm))) // MOBA_BLOCK
    valid = blk < own
    ones_rows = _bias_rows(MOBA_SEL - MOBA_HD, tm)
    for h in range(MOBA_HEADS):
        qt = mq_ref[0, h]
        gate = jnp.dot(km_ref[0, h], qt, precision=lax.Precision.HIGHEST, preferred_element_type=F32)
        g = jnp.where(valid, gate, NEG)
        sel = blk == own
        for _ in range(MOBA_TOPK):
            mx = jnp.max(g, axis=0, keepdims=True)
            idx = jnp.min(jnp.where(g == mx, blkf, float(nblk)), axis=0, keepdims=True)
            pick = (blkf == idx) & (mx > 0.5 * NEG)
            sel = sel | pick
            g = jnp.where(pick, NEG, g)
        selbias = jnp.where(sel, 0.0, NEG)
        o_ref[0, h] = jnp.concatenate([qt * (MOBA_HD ** -0.5 * LOG2E), ones_rows, selbias],
                                      axis=0).astype(BF16)


def _moba_gate(mq, km_pad, *, tm):
    B, H, _, S = mq.shape
    return pl.pallas_call(
        functools.partial(_moba_gate_kernel, tm=tm),
        grid=(B, S // tm),
        in_specs=[pl.BlockSpec((1, H, MOBA_HD, tm), lambda b, i: (b, 0, 0, i)),
                  pl.BlockSpec((1, H, LANES - MOBA_SEL, MOBA_HD), lambda b, i: (b, 0, 0, 0))],
        out_specs=pl.BlockSpec((1, H, LANES, tm), lambda b, i: (b, 0, 0, i)),
        out_shape=jax.ShapeDtypeStruct((B, H, LANES, S), BF16),
        compiler_params=pltpu.CompilerParams(dimension_semantics=("parallel", "parallel"),
                                             vmem_limit_bytes=VMEM_LIMIT),
        name="moba_gate",
    )(mq, km_pad)


def _causal_flash(chains, next_chains, k_ref, vt_ref, scratch, qi, j0, j0_next, t, tk):
    assert t == 2 * tk
    s_a, s_b, mb_a, mb_b, m_scr, acc_scr = scratch
    nc = len(chains)

    every, first, second = slice(0, t), slice(0, tk), slice(tk, t)

    def scores(kt, s_dst, mb_dst, qs=every, chains=chains):
        tiles = {}
        for c, (hh, qt) in enumerate(chains):
            if hh not in tiles:
                tiles[hh] = k_ref[0, hh, pl.ds(pl.multiple_of(kt * tk, tk), tk), :]
            s = _dot(tiles[hh], qt[:, qs])
            s_dst[c, :, qs] = s
            mb_dst[c, :, qs] = jnp.max(s, axis=0, keepdims=True)

    def softmax_pv(kt, s_src, mb_src, qs=every, diagonal=False):
        for c, (hh, _) in enumerate(chains):
            s = s_src[c, :, qs]
            if diagonal:
                keep = (lax.broadcasted_iota(jnp.int32, (tk, tk), 0)
                        <= lax.broadcasted_iota(jnp.int32, (tk, tk), 1))
                s = jnp.where(keep, s, NEG)
                mb = jnp.max(s, axis=0, keepdims=True)
            else:
                mb = mb_src[c, :, qs]
            m = m_scr[c, :, qs]
            m_new = jnp.maximum(m, mb)
            p = jnp.exp2(s - m_new).astype(BF16)
            acc_scr[c, :, qs] = acc_scr[c, :, qs] * jnp.exp2(m - m_new) + _dot(vt_ref[0, hh, kt], p)
            m_scr[c, :, qs] = m_new

    for c in range(nc):
        m_scr[c] = jnp.full((1, t), NEG, F32)
        acc_scr[c] = jnp.zeros((V_ROWS, t), F32)

    @pl.when(qi == 0)
    def _():
        scores(2 * j0, s_a, mb_a)

    def pair(j, carry):
        scores(2 * j + 1, s_b, mb_b)
        softmax_pv(2 * j, s_a, mb_a)
        scores(2 * j + 2, s_a, mb_a)
        softmax_pv(2 * j + 1, s_b, mb_b)
        return carry

    unroll = min(MAX_UNROLL, UNROLL_CHAIN_PAIRS // nc)
    start = j0
    for width in sorted({unroll, min(unroll, 2), 1}, reverse=True):
        def body(i, carry, width=width, start=start):
            for u in range(width):
                pair(start + width * i + u, carry)
            return carry

        trips = (qi - start) // width
        lax.fori_loop(0, trips, body, 0)
        start = start + trips * width
    scores(2 * qi + 1, s_b, mb_b, second)
    softmax_pv(2 * qi, s_a, mb_a, first, diagonal=True)
    softmax_pv(2 * qi, s_a, mb_a, second)
    scores(2 * j0_next, s_a, mb_a, chains=next_chains)
    softmax_pv(2 * qi + 1, s_b, mb_b, second, diagonal=True)
    outs = []
    for c in range(nc):
        acc = acc_scr[c]
        outs.append(acc[:V_DIM] * (1.0 / acc[V_ONE:V_ONE + 1]))
    return outs


def _flash_scratch(nc, t, tk):
    return ([pltpu.VMEM((nc, tk, t), F32)] * 2 + [pltpu.VMEM((nc, 1, t), F32)] * 3
            + [pltpu.VMEM((nc, V_ROWS, t), F32)])


def _first_pairs(j0_ref):
    b, p, i = pl.program_id(0), pl.program_id(1), pl.program_id(2)
    step = (b * pl.num_programs(1) + p) * pl.num_programs(2) + i
    last = pl.num_programs(0) * pl.num_programs(1) * pl.num_programs(2) - 1
    return j0_ref[step], j0_ref[jnp.minimum(step + 1, last)]


def _mla_attn_kernel(j0_ref, q_ref, qnext_ref, k_ref, vt_ref, o_ref, *scratch, t, tk):
    heads = range(q_ref.shape[1])
    outs = _causal_flash([(hh, q_ref[0, hh]) for hh in heads], [(hh, qnext_ref[0, hh]) for hh in heads],
                         k_ref, vt_ref, scratch, pl.program_id(2), *_first_pairs(j0_ref), t, tk)
    o_ref[0] = jnp.concatenate(outs, axis=0).astype(o_ref.dtype)


def _diff_attn_kernel(j0_ref, lam_ref, gain_ref, q_ref, qnext_ref, k_ref, vt_ref, o_ref, *scratch, t, tk, lam_init):
    lv = lam_ref[...]
    lam = (jnp.exp(jnp.sum(lv[0:1] * lv[1:2], axis=-1, keepdims=True))
           - jnp.exp(jnp.sum(lv[2:3] * lv[3:4], axis=-1, keepdims=True)) + lam_init)
    feat = lax.broadcasted_iota(jnp.int32, (LANES, t), 0)

    def map_chains(ref):
        chains = []
        for hh in range(ref.shape[1]):
            q = ref[0, hh]
            chains.append((hh, jnp.where(feat < DIFF_MAP2, q, jnp.zeros_like(q))))
            chains.append((hh, jnp.where(feat >= DIFF_MAP2, q, jnp.zeros_like(q))))
        return chains

    res = _causal_flash(map_chains(q_ref), map_chains(qnext_ref), k_ref, vt_ref, scratch,
                        pl.program_id(2), *_first_pairs(j0_ref), t, tk)
    outs = []
    for a, b in zip(res[0::2], res[1::2]):
        o = a - lam * b
        outs.append(o * lax.rsqrt(jnp.mean(o * o, axis=0, keepdims=True) + EPS))
    o_ref[0] = (jnp.concatenate(outs, axis=0) * gain_ref[...] * (1.0 - lam_init)).astype(o_ref.dtype)


def _group_attention(kernel_fn, q, k, vt, extra=(), first_pair=None, *, t, tk, group, chains, name):
    B, H, _, S = q.shape
    if first_pair is None:
        first_pair = jnp.zeros((B, H // group, S // t), jnp.int32)
    nq = S // t
    qspec = pl.BlockSpec((1, group, LANES, t), lambda b, p, i, j0: (b, p, 0, i))
    qnext = pl.BlockSpec((1, group, LANES, t), lambda b, p, i, j0: (b, p, 0, jnp.minimum(i + 1, nq - 1)))
    kspec = pl.BlockSpec((1, group, S, LANES), lambda b, p, i, j0: (b, p, 0, 0))
    vspec = pl.BlockSpec((1, group, S // tk, V_ROWS, tk), lambda b, p, i, j0: (b, p, 0, 0, 0))
    xspecs = [pl.BlockSpec(a.shape, lambda b, p, i, j0: (0, 0)) for a in extra]
    return pl.pallas_call(
        functools.partial(kernel_fn, t=t, tk=tk),
        grid_spec=pltpu.PrefetchScalarGridSpec(
            num_scalar_prefetch=1,
            grid=(B, H // group, S // t),
            in_specs=xspecs + [qspec, qnext, kspec, vspec],
            out_specs=pl.BlockSpec((1, group * V_DIM, t), lambda b, p, i, j0: (b, p, i)),
            scratch_shapes=_flash_scratch(chains, t, tk)),
        out_shape=jax.ShapeDtypeStruct((B, H * V_DIM, S), BF16),
        compiler_params=pltpu.CompilerParams(dimension_semantics=("parallel", "parallel", "arbitrary"),
                                             vmem_limit_bytes=VMEM_LIMIT),
        name=name,
    )(first_pair.reshape(-1), *extra, q, q, k, vt)


def _memkv_kernel(x_ref, g_ref, w_ref, o_ref):
    o_ref[0] = _dot(_rms(x_ref[0], g_ref[...]).astype(BF16), w_ref[...]).astype(o_ref.dtype)


def _memkv(mem, g, wkv, *, l):
    B, M, _ = mem.shape
    return pl.pallas_call(
        _memkv_kernel,
        grid=(B,),
        in_specs=[pl.BlockSpec((1, M, D_MODEL), lambda b: (b, 0, 0)), _layer_spec(g, l), _layer_spec(wkv, l)],
        out_specs=pl.BlockSpec((1, M, 2 * D_MODEL), lambda b: (b, 0, 0)),
        out_shape=jax.ShapeDtypeStruct((B, M, 2 * D_MODEL), BF16),
        compiler_params=pltpu.CompilerParams(dimension_semantics=("parallel",), vmem_limit_bytes=VMEM_LIMIT),
        name="memkv",
    )(mem, g, wkv)


def _post_kernel(h_ref, oa_ref, ob_ref, oc_ref, wout_ref, g_ref, wq_ref, kv_ref, wo_ref,
                 g2_ref, w1_ref, w2_ref, gf_ref, o_ref, *, final, chunk):
    na, nb = oa_ref.shape[1], ob_ref.shape[1]
    h1 = (h_ref[0] + _dot_tn(oa_ref[0], wout_ref[0:na]) + _dot_tn(ob_ref[0], wout_ref[na:na + nb])
          + _dot_tn(oc_ref[0], wout_ref[na + nb:]))
    n = _rms(h1, g_ref[...]).astype(BF16)
    q = (_dot(n, wq_ref[...]) * (CROSS_HD ** -0.5 * LOG2E)).astype(BF16)
    ctx = []
    for h in range(CROSS_HEADS):
        sl = slice(h * CROSS_HD, (h + 1) * CROSS_HD)
        s = _dot_nt(q[:, sl], kv_ref[0, :, sl])
        p = jnp.exp2(s - jnp.max(s, axis=-1, keepdims=True))
        l = jnp.sum(p, axis=-1, keepdims=True)
        c = _dot(p.astype(BF16), kv_ref[0, :, D_MODEL + h * CROSS_HD:D_MODEL + (h + 1) * CROSS_HD])
        ctx.append((c * (1.0 / l)).astype(BF16))
    h2 = h1 + _dot(jnp.concatenate(ctx, axis=-1), wo_ref[...])
    n = _rms(h2, g2_ref[...]).astype(BF16)
    acc = h2
    for c in range(D_FF // chunk):
        a = jnp.maximum(_dot(n, w1_ref[:, c * chunk:(c + 1) * chunk]), 0.0)
        acc = acc + _dot((a * a).astype(BF16), w2_ref[c * chunk:(c + 1) * chunk, :])
    o_ref[0] = _rms(acc, gf_ref[...]) if final else acc


def _post(h, oa, ob, oc, wout, g, wq, memkv, wo, g2, w1, w2, gf, *, l, tm, final):
    B, S, _ = h.shape
    tok = lambda b, i: (b, i, 0)
    const2 = lambda b, i: (0, 0)
    small = lambda a: pl.BlockSpec(a.shape, const2)
    resident = lambda a: _layer_spec(a, l, single_buffer=True)
    return pl.pallas_call(
        functools.partial(_post_kernel, final=final, chunk=D_MODEL),
        grid=(B, S // tm),
        in_specs=[pl.BlockSpec((1, tm, D_MODEL), tok)]
                 + [pl.BlockSpec((1, o.shape[1], tm), lambda b, i: (b, 0, i)) for o in (oa, ob, oc)] + [
                  resident(wout), _layer_spec(g, l), resident(wq),
                  pl.BlockSpec((1,) + memkv.shape[1:], lambda b, i: (b, 0, 0)), resident(wo),
                  _layer_spec(g2, l), resident(w1), resident(w2), small(gf)],
        out_specs=pl.BlockSpec((1, tm, D_MODEL), tok),
        out_shape=jax.ShapeDtypeStruct(h.shape, F32),
        compiler_params=pltpu.CompilerParams(dimension_semantics=("parallel", "parallel"),
                                             vmem_limit_bytes=VMEM_LIMIT),
        name="post",
    )(h, oa, ob, oc, wout, g, wq, memkv, wo, g2, w1, w2, gf)


def _rot_pairs(w):
    half = w.shape[-1] // 2
    return jnp.concatenate([-w[..., half:], w[..., :half]], axis=-1)


def _pad_lanes(w, lo, width=LANES):
    pad = [(0, 0)] * (w.ndim - 1) + [(lo, width - lo - w.shape[-1])]
    return jnp.pad(w, pad)


def _head_groups(w, nheads, hd, lo=0, width=LANES):
    K = w.shape[0]
    return _pad_lanes(w.reshape(K, nheads, hd), lo, width).reshape(K, nheads * width)


def _widen_w_in(w):
    cuts = np.cumsum([0, MLA_Q_RANK, MLA_KV_RANK, MLA_ROPE, C_DIFF, C_DIFF, C_DIFF, C_MOBA, C_MOBA, C_MOBA])
    cq, ckv, kr, dq, dk, dv, mq, mk, mv = [w[:, int(a):int(b)] for a, b in zip(cuts[:-1], cuts[1:])]
    two_maps = lambda x: _head_groups(x, 2 * DIFF_HEADS, DIFF_HD, 0, HALF)
    out = jnp.concatenate([
        cq, ckv, _pad_lanes(kr, MLA_NOPE), _pad_lanes(_rot_pairs(kr), MLA_NOPE),
        dq, two_maps(dk), dv, mq, _head_groups(mk, MOBA_HEADS, MOBA_HD), mv], axis=1).astype(BF16)
    assert out.shape[1] == C_WIDE
    return out


def _widen_w_uq(w):
    K = w.shape[0]
    e = w.reshape(K, MLA_HEADS, MLA_NOPE + MLA_ROPE)
    nope = e[..., :MLA_NOPE].reshape(K, MLA_HEADS * MLA_NOPE)
    rope = _pad_lanes(e[..., MLA_NOPE:].reshape(K, MLA_HEADS * MLA_ROPE), 0, 2 * LANES)
    rot = _pad_lanes(_rot_pairs(e[..., MLA_NOPE:]).reshape(K, MLA_HEADS * MLA_ROPE), 0, 2 * LANES)
    out = jnp.concatenate([nope, rope, rot], axis=1).astype(BF16)
    assert out.shape[1] == UQ_COLS
    return out


def _widen_w_ukv(w):
    K = w.shape[0]
    e = w.reshape(K, MLA_HEADS, MLA_NOPE + MLA_V)
    kn = _pad_lanes(e[..., :MLA_NOPE], 0).reshape(K, MLA_HEADS * LANES)
    vv = e[..., MLA_NOPE:].reshape(K, MLA_HEADS * MLA_V)
    out = jnp.concatenate([kn, vv], axis=1).astype(BF16)
    assert out.shape[1] == UKV_COLS
    return out


def _diff_first_pairs(qnorm, knorm, rel, t, tk, group):
    B, nq = qnorm.shape[:2]
    nm = 2 * DIFF_HEADS
    qn = qnorm[:, :, 0, :nm]
    kn = jnp.max(knorm[:, :, 0, :nm], axis=1)
    spread = NORM_MARGIN * 2.0 * qn * kn[:, None, :]
    relf = rel[..., 0].astype(F32)
    dist = relf[:, ::t][:, :, None] - relf[:, tk - 1::tk][:, None, :]
    c = jnp.asarray([_alibi_slope(i // 2) * LOG2E for i in range(nm)], F32)
    dead = c[None, None, :, None] * dist[:, :, None, :] > spread[..., None] + UNDERFLOW_BITS
    tiles = jnp.sum(dead.astype(jnp.int32), axis=-1)
    pairs = jnp.min(tiles.reshape(B, nq, DIFF_HEADS // group, 2 * group), axis=-1) // 2
    return pairs.transpose(0, 2, 1)


def kernel(x, mem, positions, attn_norm, w_in, mla_q_norm, mla_w_uq, mla_kv_norm, mla_w_ukv, diff_lambda_q1, diff_lambda_k1, diff_lambda_q2, diff_lambda_k2, diff_sub_norm, w_out, cross_norm, mem_norm, cross_wq, cross_wkv, cross_wo, mlp_norm, mlp_w1, mlp_w2, final_norm):
    B, S, _ = x.shape
    depth = w_in.shape[0]
    tm = 512
    t = 512
    tk = t // 2
    assert S % tm == 0 and tm == t and tk % MOBA_BLOCK == 0 and S // MOBA_BLOCK <= LANES - MOBA_SEL

    pos = positions.astype(jnp.int32)[:, None, :]
    rel = (positions - positions[:, :1]).astype(jnp.int32)[..., None]
    half = MLA_ROPE // 2
    inv_lane = (ROPE_THETA ** (-jnp.arange(half, dtype=F32) / half))[:, None]
    rows = lambda v: v.astype(F32)[:, None, :]
    wout_b, wq_b, wkv_b, wo_b, w1_b, w2_b = (w.astype(BF16) for w in (w_out, cross_wq, cross_wkv, cross_wo,
                                                                       mlp_w1, mlp_w2))
    w_wide, wuq, wukv = jax.vmap(_widen_w_in)(w_in), jax.vmap(_widen_w_uq)(mla_w_uq), jax.vmap(_widen_w_ukv)(mla_w_ukv)
    g_attn, g_q, g_kv, g_cross, g_mem, g_mlp = (rows(v) for v in (attn_norm, mla_q_norm, mla_kv_norm, cross_norm,
                                                                  mem_norm, mlp_norm))

    h = x
    for l in range(depth):
        outs = _inproj(h, pos, rel, inv_lane, g_attn, w_wide, g_q, wuq, g_kv, wukv, l=l, tm=tm, t=tk)
        qa, ka, va, qd, kd, vd, mq, mk, mv, kmean, qnorm, knorm = outs
        km = kmean.reshape(B, S // MOBA_BLOCK, MOBA_HEADS, LANES)[..., :MOBA_HD].transpose(0, 2, 1, 3)
        km_pad = jnp.pad(km, ((0, 0), (0, 0), (0, LANES - MOBA_SEL - S // MOBA_BLOCK), (0, 0)))
        mq_aug = _moba_gate(mq, km_pad, tm=math.gcd(S, GATE_TILE))

        attn = functools.partial(_group_attention, t=t, tk=tk)
        o_a = attn(_mla_attn_kernel, qa, ka, va, group=3, chains=3, name="mla_attn")
        lam_rows = jnp.stack([diff_lambda_q1[l], diff_lambda_k1[l], diff_lambda_q2[l], diff_lambda_k2[l]])
        lam_rows = jnp.pad(lam_rows.astype(F32), ((0, 4), (0, LANES - DIFF_HD)))
        gain = jnp.tile(diff_sub_norm[l].astype(F32), DIFF_GROUP)[:, None]
        lam_init = 0.8 - 0.6 * math.exp(-0.3 * l)
        o_b = attn(functools.partial(_diff_attn_kernel, lam_init=lam_init), qd, kd, vd,
                   extra=(lam_rows, gain), first_pair=_diff_first_pairs(qnorm, knorm, rel, t, tk, DIFF_GROUP),
                   group=DIFF_GROUP, chains=2 * DIFF_GROUP, name="diff_attn")
        o_c = attn(_mla_attn_kernel, mq_aug, mk, mv, group=4, chains=4, name="moba_attn")

        memkv = _memkv(mem, g_mem, wkv_b, l=l)
        h = _post(h, o_a, o_b, o_c, wout_b, g_cross, wq_b, memkv, wo_b, g_mlp, w1_b, w2_b,
                  final_norm.astype(F32)[None, :], l=l, tm=tm, final=(l == depth - 1))
    return h
```

```python
import functools
import math

import jax
import jax.numpy as jnp
import numpy as np
from jax import lax
from jax.experimental import pallas as pl
from jax.experimental.pallas import tpu as pltpu

D_MODEL = 1024
MLA_HEADS = 6
MLA_NOPE = 64
MLA_ROPE = 32
MLA_V = 64
MLA_Q_RANK = 256
MLA_KV_RANK = 128
ROPE_THETA = 10000.0
DIFF_HEADS = 6
DIFF_HD = 32
MOBA_HEADS = 4
MOBA_HD = 64
MOBA_BLOCK = 256
MOBA_TOPK = 3
CROSS_HEADS = 4
CROSS_HD = D_MODEL // CROSS_HEADS
D_FF = 4 * D_MODEL
EPS = 1e-6
NEG = -1e30
N_ALIBI = DIFF_HEADS + MOBA_HEADS
C_DIFF = DIFF_HEADS * 2 * DIFF_HD
C_MOBA = MOBA_HEADS * MOBA_HD

LANES = 128
HALF = LANES // 2
LOG2E = 1.4426950408889634
VMEM_LIMIT = 56 * 1024 * 1024

F32 = jnp.float32
BF16 = jnp.bfloat16

V_DIM = 64
V_ONE = V_DIM
V_ROWS = 80
DIFF_MAP2 = HALF
DIFF_BIAS = DIFF_HD
MOBA_BIAS = MOBA_HD
MOBA_SEL = 96
N_SPLIT = 3
UNROLL_CHAIN_PAIRS = 12
GATE_TILE = 2048
MAX_UNROLL = 4
DIFF_GROUP = 2
UNDERFLOW_BITS = 152.0
NORM_MARGIN = 1.02

O_CQ = 0
O_CKV = O_CQ + MLA_Q_RANK
O_KR = O_CKV + MLA_KV_RANK
O_KRS = O_KR + LANES
O_DQ = O_KRS + LANES
O_DK = O_DQ + C_DIFF
O_DV = O_DK + DIFF_HEADS * LANES
O_MQ = O_DV + C_DIFF
O_MK = O_MQ + C_MOBA
O_MV = O_MK + MOBA_HEADS * LANES
C_WIDE = O_MV + C_MOBA
UQ_NOPE = 0
UQ_ROPE = MLA_HEADS * MLA_NOPE
UQ_ROT = UQ_ROPE + 2 * LANES
UQ_COLS = UQ_ROT + 2 * LANES
UKV_V = MLA_HEADS * LANES
UKV_COLS = UKV_V + MLA_HEADS * MLA_V


def _alibi_slope(h):
    return 2.0 ** (-8.0 * (h + 1) / N_ALIBI)


def _rms(x, g):
    return x * lax.rsqrt(jnp.mean(x * x, axis=-1, keepdims=True) + EPS) * g


def _dot(a, b):
    return jnp.dot(a, b, preferred_element_type=F32)


def _dot_nt(a, b):
    return lax.dot_general(a, b, (((1,), (1,)), ((), ())), preferred_element_type=F32)


def _dot_tn(a, b):
    return lax.dot_general(a, b, (((0,), (0,)), ((), ())), preferred_element_type=F32)


def _lane(shape):
    return lax.broadcasted_iota(jnp.int32, shape, len(shape) - 1)


def _onehot_lanes(lanes, value=1.0):
    l = _lane((1, LANES))
    out = jnp.zeros((1, LANES), F32)
    for i in lanes:
        out = jnp.where(l == i, value, out)
    return out


def _split3(x):
    hi = x.astype(BF16).astype(F32)
    r = x - hi
    mid = r.astype(BF16).astype(F32)
    lo = (r - mid).astype(BF16).astype(F32)
    return hi, mid, lo


def _store_vt(ref, h, vt, t):
    tm = vt.shape[1]
    tail = jnp.where(lax.broadcasted_iota(jnp.int32, (V_ROWS - V_DIM, tm), 0) == 0, 1.0, 0.0)
    x = jnp.concatenate([vt, tail], axis=0).astype(BF16)
    for j in range(tm // t):
        ref[0, h, j] = x[:, j * t:(j + 1) * t]


def _bias_rows(n, tm):
    return jnp.where(lax.broadcasted_iota(jnp.int32, (n, tm), 0) < N_SPLIT, 1.0, 0.0)


def _inproj_kernel(h_ref, pos_ref, rel_ref, inv_ref, g_ref, w_ref, qn_ref, wuq_ref, kvn_ref, wukv_ref,
                   qa_ref, ka_ref, va_ref, qd_ref, kd_ref, vd_ref, mq_ref, mk_ref, mv_ref, kmean_ref,
                   qnorm_ref, knorm_ref, *, tm, t):
    si = pl.program_id(1)
    nb = _rms(h_ref[0], g_ref[...]).astype(BF16)
    group = lambda x, o, i: x[:, o + i * LANES:o + (i + 1) * LANES]

    pm = _dot(nb, w_ref[:, O_CQ:O_DQ])
    cq = _rms(pm[:, O_CQ:O_CKV], qn_ref[...]).astype(BF16)
    ckv = _rms(pm[:, O_CKV:O_KR], kvn_ref[...]).astype(BF16)
    q2 = _dot(cq, wuq_ref[...])
    kv2 = _dot(ckv, wukv_ref[...])
    ang = inv_ref[...] * pos_ref[0].astype(F32)
    reps = LANES // ang.shape[0]
    cs = jnp.concatenate([jnp.cos(ang)] * reps, axis=0).T
    sn = jnp.concatenate([jnp.sin(ang)] * reps, axis=0).T
    krope = pm[:, O_KR:O_KRS] * cs + pm[:, O_KRS:O_DQ] * sn
    qscale = (MLA_NOPE + MLA_ROPE) ** -0.5 * LOG2E
    rope_t = [(group(q2, UQ_ROPE, i) * cs + group(q2, UQ_ROT, i) * sn).T for i in range(2)]
    pad_rows = jnp.zeros((LANES - MLA_NOPE - MLA_ROPE, tm), F32)
    for p in range(MLA_HEADS // 2):
        nope_t = group(q2, UQ_NOPE, p).T
        v_t = group(kv2, UKV_V, p).T
        for e in range(2):
            h = 2 * p + e
            r = (h % 4) * MLA_ROPE
            qa = jnp.concatenate([nope_t[e * HALF:(e + 1) * HALF], rope_t[h // 4][r:r + MLA_ROPE], pad_rows], axis=0)
            qa_ref[0, h] = (qa * qscale).astype(BF16)
            _store_vt(va_ref, h, v_t[e * HALF:(e + 1) * HALF], t)
    for h in range(MLA_HEADS):
        ka_ref[0, h] = (group(kv2, 0, h) + krope).astype(BF16)

    relf = rel_ref[0].astype(F32)

    pq = _dot(nb, w_ref[:, O_DQ:O_DK])
    pk = _dot(nb, w_ref[:, O_DK:O_DV])
    pv = _dot(nb, w_ref[:, O_DV:O_MQ])
    dscale = DIFF_HD ** -0.5 * LOG2E
    ones_rows = _bias_rows(DIFF_MAP2 - DIFF_HD, tm)
    lane1 = _lane((1, LANES))
    qn_row = jnp.zeros((1, LANES), F32)
    kn_row = jnp.zeros((1, LANES), F32)
    for p in range(DIFF_HEADS // 2):
        q_t = group(pq, 0, p).T * dscale
        v_t = group(pv, 0, p).T
        for e in range(2):
            o = e * HALF
            qd_ref[0, 2 * p + e] = jnp.concatenate(
                [q_t[o:o + DIFF_HD], ones_rows, q_t[o + DIFF_HD:o + HALF], ones_rows], axis=0).astype(BF16)
            _store_vt(vd_ref, 2 * p + e, v_t[o:o + HALF], t)
            for mp in range(2):
                qm = q_t[o + mp * DIFF_HD:o + (mp + 1) * DIFF_HD]
                qsq = jnp.max(jnp.sum(qm * qm, axis=0, keepdims=True), axis=1, keepdims=True)
                qn_row = jnp.where(lane1 == 2 * (2 * p + e) + mp, jnp.sqrt(qsq), qn_row)
    in_map1 = _lane((tm, LANES)) < DIFF_MAP2
    for h in range(DIFF_HEADS):
        kb = group(pk, 0, h)
        for mp in range(2):
            mine = in_map1 if mp == 0 else jnp.logical_not(in_map1)
            ksq = jnp.sum(jnp.where(mine, kb * kb, 0.0), axis=1, keepdims=True)
            kn_row = jnp.where(lane1 == 2 * h + mp, jnp.sqrt(jnp.max(ksq, axis=0, keepdims=True)), kn_row)
        for i, piece in enumerate(_split3(relf * (_alibi_slope(h) * LOG2E))):
            kb = kb + piece * _onehot_lanes([DIFF_BIAS + i, DIFF_MAP2 + DIFF_BIAS + i])
        kd_ref[0, h] = kb.astype(BF16)
    qnorm_ref[0, 0] = qn_row
    knorm_ref[0, 0] = kn_row

    pq = _dot(nb, w_ref[:, O_MQ:O_MK])
    pk = _dot(nb, w_ref[:, O_MK:O_MV])
    pv = _dot(nb, w_ref[:, O_MV:C_WIDE])
    for p in range(MOBA_HEADS // 2):
        q_t = group(pq, 0, p).T
        v_t = group(pv, 0, p).T
        for e in range(2):
            mq_ref[0, 2 * p + e] = q_t[e * HALF:(e + 1) * HALF]
            _store_vt(mv_ref, 2 * p + e, v_t[e * HALF:(e + 1) * HALF], t)
    row = lax.broadcasted_iota(jnp.int32, (tm, 1), 0)
    blk = (si * tm + row) // MOBA_BLOCK
    blk_onehot = jnp.where(_lane((tm, LANES)) == MOBA_SEL + blk, 1.0, 0.0)
    for h in range(MOBA_HEADS):
        kb = group(pk, 0, h) + blk_onehot
        for i, piece in enumerate(_split3(relf * (_alibi_slope(DIFF_HEADS + h) * LOG2E))):
            kb = kb + piece * _onehot_lanes([MOBA_BIAS + i])
        mk_ref[0, h] = kb.astype(BF16)
    for j in range(tm // MOBA_BLOCK):
        kmean_ref[0, j] = jnp.mean(pk[j * MOBA_BLOCK:(j + 1) * MOBA_BLOCK, :], axis=0, keepdims=True)


def _layer_spec(a, l, single_buffer=False):
    zeros = (0,) * (a.ndim - 1)
    mode = dict(pipeline_mode=pl.Buffered(1)) if single_buffer else {}
    return pl.BlockSpec((None,) + a.shape[1:], lambda *_: (l,) + zeros, **mode)


def _inproj(h, pos, rel, inv_lane, g, w_wide, qn, wuq, kvn, wukv, *, l, tm, t):
    B, S, _ = h.shape
    grid = (B, S // tm)
    tok = lambda b, i: (b, i, 0)
    const2 = lambda b, i: (0, 0)
    q_out = lambda nh: (jax.ShapeDtypeStruct((B, nh, LANES, S), BF16),
                        pl.BlockSpec((1, nh, LANES, tm), lambda b, i: (b, 0, 0, i)))
    k_out = lambda nh, dt: (jax.ShapeDtypeStruct((B, nh, S, LANES), dt),
                            pl.BlockSpec((1, nh, tm, LANES), lambda b, i: (b, 0, i, 0)))
    v_out = lambda nh: (jax.ShapeDtypeStruct((B, nh, S // t, V_ROWS, t), BF16),
                        pl.BlockSpec((1, nh, tm // t, V_ROWS, t), lambda b, i: (b, 0, i, 0, 0)))
    outs = [q_out(MLA_HEADS), k_out(MLA_HEADS, BF16), v_out(MLA_HEADS),
            q_out(DIFF_HEADS), k_out(DIFF_HEADS, BF16), v_out(DIFF_HEADS),
            (jax.ShapeDtypeStruct((B, MOBA_HEADS, MOBA_HD, S), F32),
             pl.BlockSpec((1, MOBA_HEADS, MOBA_HD, tm), lambda b, i: (b, 0, 0, i))),
            k_out(MOBA_HEADS, BF16), v_out(MOBA_HEADS),
            (jax.ShapeDtypeStruct((B, S // MOBA_BLOCK, 1, MOBA_HEADS * LANES), F32),
             pl.BlockSpec((1, tm // MOBA_BLOCK, 1, MOBA_HEADS * LANES), lambda b, i: (b, i, 0, 0)))]
    outs += [(jax.ShapeDtypeStruct((B, S // tm, 1, LANES), F32),
              pl.BlockSpec((1, 1, 1, LANES), lambda b, i: (b, i, 0, 0)))] * 2
    full = lambda a: pl.BlockSpec(a.shape, const2)
    return pl.pallas_call(
        functools.partial(_inproj_kernel, tm=tm, t=t),
        grid=grid,
        in_specs=[pl.BlockSpec((1, tm, D_MODEL), tok), pl.BlockSpec((1, 1, tm), lambda b, i: (b, 0, i)),
                  pl.BlockSpec((1, tm, 1), tok), full(inv_lane)]
                 + [_layer_spec(a, l) for a in (g, w_wide, qn, wuq, kvn, wukv)],
        out_specs=[o[1] for o in outs],
        out_shape=[o[0] for o in outs],
        compiler_params=pltpu.CompilerParams(dimension_semantics=("parallel", "parallel"),
                                             vmem_limit_bytes=VMEM_LIMIT),
        name="inproj",
    )(h, pos, rel, inv_lane, g, w_wide, qn, wuq, kvn, wukv)


def _moba_gate_kernel(mq_ref, km_ref, o_ref, *, tm):
    si = pl.program_id(1)
    nblk = LANES - MOBA_SEL
    blk = lax.broadcasted_iota(jnp.int32, (nblk, tm), 0)
    blkf = blk.astype(F32)
    own = (si * tm + _lane((1, tm))) // MOBA_BLOCK
    valid = blk < own
    ones_rows = _bias_rows(MOBA_SEL - MOBA_HD, tm)
    for h in range(MOBA_HEADS):
        qt = mq_ref[0, h]
        gate = jnp.dot(km_ref[0, h], qt, precision=lax.Precision.HIGHEST, preferred_element_type=F32)
        g = jnp.where(valid, gate, NEG)
        sel = blk == own
        for _ in range(MOBA_TOPK):
            mx = jnp.max(g, axis=0, keepdims=True)
            idx = jnp.min(jnp.where(g == mx, blkf, float(nblk)), axis=0, keepdims=True)
            pick = (blkf == idx) & (mx > 0.5 * NEG)
            sel = sel | pick
            g = jnp.where(pick, NEG, g)
        selbias = jnp.where(sel, 0.0, NEG)
        o_ref[0, h] = jnp.concatenate([qt * (MOBA_HD ** -0.5 * LOG2E), ones_rows, selbias],
                                      axis=0).astype(BF16)


def _moba_gate(mq, km_pad, *, tm):
    B, H, _, S = mq.shape
    return pl.pallas_call(
        functools.partial(_moba_gate_kernel, tm=tm),
        grid=(B, S // tm),
        in_specs=[pl.BlockSpec((1, H, MOBA_HD, tm), lambda b, i: (b, 0, 0, i)),
                  pl.BlockSpec((1, H, LANES - MOBA_SEL, MOBA_HD), lambda b, i: (b, 0, 0, 0))],
        out_specs=pl.BlockSpec((1, H, LANES, tm), lambda b, i: (b, 0, 0, i)),
        out_shape=jax.ShapeDtypeStruct((B, H, LANES, S), BF16),
        compiler_params=pltpu.CompilerParams(dimension_semantics=("parallel", "parallel"),
                                             vmem_limit_bytes=VMEM_LIMIT),
        name="moba_gate",
    )(mq, km_pad)


def _causal_flash(chains, next_chains, k_ref, vt_ref, scratch, qi, plan, t, tk, early=None):
    assert t == 2 * tk
    s_a, s_b, mb_a, mb_b, m_scr, acc_scr = scratch
    nc = len(chains)
    j0, j0_late, j0_next, j0_late_next = plan
    early = tuple(range(nc)) if early is None else tuple(early)

    every, first, second = slice(0, t), slice(0, tk), slice(tk, t)

    everyone = tuple(range(nc))
    late = tuple(c for c in everyone if c not in early)

    def scores(kt, s_dst, mb_dst, qs=every, chains=chains, only=everyone):
        tiles = {}
        for c, (hh, qt) in enumerate(chains):
            if c not in only:
                continue
            if hh not in tiles:
                tiles[hh] = k_ref[0, hh, pl.ds(pl.multiple_of(kt * tk, tk), tk), :]
            s = _dot(tiles[hh], qt[:, qs])
            s_dst[c, :, qs] = s
            mb_dst[c, :, qs] = jnp.max(s, axis=0, keepdims=True)

    def softmax_pv(kt, s_src, mb_src, qs=every, diagonal=False, only=everyone):
        for c, (hh, _) in enumerate(chains):
            if c not in only:
                continue
            s = s_src[c, :, qs]
            if diagonal:
                keep = (lax.broadcasted_iota(jnp.int32, (tk, tk), 0)
                        <= lax.broadcasted_iota(jnp.int32, (tk, tk), 1))
                s = jnp.where(keep, s, NEG)
                mb = jnp.max(s, axis=0, keepdims=True)
            else:
                mb = mb_src[c, :, qs]
            m = m_scr[c, :, qs]
            m_new = jnp.maximum(m, mb)
            p = jnp.exp2(s - m_new).astype(BF16)
            acc_scr[c, :, qs] = acc_scr[c, :, qs] * jnp.exp2(m - m_new) + _dot(vt_ref[0, hh, kt], p)
            m_scr[c, :, qs] = m_new

    for c in range(nc):
        m_scr[c] = jnp.full((1, t), NEG, F32)
        acc_scr[c] = jnp.zeros((V_ROWS, t), F32)

    @pl.when(qi == 0)
    def _():
        scores(2 * j0, s_a, mb_a, only=early)
        scores(2 * j0_late, s_a, mb_a, only=late)

    def far_pairs(start, stop, only):
        def pair(j):
            scores(2 * j + 1, s_b, mb_b, only=only)
            softmax_pv(2 * j, s_a, mb_a, only=only)
            scores(2 * j + 2, s_a, mb_a, only=only)
            softmax_pv(2 * j + 1, s_b, mb_b, only=only)

        unroll = min(MAX_UNROLL, UNROLL_CHAIN_PAIRS // len(only))
        for width in sorted({unroll, min(unroll, 2), 1}, reverse=True):
            def body(i, carry, width=width, start=start):
                for u in range(width):
                    pair(start + width * i + u)
                return carry

            trips = (stop - start) // width
            lax.fori_loop(0, trips, body, 0)
            start = start + trips * width

    if late:
        far_pairs(j0, j0_late, early)
    far_pairs(j0_late, qi, everyone)
    scores(2 * qi + 1, s_b, mb_b, second)
    softmax_pv(2 * qi, s_a, mb_a, first, diagonal=True)
    softmax_pv(2 * qi, s_a, mb_a, second)
    scores(2 * j0_next, s_a, mb_a, chains=next_chains, only=early)
    scores(2 * j0_late_next, s_a, mb_a, chains=next_chains, only=late)
    softmax_pv(2 * qi + 1, s_b, mb_b, second, diagonal=True)
    outs = []
    for c in range(nc):
        acc = acc_scr[c]
        outs.append(acc[:V_DIM] * (1.0 / acc[V_ONE:V_ONE + 1]))
    return outs


def _flash_scratch(nc, t, tk):
    return ([pltpu.VMEM((nc, tk, t), F32)] * 2 + [pltpu.VMEM((nc, 1, t), F32)] * 3
            + [pltpu.VMEM((nc, V_ROWS, t), F32)])


def _first_pairs(j0_ref):
    b, p, i = pl.program_id(0), pl.program_id(1), pl.program_id(2)
    step = (b * pl.num_programs(1) + p) * pl.num_programs(2) + i
    last = pl.num_programs(0) * pl.num_programs(1) * pl.num_programs(2) - 1
    nxt = jnp.minimum(step + 1, last)
    return j0_ref[2 * step], j0_ref[2 * step + 1], j0_ref[2 * nxt], j0_ref[2 * nxt + 1]


def _mla_attn_kernel(j0_ref, q_ref, qnext_ref, k_ref, vt_ref, o_ref, *scratch, t, tk):
    heads = range(q_ref.shape[1])
    outs = _causal_flash([(hh, q_ref[0, hh]) for hh in heads], [(hh, qnext_ref[0, hh]) for hh in heads],
                         k_ref, vt_ref, scratch, pl.program_id(2), _first_pairs(j0_ref), t, tk)
    o_ref[0] = jnp.concatenate(outs, axis=0).astype(o_ref.dtype)


def _diff_attn_kernel(j0_ref, lam_ref, gain_ref, q_ref, qnext_ref, k_ref, vt_ref, o_ref, *scratch, t, tk, lam_init):
    lv = lam_ref[...]
    lam = (jnp.exp(jnp.sum(lv[0:1] * lv[1:2], axis=-1, keepdims=True))
           - jnp.exp(jnp.sum(lv[2:3] * lv[3:4], axis=-1, keepdims=True)) + lam_init)
    feat = lax.broadcasted_iota(jnp.int32, (LANES, t), 0)

    def map_chains(ref):
        chains = []
        for hh in range(ref.shape[1]):
            q = ref[0, hh]
            chains.append((hh, jnp.where(feat < DIFF_MAP2, q, jnp.zeros_like(q))))
            chains.append((hh, jnp.where(feat >= DIFF_MAP2, q, jnp.zeros_like(q))))
        return chains

    res = _causal_flash(map_chains(q_ref), map_chains(qnext_ref), k_ref, vt_ref, scratch,
                        pl.program_id(2), _first_pairs(j0_ref), t, tk, early=(2, 3))
    outs = []
    for a, b in zip(res[0::2], res[1::2]):
        o = a - lam * b
        outs.append(o * lax.rsqrt(jnp.mean(o * o, axis=0, keepdims=True) + EPS))
    o_ref[0] = (jnp.concatenate(outs, axis=0) * gain_ref[...] * (1.0 - lam_init)).astype(o_ref.dtype)


def _group_attention(kernel_fn, q, k, vt, extra=(), first_pair=None, *, t, tk, group, chains, name):
    B, H, _, S = q.shape
    if first_pair is None:
        first_pair = jnp.zeros((B, H // group, S // t, 2), jnp.int32)
    nq = S // t
    qspec = pl.BlockSpec((1, group, LANES, t), lambda b, p, i, j0: (b, p, 0, i))
    qnext = pl.BlockSpec((1, group, LANES, t), lambda b, p, i, j0: (b, p, 0, jnp.minimum(i + 1, nq - 1)))
    kspec = pl.BlockSpec((1, group, S, LANES), lambda b, p, i, j0: (b, p, 0, 0))
    vspec = pl.BlockSpec((1, group, S // tk, V_ROWS, tk), lambda b, p, i, j0: (b, p, 0, 0, 0))
    xspecs = [pl.BlockSpec(a.shape, lambda b, p, i, j0: (0, 0)) for a in extra]
    return pl.pallas_call(
        functools.partial(kernel_fn, t=t, tk=tk),
        grid_spec=pltpu.PrefetchScalarGridSpec(
            num_scalar_prefetch=1,
            grid=(B, H // group, S // t),
            in_specs=xspecs + [qspec, qnext, kspec, vspec],
            out_specs=pl.BlockSpec((1, group * V_DIM, t), lambda b, p, i, j0: (b, p, i)),
            scratch_shapes=_flash_scratch(chains, t, tk)),
        out_shape=jax.ShapeDtypeStruct((B, H * V_DIM, S), BF16),
        compiler_params=pltpu.CompilerParams(dimension_semantics=("parallel", "parallel", "arbitrary"),
                                             vmem_limit_bytes=VMEM_LIMIT),
        name=name,
    )(first_pair.reshape(-1), *extra, q, q, k, vt)


def _memkv_kernel(x_ref, g_ref, w_ref, o_ref):
    o_ref[0] = _dot(_rms(x_ref[0], g_ref[...]).astype(BF16), w_ref[...]).astype(o_ref.dtype)


def _memkv(mem, g, wkv, *, l):
    B, M, _ = mem.shape
    return pl.pallas_call(
        _memkv_kernel,
        grid=(B,),
        in_specs=[pl.BlockSpec((1, M, D_MODEL), lambda b: (b, 0, 0)), _layer_spec(g, l), _layer_spec(wkv, l)],
        out_specs=pl.BlockSpec((1, M, 2 * D_MODEL), lambda b: (b, 0, 0)),
        out_shape=jax.ShapeDtypeStruct((B, M, 2 * D_MODEL), BF16),
        compiler_params=pltpu.CompilerParams(dimension_semantics=("parallel",), vmem_limit_bytes=VMEM_LIMIT),
        name="memkv",
    )(mem, g, wkv)


def _post_kernel(h_ref, oa_ref, ob_ref, oc_ref, wout_ref, g_ref, wq_ref, kv_ref, wo_ref,
                 g2_ref, w1_ref, w2_ref, gf_ref, o_ref, *, final, chunk):
    na, nb = oa_ref.shape[1], ob_ref.shape[1]
    h1 = (h_ref[0] + _dot_tn(oa_ref[0], wout_ref[0:na]) + _dot_tn(ob_ref[0], wout_ref[na:na + nb])
          + _dot_tn(oc_ref[0], wout_ref[na + nb:]))
    n = _rms(h1, g_ref[...]).astype(BF16)
    q = (_dot(n, wq_ref[...]) * (CROSS_HD ** -0.5 * LOG2E)).astype(BF16)
    ctx = []
    for h in range(CROSS_HEADS):
        sl = slice(h * CROSS_HD, (h + 1) * CROSS_HD)
        s = _dot_nt(q[:, sl], kv_ref[0, :, sl])
        p = jnp.exp2(s - jnp.max(s, axis=-1, keepdims=True))
        l = jnp.sum(p, axis=-1, keepdims=True)
        c = _dot(p.astype(BF16), kv_ref[0, :, D_MODEL + h * CROSS_HD:D_MODEL + (h + 1) * CROSS_HD])
        ctx.append((c * (1.0 / l)).astype(BF16))
    h2 = h1 + _dot(jnp.concatenate(ctx, axis=-1), wo_ref[...])
    n = _rms(h2, g2_ref[...]).astype(BF16)
    acc = h2
    for c in range(D_FF // chunk):
        a = jnp.maximum(_dot(n, w1_ref[:, c * chunk:(c + 1) * chunk]), 0.0)
        acc = acc + _dot((a * a).astype(BF16), w2_ref[c * chunk:(c + 1) * chunk, :])
    o_ref[0] = _rms(acc, gf_ref[...]) if final else acc


def _post(h, oa, ob, oc, wout, g, wq, memkv, wo, g2, w1, w2, gf, *, l, tm, final):
    B, S, _ = h.shape
    tok = lambda b, i: (b, i, 0)
    const2 = lambda b, i: (0, 0)
    small = lambda a: pl.BlockSpec(a.shape, const2)
    resident = lambda a: _layer_spec(a, l, single_buffer=True)
    return pl.pallas_call(
        functools.partial(_post_kernel, final=final, chunk=D_MODEL),
        grid=(B, S // tm),
        in_specs=[pl.BlockSpec((1, tm, D_MODEL), tok)]
                 + [pl.BlockSpec((1, o.shape[1], tm), lambda b, i: (b, 0, i)) for o in (oa, ob, oc)] + [
                  resident(wout), _layer_spec(g, l), resident(wq),
                  pl.BlockSpec((1,) + memkv.shape[1:], lambda b, i: (b, 0, 0)), resident(wo),
                  _layer_spec(g2, l), resident(w1), resident(w2), small(gf)],
        out_specs=pl.BlockSpec((1, tm, D_MODEL), tok),
        out_shape=jax.ShapeDtypeStruct(h.shape, F32),
        compiler_params=pltpu.CompilerParams(dimension_semantics=("parallel", "parallel"),
                                             vmem_limit_bytes=VMEM_LIMIT),
        name="post",
    )(h, oa, ob, oc, wout, g, wq, memkv, wo, g2, w1, w2, gf)


def _rot_pairs(w):
    half = w.shape[-1] // 2
    return jnp.concatenate([-w[..., half:], w[..., :half]], axis=-1)


def _pad_lanes(w, lo, width=LANES):
    pad = [(0, 0)] * (w.ndim - 1) + [(lo, width - lo - w.shape[-1])]
    return jnp.pad(w, pad)


def _head_groups(w, nheads, hd, lo=0, width=LANES):
    K = w.shape[0]
    return _pad_lanes(w.reshape(K, nheads, hd), lo, width).reshape(K, nheads * width)


def _widen_w_in(w):
    cuts = np.cumsum([0, MLA_Q_RANK, MLA_KV_RANK, MLA_ROPE, C_DIFF, C_DIFF, C_DIFF, C_MOBA, C_MOBA, C_MOBA])
    cq, ckv, kr, dq, dk, dv, mq, mk, mv = [w[:, int(a):int(b)] for a, b in zip(cuts[:-1], cuts[1:])]
    two_maps = lambda x: _head_groups(x, 2 * DIFF_HEADS, DIFF_HD, 0, HALF)
    out = jnp.concatenate([
        cq, ckv, _pad_lanes(kr, MLA_NOPE), _pad_lanes(_rot_pairs(kr), MLA_NOPE),
        dq, two_maps(dk), dv, mq, _head_groups(mk, MOBA_HEADS, MOBA_HD), mv], axis=1).astype(BF16)
    assert out.shape[1] == C_WIDE
    return out


def _widen_w_uq(w):
    K = w.shape[0]
    e = w.reshape(K, MLA_HEADS, MLA_NOPE + MLA_ROPE)
    nope = e[..., :MLA_NOPE].reshape(K, MLA_HEADS * MLA_NOPE)
    rope = _pad_lanes(e[..., MLA_NOPE:].reshape(K, MLA_HEADS * MLA_ROPE), 0, 2 * LANES)
    rot = _pad_lanes(_rot_pairs(e[..., MLA_NOPE:]).reshape(K, MLA_HEADS * MLA_ROPE), 0, 2 * LANES)
    out = jnp.concatenate([nope, rope, rot], axis=1).astype(BF16)
    assert out.shape[1] == UQ_COLS
    return out


def _widen_w_ukv(w):
    K = w.shape[0]
    e = w.reshape(K, MLA_HEADS, MLA_NOPE + MLA_V)
    kn = _pad_lanes(e[..., :MLA_NOPE], 0).reshape(K, MLA_HEADS * LANES)
    vv = e[..., MLA_NOPE:].reshape(K, MLA_HEADS * MLA_V)
    out = jnp.concatenate([kn, vv], axis=1).astype(BF16)
    assert out.shape[1] == UKV_COLS
    return out


def _diff_first_pairs(qnorm, knorm, rel, t, tk, group):
    B, nq = qnorm.shape[:2]
    nm = 2 * DIFF_HEADS
    qn = qnorm[:, :, 0, :nm]
    kn = jnp.max(knorm[:, :, 0, :nm], axis=1)
    spread = NORM_MARGIN * 2.0 * qn * kn[:, None, :]
    relf = rel[..., 0].astype(F32)
    dist = relf[:, ::t][:, :, None] - relf[:, tk - 1::tk][:, None, :]
    c = jnp.asarray([_alibi_slope(i // 2) * LOG2E for i in range(nm)], F32)
    dead = c[None, None, :, None] * dist[:, :, None, :] > spread[..., None] + UNDERFLOW_BITS
    tiles = jnp.sum(dead.astype(jnp.int32), axis=-1)
    assert group == 2
    per_head = jnp.min(tiles.reshape(B, nq, DIFF_HEADS // 2, 2, 2), axis=-1) // 2
    late = per_head[..., 0]
    early = jnp.minimum(per_head[..., 1], late)
    return jnp.stack([early, late], axis=-1).transpose(0, 2, 1, 3)


def kernel(x, mem, positions, attn_norm, w_in, mla_q_norm, mla_w_uq, mla_kv_norm, mla_w_ukv, diff_lambda_q1, diff_lambda_k1, diff_lambda_q2, diff_lambda_k2, diff_sub_norm, w_out, cross_norm, mem_norm, cross_wq, cross_wkv, cross_wo, mlp_norm, mlp_w1, mlp_w2, final_norm):
    B, S, _ = x.shape
    depth = w_in.shape[0]
    tm = 512
    t = 512
    tk = t // 2
    assert S % tm == 0 and tm == t and tk % MOBA_BLOCK == 0 and S // MOBA_BLOCK <= LANES - MOBA_SEL

    pos = positions.astype(jnp.int32)[:, None, :]
    rel = (positions - positions[:, :1]).astype(jnp.int32)[..., None]
    half = MLA_ROPE // 2
    inv_lane = (ROPE_THETA ** (-jnp.arange(half, dtype=F32) / half))[:, None]
    rows = lambda v: v.astype(F32)[:, None, :]
    wout_b, wq_b, wkv_b, wo_b, w1_b, w2_b = (w.astype(BF16) for w in (w_out, cross_wq, cross_wkv, cross_wo,
                                                                       mlp_w1, mlp_w2))
    w_wide, wuq, wukv = jax.vmap(_widen_w_in)(w_in), jax.vmap(_widen_w_uq)(mla_w_uq), jax.vmap(_widen_w_ukv)(mla_w_ukv)
    g_attn, g_q, g_kv, g_cross, g_mem, g_mlp = (rows(v) for v in (attn_norm, mla_q_norm, mla_kv_norm, cross_norm,
                                                                  mem_norm, mlp_norm))

    h = x
    for l in range(depth):
        outs = _inproj(h, pos, rel, inv_lane, g_attn, w_wide, g_q, wuq, g_kv, wukv, l=l, tm=tm, t=tk)
        qa, ka, va, qd, kd, vd, mq, mk, mv, kmean, qnorm, knorm = outs
        km = kmean.reshape(B, S // MOBA_BLOCK, MOBA_HEADS, LANES)[..., :MOBA_HD].transpose(0, 2, 1, 3)
        km_pad = jnp.pad(km, ((0, 0), (0, 0), (0, LANES - MOBA_SEL - S // MOBA_BLOCK), (0, 0)))
        mq_aug = _moba_gate(mq, km_pad, tm=math.gcd(S, GATE_TILE))

        attn = functools.partial(_group_attention, t=t, tk=tk)
        o_a = attn(_mla_attn_kernel, qa, ka, va, group=3, chains=3, name="mla_attn")
        lam_rows = jnp.stack([diff_lambda_q1[l], diff_lambda_k1[l], diff_lambda_q2[l], diff_lambda_k2[l]])
        lam_rows = jnp.pad(lam_rows.astype(F32), ((0, 4), (0, LANES - DIFF_HD)))
        gain = jnp.tile(diff_sub_norm[l].astype(F32), DIFF_GROUP)[:, None]
        lam_init = 0.8 - 0.6 * math.exp(-0.3 * l)
        o_b = attn(functools.partial(_diff_attn_kernel, lam_init=lam_init), qd, kd, vd,
                   extra=(lam_rows, gain), first_pair=_diff_first_pairs(qnorm, knorm, rel, t, tk, DIFF_GROUP),
                   group=DIFF_GROUP, chains=2 * DIFF_GROUP, name="diff_attn")
        o_c = attn(_mla_attn_kernel, mq_aug, mk, mv, group=4, chains=4, name="moba_attn")

        memkv = _memkv(mem, g_mem, wkv_b, l=l)
        h = _post(h, o_a, o_b, o_c, wout_b, g_cross, wq_b, memkv, wo_b, g_mlp, w1_b, w2_b,
                  final_norm.astype(F32)[None, :], l=l, tm=tm, final=(l == depth - 1))
    return h
```

```python
import functools
import math

import jax
import jax.numpy as jnp
import numpy as np
from jax import lax
from jax.experimental import pallas as pl
from jax.experimental.pallas import tpu as pltpu

D_MODEL = 1024
MLA_HEADS = 6
MLA_NOPE = 64
MLA_ROPE = 32
MLA_V = 64
MLA_Q_RANK = 256
MLA_KV_RANK = 128
ROPE_THETA = 10000.0
DIFF_HEADS = 6
DIFF_HD = 32
MOBA_HEADS = 4
MOBA_HD = 64
MOBA_BLOCK = 256
MOBA_TOPK = 3
CROSS_HEADS = 4
CROSS_HD = D_MODEL // CROSS_HEADS
D_FF = 4 * D_MODEL
EPS = 1e-6
NEG = -1e30
N_ALIBI = DIFF_HEADS + MOBA_HEADS
C_DIFF = DIFF_HEADS * 2 * DIFF_HD
C_MOBA = MOBA_HEADS * MOBA_HD

LANES = 128
HALF = LANES // 2
LOG2E = 1.4426950408889634
VMEM_LIMIT = 56 * 1024 * 1024

F32 = jnp.float32
BF16 = jnp.bfloat16

V_DIM = 64
V_ONE = V_DIM
V_ROWS = 80
DIFF_MAP2 = HALF
DIFF_BIAS = DIFF_HD
MOBA_BIAS = MOBA_HD
MOBA_SEL = 96
N_SPLIT = 3
UNROLL_CHAIN_PAIRS = 12
GATE_TILE = 2048
MAX_UNROLL = 4
DIFF_GROUP = 2
UNDERFLOW_BITS = 152.0
NORM_MARGIN = 1.02

O_CQ = 0
O_CKV = O_CQ + MLA_Q_RANK
O_KR = O_CKV + MLA_KV_RANK
O_KRS = O_KR + LANES
O_DQ = O_KRS + LANES
O_DK = O_DQ + C_DIFF
O_DV = O_DK + DIFF_HEADS * LANES
O_MQ = O_DV + C_DIFF
O_MK = O_MQ + C_MOBA
O_MV = O_MK + MOBA_HEADS * LANES
C_WIDE = O_MV + C_MOBA
UQ_NOPE = 0
UQ_ROPE = MLA_HEADS * MLA_NOPE
UQ_ROT = UQ_ROPE + 2 * LANES
UQ_COLS = UQ_ROT + 2 * LANES
UKV_V = MLA_HEADS * LANES
UKV_COLS = UKV_V + MLA_HEADS * MLA_V


def _alibi_slope(h):
    return 2.0 ** (-8.0 * (h + 1) / N_ALIBI)


def _rms(x, g):
    return x * lax.rsqrt(jnp.mean(x * x, axis=-1, keepdims=True) + EPS) * g


def _dot(a, b):
    return jnp.dot(a, b, preferred_element_type=F32)


def _dot_nt(a, b):
    return lax.dot_general(a, b, (((1,), (1,)), ((), ())), preferred_element_type=F32)


def _dot_tn(a, b):
    return lax.dot_general(a, b, (((0,), (0,)), ((), ())), preferred_element_type=F32)


def _lane(shape):
    return lax.broadcasted_iota(jnp.int32, shape, len(shape) - 1)


def _onehot_lanes(lanes, value=1.0):
    l = _lane((1, LANES))
    out = jnp.zeros((1, LANES), F32)
    for i in lanes:
        out = jnp.where(l == i, value, out)
    return out


def _split3(x):
    hi = x.astype(BF16).astype(F32)
    r = x - hi
    mid = r.astype(BF16).astype(F32)
    lo = (r - mid).astype(BF16).astype(F32)
    return hi, mid, lo


def _store_vt(ref, h, vt, t):
    tm = vt.shape[1]
    tail = jnp.where(lax.broadcasted_iota(jnp.int32, (V_ROWS - V_DIM, tm), 0) == 0, 1.0, 0.0)
    x = jnp.concatenate([vt, tail], axis=0).astype(BF16)
    for j in range(tm // t):
        ref[0, h, j] = x[:, j * t:(j + 1) * t]


def _bias_rows(n, tm):
    return jnp.where(lax.broadcasted_iota(jnp.int32, (n, tm), 0) < N_SPLIT, 1.0, 0.0)


def _inproj_kernel(h_ref, pos_ref, rel_ref, inv_ref, g_ref, w_ref, qn_ref, wuq_ref, kvn_ref, wukv_ref,
                   qa_ref, ka_ref, va_ref, qd_ref, kd_ref, vd_ref, mq_ref, mk_ref, mv_ref, kmean_ref,
                   qnorm_ref, knorm_ref, *, tm, t):
    si = pl.program_id(1)
    nb = _rms(h_ref[0], g_ref[...]).astype(BF16)
    group = lambda x, o, i: x[:, o + i * LANES:o + (i + 1) * LANES]

    pm = _dot(nb, w_ref[:, O_CQ:O_DQ])
    cq = _rms(pm[:, O_CQ:O_CKV], qn_ref[...]).astype(BF16)
    ckv = _rms(pm[:, O_CKV:O_KR], kvn_ref[...]).astype(BF16)
    q2 = _dot(cq, wuq_ref[...])
    kv2 = _dot(ckv, wukv_ref[...])
    ang = inv_ref[...] * pos_ref[0].astype(F32)
    reps = LANES // ang.shape[0]
    cs = jnp.concatenate([jnp.cos(ang)] * reps, axis=0).T
    sn = jnp.concatenate([jnp.sin(ang)] * reps, axis=0).T
    krope = pm[:, O_KR:O_KRS] * cs + pm[:, O_KRS:O_DQ] * sn
    qscale = (MLA_NOPE + MLA_ROPE) ** -0.5 * LOG2E
    rope_t = [(group(q2, UQ_ROPE, i) * cs + group(q2, UQ_ROT, i) * sn).T for i in range(2)]
    pad_rows = jnp.zeros((LANES - MLA_NOPE - MLA_ROPE, tm), F32)
    for p in range(MLA_HEADS // 2):
        nope_t = group(q2, UQ_NOPE, p).T
        v_t = group(kv2, UKV_V, p).T
        for e in range(2):
            h = 2 * p + e
            r = (h % 4) * MLA_ROPE
            qa = jnp.concatenate([nope_t[e * HALF:(e + 1) * HALF], rope_t[h // 4][r:r + MLA_ROPE], pad_rows], axis=0)
            qa_ref[0, h] = (qa * qscale).astype(BF16)
            _store_vt(va_ref, h, v_t[e * HALF:(e + 1) * HALF], t)
    for h in range(MLA_HEADS):
        ka_ref[0, h] = (group(kv2, 0, h) + krope).astype(BF16)

    relf = rel_ref[0].astype(F32)

    pq = _dot(nb, w_ref[:, O_DQ:O_DK])
    pk = _dot(nb, w_ref[:, O_DK:O_DV])
    pv = _dot(nb, w_ref[:, O_DV:O_MQ])
    dscale = DIFF_HD ** -0.5 * LOG2E
    ones_rows = _bias_rows(DIFF_MAP2 - DIFF_HD, tm)
    lane1 = _lane((1, LANES))
    qn_row = jnp.zeros((1, LANES), F32)
    kn_row = jnp.zeros((1, LANES), F32)
    for p in range(DIFF_HEADS // 2):
        q_t = group(pq, 0, p).T * dscale
        v_t = group(pv, 0, p).T
        for e in range(2):
            o = e * HALF
            qd_ref[0, 2 * p + e] = jnp.concatenate(
                [q_t[o:o + DIFF_HD], ones_rows, q_t[o + DIFF_HD:o + HALF], ones_rows], axis=0).astype(BF16)
            _store_vt(vd_ref, 2 * p + e, v_t[o:o + HALF], t)
            for mp in range(2):
                qm = q_t[o + mp * DIFF_HD:o + (mp + 1) * DIFF_HD]
                qsq = jnp.max(jnp.sum(qm * qm, axis=0, keepdims=True), axis=1, keepdims=True)
                qn_row = jnp.where(lane1 == 2 * (2 * p + e) + mp, jnp.sqrt(qsq), qn_row)
    in_map1 = _lane((tm, LANES)) < DIFF_MAP2
    for h in range(DIFF_HEADS):
        kb = group(pk, 0, h)
        for mp in range(2):
            mine = in_map1 if mp == 0 else jnp.logical_not(in_map1)
            ksq = jnp.sum(jnp.where(mine, kb * kb, 0.0), axis=1, keepdims=True)
            kn_row = jnp.where(lane1 == 2 * h + mp, jnp.sqrt(jnp.max(ksq, axis=0, keepdims=True)), kn_row)
        for i, piece in enumerate(_split3(relf * (_alibi_slope(h) * LOG2E))):
            kb = kb + piece * _onehot_lanes([DIFF_BIAS + i, DIFF_MAP2 + DIFF_BIAS + i])
        kd_ref[0, h] = kb.astype(BF16)
    qnorm_ref[0, 0] = qn_row
    knorm_ref[0, 0] = kn_row

    pq = _dot(nb, w_ref[:, O_MQ:O_MK])
    pk = _dot(nb, w_ref[:, O_MK:O_MV])
    pv = _dot(nb, w_ref[:, O_MV:C_WIDE])
    for p in range(MOBA_HEADS // 2):
        q_t = group(pq, 0, p).T
        v_t = group(pv, 0, p).T
        for e in range(2):
            mq_ref[0, 2 * p + e] = q_t[e * HALF:(e + 1) * HALF]
            _store_vt(mv_ref, 2 * p + e, v_t[e * HALF:(e + 1) * HALF], t)
    row = lax.broadcasted_iota(jnp.int32, (tm, 1), 0)
    blk = (si * tm + row) // MOBA_BLOCK
    blk_onehot = jnp.where(_lane((tm, LANES)) == MOBA_SEL + blk, 1.0, 0.0)
    for h in range(MOBA_HEADS):
        kb = group(pk, 0, h) + blk_onehot
        for i, piece in enumerate(_split3(relf * (_alibi_slope(DIFF_HEADS + h) * LOG2E))):
            kb = kb + piece * _onehot_lanes([MOBA_BIAS + i])
        mk_ref[0, h] = kb.astype(BF16)
    for j in range(tm // MOBA_BLOCK):
        kmean_ref[0, j] = jnp.mean(pk[j * MOBA_BLOCK:(j + 1) * MOBA_BLOCK, :], axis=0, keepdims=True)


def _layer_spec(a, l, single_buffer=False):
    zeros = (0,) * (a.ndim - 1)
    mode = dict(pipeline_mode=pl.Buffered(1)) if single_buffer else {}
    return pl.BlockSpec((None,) + a.shape[1:], lambda *_: (l,) + zeros, **mode)


def _inproj(h, pos, rel, inv_lane, g, w_wide, qn, wuq, kvn, wukv, *, l, tm, t):
    B, S, _ = h.shape
    grid = (B, S // tm)
    tok = lambda b, i: (b, i, 0)
    const2 = lambda b, i: (0, 0)
    q_out = lambda nh: (jax.ShapeDtypeStruct((B, nh, LANES, S), BF16),
                        pl.BlockSpec((1, nh, LANES, tm), lambda b, i: (b, 0, 0, i)))
    k_out = lambda nh, dt: (jax.ShapeDtypeStruct((B, nh, S, LANES), dt),
                            pl.BlockSpec((1, nh, tm, LANES), lambda b, i: (b, 0, i, 0)))
    v_out = lambda nh: (jax.ShapeDtypeStruct((B, nh, S // t, V_ROWS, t), BF16),
                        pl.BlockSpec((1, nh, tm // t, V_ROWS, t), lambda b, i: (b, 0, i, 0, 0)))
    outs = [q_out(MLA_HEADS), k_out(MLA_HEADS, BF16), v_out(MLA_HEADS),
            q_out(DIFF_HEADS), k_out(DIFF_HEADS, BF16), v_out(DIFF_HEADS),
            (jax.ShapeDtypeStruct((B, MOBA_HEADS, MOBA_HD, S), F32),
             pl.BlockSpec((1, MOBA_HEADS, MOBA_HD, tm), lambda b, i: (b, 0, 0, i))),
            k_out(MOBA_HEADS, BF16), v_out(MOBA_HEADS),
            (jax.ShapeDtypeStruct((B, S // MOBA_BLOCK, 1, MOBA_HEADS * LANES), F32),
             pl.BlockSpec((1, tm // MOBA_BLOCK, 1, MOBA_HEADS * LANES), lambda b, i: (b, i, 0, 0)))]
    outs += [(jax.ShapeDtypeStruct((B, S // tm, 1, LANES), F32),
              pl.BlockSpec((1, 1, 1, LANES), lambda b, i: (b, i, 0, 0)))] * 2
    full = lambda a: pl.BlockSpec(a.shape, const2)
    return pl.pallas_call(
        functools.partial(_inproj_kernel, tm=tm, t=t),
        grid=grid,
        in_specs=[pl.BlockSpec((1, tm, D_MODEL), tok), pl.BlockSpec((1, 1, tm), lambda b, i: (b, 0, i)),
                  pl.BlockSpec((1, tm, 1), tok), full(inv_lane)]
                 + [_layer_spec(a, l) for a in (g, w_wide, qn, wuq, kvn, wukv)],
        out_specs=[o[1] for o in outs],
        out_shape=[o[0] for o in outs],
        compiler_params=pltpu.CompilerParams(dimension_semantics=("parallel", "parallel"),
                                             vmem_limit_bytes=VMEM_LIMIT),
        name="inproj",
    )(h, pos, rel, inv_lane, g, w_wide, qn, wuq, kvn, wukv)


def _moba_gate_kernel(mq_ref, km_ref, o_ref, *, tm):
    si = pl.program_id(1)
    nblk = LANES - MOBA_SEL
    blk = lax.broadcasted_iota(jnp.int32, (nblk, tm), 0)
    blkf = blk.astype(F32)
    own = (si * tm + _lane((1, tm))) // MOBA_BLOCK
    valid = blk < own
    ones_rows = _bias_rows(MOBA_SEL - MOBA_HD, tm)
    for h in range(MOBA_HEADS):
        qt = mq_ref[0, h]
        gate = jnp.dot(km_ref[0, h], qt, precision=lax.Precision.HIGHEST, preferred_element_type=F32)
        g = jnp.where(valid, gate, NEG)
        sel = blk == own
        for _ in range(MOBA_TOPK):
            mx = jnp.max(g, axis=0, keepdims=True)
            idx = jnp.min(jnp.where(g == mx, blkf, float(nblk)), axis=0, keepdims=True)
            pick = (blkf == idx) & (mx > 0.5 * NEG)
            sel = sel | pick
            g = jnp.where(pick, NEG, g)
        selbias = jnp.where(sel, 0.0, NEG)
        o_ref[0, h] = jnp.concatenate([qt * (MOBA_HD ** -0.5 * LOG2E), ones_rows, selbias],
                                      axis=0).astype(BF16)


def _moba_gate(mq, km_pad, *, tm):
    B, H, _, S = mq.shape
    return pl.pallas_call(
        functools.partial(_moba_gate_kernel, tm=tm),
        grid=(B, S // tm),
        in_specs=[pl.BlockSpec((1, H, MOBA_HD, tm), lambda b, i: (b, 0, 0, i)),
                  pl.BlockSpec((1, H, LANES - MOBA_SEL, MOBA_HD), lambda b, i: (b, 0, 0, 0))],
        out_specs=pl.BlockSpec((1, H, LANES, tm), lambda b, i: (b, 0, 0, i)),
        out_shape=jax.ShapeDtypeStruct((B, H, LANES, S), BF16),
        compiler_params=pltpu.CompilerParams(dimension_semantics=("parallel", "parallel"),
                                             vmem_limit_bytes=VMEM_LIMIT),
        name="moba_gate",
    )(mq, km_pad)


def _causal_flash(chains, next_chains, k_ref, vt_ref, scratch, qi, plan, t, tk, early=None):
    assert t == 2 * tk
    s_a, s_b, mb_a, mb_b, m_scr, acc_scr = scratch
    nc = len(chains)
    j0, j0_late, j0_next, j0_late_next = plan
    early = tuple(range(nc)) if early is None else tuple(early)

    every, first, second = slice(0, t), slice(0, tk), slice(tk, t)

    everyone = tuple(range(nc))
    late = tuple(c for c in everyone if c not in early)

    def scores(kt, s_dst, mb_dst, qs=every, chains=chains, only=everyone):
        tiles = {}
        for c, (hh, qt) in enumerate(chains):
            if c not in only:
                continue
            if hh not in tiles:
                tiles[hh] = k_ref[0, hh, pl.ds(pl.multiple_of(kt * tk, tk), tk), :]
            s = _dot(tiles[hh], qt[:, qs])
            s_dst[c, :, qs] = s
            mb_dst[c, :, qs] = jnp.max(s, axis=0, keepdims=True)

    def softmax_pv(kt, s_src, mb_src, qs=every, diagonal=False, only=everyone):
        for c, (hh, _) in enumerate(chains):
            if c not in only:
                continue
            s = s_src[c, :, qs]
            if diagonal:
                keep = (lax.broadcasted_iota(jnp.int32, (tk, tk), 0)
                        <= lax.broadcasted_iota(jnp.int32, (tk, tk), 1))
                s = jnp.where(keep, s, NEG)
                mb = jnp.max(s, axis=0, keepdims=True)
            else:
                mb = mb_src[c, :, qs]
            m = m_scr[c, :, qs]
            m_new = jnp.maximum(m, mb)
            p = jnp.exp2(s - m_new).astype(BF16)
            acc_scr[c, :, qs] = acc_scr[c, :, qs] * jnp.exp2(m - m_new) + _dot(vt_ref[0, hh, kt], p)
            m_scr[c, :, qs] = m_new

    for c in range(nc):
        m_scr[c] = jnp.full((1, t), NEG, F32)
        acc_scr[c] = jnp.zeros((V_ROWS, t), F32)

    @pl.when(qi == 0)
    def _():
        scores(2 * j0, s_a, mb_a, only=early)
        scores(2 * j0_late, s_a, mb_a, only=late)

    def pair(j, only):
        scores(2 * j + 1, s_b, mb_b, only=only)
        softmax_pv(2 * j, s_a, mb_a, only=only)
        scores(2 * j + 2, s_a, mb_a, only=only)
        softmax_pv(2 * j + 1, s_b, mb_b, only=only)

    def far_pairs(start, stop, only):
        unroll = min(MAX_UNROLL, UNROLL_CHAIN_PAIRS // len(only))
        for width in sorted({unroll, min(unroll, 2), 1}, reverse=True):
            def body(i, carry, width=width, start=start):
                for u in range(width):
                    pair(start + width * i + u, only)
                return carry

            trips = (stop - start) // width
            lax.fori_loop(0, trips, body, 0)
            start = start + trips * width

    def own_range():
        scores(2 * qi + 1, s_b, mb_b, second)
        softmax_pv(2 * qi, s_a, mb_a, first, diagonal=True)
        softmax_pv(2 * qi, s_a, mb_a, second)
        scores(2 * j0_next, s_a, mb_a, chains=next_chains, only=early)
        scores(2 * j0_late_next, s_a, mb_a, chains=next_chains, only=late)
        softmax_pv(2 * qi + 1, s_b, mb_b, second, diagonal=True)

    one_far_pair = (j0 == qi - 1) & (j0_late == qi - 1)

    @pl.when(one_far_pair)
    def _():
        pair(qi - 1, everyone)
        own_range()

    @pl.when(jnp.logical_not(one_far_pair))
    def _():
        if late:
            far_pairs(j0, j0_late, early)
        far_pairs(j0_late, qi, everyone)
        own_range()
    outs = []
    for c in range(nc):
        acc = acc_scr[c]
        outs.append(acc[:V_DIM] * (1.0 / acc[V_ONE:V_ONE + 1]))
    return outs


def _flash_scratch(nc, t, tk):
    return ([pltpu.VMEM((nc, tk, t), F32)] * 2 + [pltpu.VMEM((nc, 1, t), F32)] * 3
            + [pltpu.VMEM((nc, V_ROWS, t), F32)])


def _first_pairs(j0_ref):
    b, p, i = pl.program_id(0), pl.program_id(1), pl.program_id(2)
    step = (b * pl.num_programs(1) + p) * pl.num_programs(2) + i
    last = pl.num_programs(0) * pl.num_programs(1) * pl.num_programs(2) - 1
    nxt = jnp.minimum(step + 1, last)
    return j0_ref[2 * step], j0_ref[2 * step + 1], j0_ref[2 * nxt], j0_ref[2 * nxt + 1]


def _mla_attn_kernel(j0_ref, q_ref, qnext_ref, k_ref, vt_ref, o_ref, *scratch, t, tk):
    heads = range(q_ref.shape[1])
    outs = _causal_flash([(hh, q_ref[0, hh]) for hh in heads], [(hh, qnext_ref[0, hh]) for hh in heads],
                         k_ref, vt_ref, scratch, pl.program_id(2), _first_pairs(j0_ref), t, tk)
    o_ref[0] = jnp.concatenate(outs, axis=0).astype(o_ref.dtype)


def _diff_attn_kernel(j0_ref, lam_ref, gain_ref, q_ref, qnext_ref, k_ref, vt_ref, o_ref, *scratch, t, tk, lam_init):
    lv = lam_ref[...]
    lam = (jnp.exp(jnp.sum(lv[0:1] * lv[1:2], axis=-1, keepdims=True))
           - jnp.exp(jnp.sum(lv[2:3] * lv[3:4], axis=-1, keepdims=True)) + lam_init)
    feat = lax.broadcasted_iota(jnp.int32, (LANES, t), 0)

    def map_chains(ref):
        chains = []
        for hh in range(ref.shape[1]):
            q = ref[0, hh]
            chains.append((hh, jnp.where(feat < DIFF_MAP2, q, jnp.zeros_like(q))))
            chains.append((hh, jnp.where(feat >= DIFF_MAP2, q, jnp.zeros_like(q))))
        return chains

    res = _causal_flash(map_chains(q_ref), map_chains(qnext_ref), k_ref, vt_ref, scratch,
                        pl.program_id(2), _first_pairs(j0_ref), t, tk, early=(2, 3))
    outs = []
    for a, b in zip(res[0::2], res[1::2]):
        o = a - lam * b
        outs.append(o * lax.rsqrt(jnp.mean(o * o, axis=0, keepdims=True) + EPS))
    o_ref[0] = (jnp.concatenate(outs, axis=0) * gain_ref[...] * (1.0 - lam_init)).astype(o_ref.dtype)


def _group_attention(kernel_fn, q, k, vt, extra=(), first_pair=None, *, t, tk, group, chains, name):
    B, H, _, S = q.shape
    if first_pair is None:
        first_pair = jnp.zeros((B, H // group, S // t, 2), jnp.int32)
    nq = S // t
    qspec = pl.BlockSpec((1, group, LANES, t), lambda b, p, i, j0: (b, p, 0, i))
    qnext = pl.BlockSpec((1, group, LANES, t), lambda b, p, i, j0: (b, p, 0, jnp.minimum(i + 1, nq - 1)))
    kspec = pl.BlockSpec((1, group, S, LANES), lambda b, p, i, j0: (b, p, 0, 0))
    vspec = pl.BlockSpec((1, group, S // tk, V_ROWS, tk), lambda b, p, i, j0: (b, p, 0, 0, 0))
    xspecs = [pl.BlockSpec(a.shape, lambda b, p, i, j0: (0, 0)) for a in extra]
    return pl.pallas_call(
        functools.partial(kernel_fn, t=t, tk=tk),
        grid_spec=pltpu.PrefetchScalarGridSpec(
            num_scalar_prefetch=1,
            grid=(B, H // group, S // t),
            in_specs=xspecs + [qspec, qnext, kspec, vspec],
            out_specs=pl.BlockSpec((1, group * V_DIM, t), lambda b, p, i, j0: (b, p, i)),
            scratch_shapes=_flash_scratch(chains, t, tk)),
        out_shape=jax.ShapeDtypeStruct((B, H * V_DIM, S), BF16),
        compiler_params=pltpu.CompilerParams(dimension_semantics=("parallel", "parallel", "arbitrary"),
                                             vmem_limit_bytes=VMEM_LIMIT),
        name=name,
    )(first_pair.reshape(-1), *extra, q, q, k, vt)


def _memkv_kernel(x_ref, g_ref, w_ref, o_ref):
    o_ref[0] = _dot(_rms(x_ref[0], g_ref[...]).astype(BF16), w_ref[...]).astype(o_ref.dtype)


def _memkv(mem, g, wkv, *, l):
    B, M, _ = mem.shape
    return pl.pallas_call(
        _memkv_kernel,
        grid=(B,),
        in_specs=[pl.BlockSpec((1, M, D_MODEL), lambda b: (b, 0, 0)), _layer_spec(g, l), _layer_spec(wkv, l)],
        out_specs=pl.BlockSpec((1, M, 2 * D_MODEL), lambda b: (b, 0, 0)),
        out_shape=jax.ShapeDtypeStruct((B, M, 2 * D_MODEL), BF16),
        compiler_params=pltpu.CompilerParams(dimension_semantics=("parallel",), vmem_limit_bytes=VMEM_LIMIT),
        name="memkv",
    )(mem, g, wkv)


def _post_kernel(h_ref, oa_ref, ob_ref, oc_ref, wout_ref, g_ref, wq_ref, kv_ref, wo_ref,
                 g2_ref, w1_ref, w2_ref, gf_ref, o_ref, *, final, chunk):
    na, nb = oa_ref.shape[1], ob_ref.shape[1]
    h1 = (h_ref[0] + _dot_tn(oa_ref[0], wout_ref[0:na]) + _dot_tn(ob_ref[0], wout_ref[na:na + nb])
          + _dot_tn(oc_ref[0], wout_ref[na + nb:]))
    n = _rms(h1, g_ref[...]).astype(BF16)
    q = (_dot(n, wq_ref[...]) * (CROSS_HD ** -0.5 * LOG2E)).astype(BF16)
    ctx = []
    for h in range(CROSS_HEADS):
        sl = slice(h * CROSS_HD, (h + 1) * CROSS_HD)
        s = _dot_nt(q[:, sl], kv_ref[0, :, sl])
        p = jnp.exp2(s - jnp.max(s, axis=-1, keepdims=True))
        l = jnp.sum(p, axis=-1, keepdims=True)
        c = _dot(p.astype(BF16), kv_ref[0, :, D_MODEL + h * CROSS_HD:D_MODEL + (h + 1) * CROSS_HD])
        ctx.append((c * (1.0 / l)).astype(BF16))
    h2 = h1 + _dot(jnp.concatenate(ctx, axis=-1), wo_ref[...])
    n = _rms(h2, g2_ref[...]).astype(BF16)
    acc = h2
    for c in range(D_FF // chunk):
        a = jnp.maximum(_dot(n, w1_ref[:, c * chunk:(c + 1) * chunk]), 0.0)
        acc = acc + _dot((a * a).astype(BF16), w2_ref[c * chunk:(c + 1) * chunk, :])
    o_ref[0] = _rms(acc, gf_ref[...]) if final else acc


def _post(h, oa, ob, oc, wout, g, wq, memkv, wo, g2, w1, w2, gf, *, l, tm, final):
    B, S, _ = h.shape
    tok = lambda b, i: (b, i, 0)
    const2 = lambda b, i: (0, 0)
    small = lambda a: pl.BlockSpec(a.shape, const2)
    resident = lambda a: _layer_spec(a, l, single_buffer=True)
    return pl.pallas_call(
        functools.partial(_post_kernel, final=final, chunk=D_MODEL),
        grid=(B, S // tm),
        in_specs=[pl.BlockSpec((1, tm, D_MODEL), tok)]
                 + [pl.BlockSpec((1, o.shape[1], tm), lambda b, i: (b, 0, i)) for o in (oa, ob, oc)] + [
                  resident(wout), _layer_spec(g, l), resident(wq),
                  pl.BlockSpec((1,) + memkv.shape[1:], lambda b, i: (b, 0, 0)), resident(wo),
                  _layer_spec(g2, l), resident(w1), resident(w2), small(gf)],
        out_specs=pl.BlockSpec((1, tm, D_MODEL), tok),
        out_shape=jax.ShapeDtypeStruct(h.shape, F32),
        compiler_params=pltpu.CompilerParams(dimension_semantics=("parallel", "parallel"),
                                             vmem_limit_bytes=VMEM_LIMIT),
        name="post",
    )(h, oa, ob, oc, wout, g, wq, memkv, wo, g2, w1, w2, gf)


def _rot_pairs(w):
    half = w.shape[-1] // 2
    return jnp.concatenate([-w[..., half:], w[..., :half]], axis=-1)


def _pad_lanes(w, lo, width=LANES):
    pad = [(0, 0)] * (w.ndim - 1) + [(lo, width - lo - w.shape[-1])]
    return jnp.pad(w, pad)


def _head_groups(w, nheads, hd, lo=0, width=LANES):
    K = w.shape[0]
    return _pad_lanes(w.reshape(K, nheads, hd), lo, width).reshape(K, nheads * width)


def _widen_w_in(w):
    cuts = np.cumsum([0, MLA_Q_RANK, MLA_KV_RANK, MLA_ROPE, C_DIFF, C_DIFF, C_DIFF, C_MOBA, C_MOBA, C_MOBA])
    cq, ckv, kr, dq, dk, dv, mq, mk, mv = [w[:, int(a):int(b)] for a, b in zip(cuts[:-1], cuts[1:])]
    two_maps = lambda x: _head_groups(x, 2 * DIFF_HEADS, DIFF_HD, 0, HALF)
    out = jnp.concatenate([
        cq, ckv, _pad_lanes(kr, MLA_NOPE), _pad_lanes(_rot_pairs(kr), MLA_NOPE),
        dq, two_maps(dk), dv, mq, _head_groups(mk, MOBA_HEADS, MOBA_HD), mv], axis=1).astype(BF16)
    assert out.shape[1] == C_WIDE
    return out


def _widen_w_uq(w):
    K = w.shape[0]
    e = w.reshape(K, MLA_HEADS, MLA_NOPE + MLA_ROPE)
    nope = e[..., :MLA_NOPE].reshape(K, MLA_HEADS * MLA_NOPE)
    rope = _pad_lanes(e[..., MLA_NOPE:].reshape(K, MLA_HEADS * MLA_ROPE), 0, 2 * LANES)
    rot = _pad_lanes(_rot_pairs(e[..., MLA_NOPE:]).reshape(K, MLA_HEADS * MLA_ROPE), 0, 2 * LANES)
    out = jnp.concatenate([nope, rope, rot], axis=1).astype(BF16)
    assert out.shape[1] == UQ_COLS
    return out


def _widen_w_ukv(w):
    K = w.shape[0]
    e = w.reshape(K, MLA_HEADS, MLA_NOPE + MLA_V)
    kn = _pad_lanes(e[..., :MLA_NOPE], 0).reshape(K, MLA_HEADS * LANES)
    vv = e[..., MLA_NOPE:].reshape(K, MLA_HEADS * MLA_V)
    out = jnp.concatenate([kn, vv], axis=1).astype(BF16)
    assert out.shape[1] == UKV_COLS
    return out


def _diff_first_pairs(qnorm, knorm, rel, t, tk, group):
    B, nq = qnorm.shape[:2]
    nm = 2 * DIFF_HEADS
    qn = qnorm[:, :, 0, :nm]
    kn = jnp.max(knorm[:, :, 0, :nm], axis=1)
    spread = NORM_MARGIN * 2.0 * qn * kn[:, None, :]
    relf = rel[..., 0].astype(F32)
    dist = relf[:, ::t][:, :, None] - relf[:, tk - 1::tk][:, None, :]
    c = jnp.asarray([_alibi_slope(i // 2) * LOG2E for i in range(nm)], F32)
    dead = c[None, None, :, None] * dist[:, :, None, :] > spread[..., None] + UNDERFLOW_BITS
    tiles = jnp.sum(dead.astype(jnp.int32), axis=-1)
    assert group == 2
    per_head = jnp.min(tiles.reshape(B, nq, DIFF_HEADS // 2, 2, 2), axis=-1) // 2
    late = per_head[..., 0]
    early = jnp.minimum(per_head[..., 1], late)
    return jnp.stack([early, late], axis=-1).transpose(0, 2, 1, 3)


def kernel(x, mem, positions, attn_norm, w_in, mla_q_norm, mla_w_uq, mla_kv_norm, mla_w_ukv, diff_lambda_q1, diff_lambda_k1, diff_lambda_q2, diff_lambda_k2, diff_sub_norm, w_out, cross_norm, mem_norm, cross_wq, cross_wkv, cross_wo, mlp_norm, mlp_w1, mlp_w2, final_norm):
    B, S, _ = x.shape
    depth = w_in.shape[0]
    tm = 512
    t = 512
    tk = t // 2
    assert S % tm == 0 and tm == t and tk % MOBA_BLOCK == 0 and S // MOBA_BLOCK <= LANES - MOBA_SEL

    pos = positions.astype(jnp.int32)[:, None, :]
    rel = (positions - positions[:, :1]).astype(jnp.int32)[..., None]
    half = MLA_ROPE // 2
    inv_lane = (ROPE_THETA ** (-jnp.arange(half, dtype=F32) / half))[:, None]
    rows = lambda v: v.astype(F32)[:, None, :]
    wout_b, wq_b, wkv_b, wo_b, w1_b, w2_b = (w.astype(BF16) for w in (w_out, cross_wq, cross_wkv, cross_wo,
                                                                       mlp_w1, mlp_w2))
    w_wide, wuq, wukv = jax.vmap(_widen_w_in)(w_in), jax.vmap(_widen_w_uq)(mla_w_uq), jax.vmap(_widen_w_ukv)(mla_w_ukv)
    g_attn, g_q, g_kv, g_cross, g_mem, g_mlp = (rows(v) for v in (attn_norm, mla_q_norm, mla_kv_norm, cross_norm,
                                                                  mem_norm, mlp_norm))

    h = x
    for l in range(depth):
        outs = _inproj(h, pos, rel, inv_lane, g_attn, w_wide, g_q, wuq, g_kv, wukv, l=l, tm=tm, t=tk)
        qa, ka, va, qd, kd, vd, mq, mk, mv, kmean, qnorm, knorm = outs
        km = kmean.reshape(B, S // MOBA_BLOCK, MOBA_HEADS, LANES)[..., :MOBA_HD].transpose(0, 2, 1, 3)
        km_pad = jnp.pad(km, ((0, 0), (0, 0), (0, LANES - MOBA_SEL - S // MOBA_BLOCK), (0, 0)))
        mq_aug = _moba_gate(mq, km_pad, tm=math.gcd(S, GATE_TILE))

        attn = functools.partial(_group_attention, t=t, tk=tk)
        o_a = attn(_mla_attn_kernel, qa, ka, va, group=3, chains=3, name="mla_attn")
        lam_rows = jnp.stack([diff_lambda_q1[l], diff_lambda_k1[l], diff_lambda_q2[l], diff_lambda_k2[l]])
        lam_rows = jnp.pad(lam_rows.astype(F32), ((0, 4), (0, LANES - DIFF_HD)))
        gain = jnp.tile(diff_sub_norm[l].astype(F32), DIFF_GROUP)[:, None]
        lam_init = 0.8 - 0.6 * math.exp(-0.3 * l)
        o_b = attn(functools.partial(_diff_attn_kernel, lam_init=lam_init), qd, kd, vd,
                   extra=(lam_rows, gain), first_pair=_diff_first_pairs(qnorm, knorm, rel, t, tk, DIFF_GROUP),
                   group=DIFF_GROUP, chains=2 * DIFF_GROUP, name="diff_attn")
        o_c = attn(_mla_attn_kernel, mq_aug, mk, mv, group=4, chains=4, name="moba_attn")

        memkv = _memkv(mem, g_mem, wkv_b, l=l)
        h = _post(h, o_a, o_b, o_c, wout_b, g_cross, wq_b, memkv, wo_b, g_mlp, w1_b, w2_b,
                  final_norm.astype(F32)[None, :], l=l, tm=tm, final=(l == depth - 1))
    return h
```

```python
import functools
import math

import jax
import jax.numpy as jnp
import numpy as np
from jax import lax
from jax.experimental import pallas as pl
from jax.experimental.pallas import tpu as pltpu

D_MODEL = 1024
MLA_HEADS = 6
MLA_NOPE = 64
MLA_ROPE = 32
MLA_V = 64
MLA_Q_RANK = 256
MLA_KV_RANK = 128
ROPE_THETA = 10000.0
DIFF_HEADS = 6
DIFF_HD = 32
MOBA_HEADS = 4
MOBA_HD = 64
MOBA_BLOCK = 256
MOBA_TOPK = 3
CROSS_HEADS = 4
CROSS_HD = D_MODEL // CROSS_HEADS
D_FF = 4 * D_MODEL
EPS = 1e-6
NEG = -1e30
N_ALIBI = DIFF_HEADS + MOBA_HEADS
C_DIFF = DIFF_HEADS * 2 * DIFF_HD
C_MOBA = MOBA_HEADS * MOBA_HD

LANES = 128
HALF = LANES // 2
LOG2E = 1.4426950408889634
VMEM_LIMIT = 56 * 1024 * 1024

F32 = jnp.float32
BF16 = jnp.bfloat16

V_DIM = 64
V_ONE = V_DIM
V_ROWS = 80
DIFF_MAP2 = HALF
DIFF_BIAS = DIFF_HD
MOBA_BIAS = MOBA_HD
MOBA_SEL = 96
N_SPLIT = 3
UNROLL_CHAIN_PAIRS = 12
GATE_TILE = 2048
MAX_UNROLL = 4
DIFF_GROUP = 2
UNDERFLOW_BITS = 152.0
NORM_MARGIN = 1.02

O_CQ = 0
O_CKV = O_CQ + MLA_Q_RANK
O_KR = O_CKV + MLA_KV_RANK
O_KRS = O_KR + LANES
O_DQ = O_KRS + LANES
O_DK = O_DQ + C_DIFF
O_DV = O_DK + DIFF_HEADS * LANES
O_MQ = O_DV + C_DIFF
O_MK = O_MQ + C_MOBA
O_MV = O_MK + MOBA_HEADS * LANES
C_WIDE = O_MV + C_MOBA
UQ_NOPE = 0
UQ_ROPE = MLA_HEADS * MLA_NOPE
UQ_ROT = UQ_ROPE + 2 * LANES
UQ_COLS = UQ_ROT + 2 * LANES
UKV_V = MLA_HEADS * LANES
UKV_COLS = UKV_V + MLA_HEADS * MLA_V


def _alibi_slope(h):
    return 2.0 ** (-8.0 * (h + 1) / N_ALIBI)


def _rms(x, g):
    return x * lax.rsqrt(jnp.mean(x * x, axis=-1, keepdims=True) + EPS) * g


def _dot(a, b):
    return jnp.dot(a, b, preferred_element_type=F32)


def _dot_nt(a, b):
    return lax.dot_general(a, b, (((1,), (1,)), ((), ())), preferred_element_type=F32)


def _dot_tn(a, b):
    return lax.dot_general(a, b, (((0,), (0,)), ((), ())), preferred_element_type=F32)


def _lane(shape):
    return lax.broadcasted_iota(jnp.int32, shape, len(shape) - 1)


def _onehot_lanes(lanes, value=1.0):
    l = _lane((1, LANES))
    out = jnp.zeros((1, LANES), F32)
    for i in lanes:
        out = jnp.where(l == i, value, out)
    return out


def _split3(x):
    hi = x.astype(BF16).astype(F32)
    r = x - hi
    mid = r.astype(BF16).astype(F32)
    lo = (r - mid).astype(BF16).astype(F32)
    return hi, mid, lo


def _store_vt(ref, h, vt, t):
    tm = vt.shape[1]
    tail = jnp.where(lax.broadcasted_iota(jnp.int32, (V_ROWS - V_DIM, tm), 0) == 0, 1.0, 0.0)
    x = jnp.concatenate([vt, tail], axis=0).astype(BF16)
    for j in range(tm // t):
        ref[0, h, j] = x[:, j * t:(j + 1) * t]


def _bias_rows(n, tm):
    return jnp.where(lax.broadcasted_iota(jnp.int32, (n, tm), 0) < N_SPLIT, 1.0, 0.0)


def _inproj_kernel(h_ref, pos_ref, rel_ref, inv_ref, g_ref, w_ref, qn_ref, wuq_ref, kvn_ref, wukv_ref,
                   qa_ref, ka_ref, va_ref, qd_ref, kd_ref, vd_ref, mq_ref, mk_ref, mv_ref, kmean_ref,
                   qnorm_ref, knorm_ref, *, tm, t):
    si = pl.program_id(1)
    nb = _rms(h_ref[0], g_ref[...]).astype(BF16)
    group = lambda x, o, i: x[:, o + i * LANES:o + (i + 1) * LANES]

    pm = _dot(nb, w_ref[:, O_CQ:O_DQ])
    cq = _rms(pm[:, O_CQ:O_CKV], qn_ref[...]).astype(BF16)
    ckv = _rms(pm[:, O_CKV:O_KR], kvn_ref[...]).astype(BF16)
    q2 = _dot(cq, wuq_ref[...])
    kv2 = _dot(ckv, wukv_ref[...])
    ang = inv_ref[...] * pos_ref[0].astype(F32)
    reps = LANES // ang.shape[0]
    cs = jnp.concatenate([jnp.cos(ang)] * reps, axis=0).T
    sn = jnp.concatenate([jnp.sin(ang)] * reps, axis=0).T
    krope = pm[:, O_KR:O_KRS] * cs + pm[:, O_KRS:O_DQ] * sn
    qscale = (MLA_NOPE + MLA_ROPE) ** -0.5 * LOG2E
    rope_t = [(group(q2, UQ_ROPE, i) * cs + group(q2, UQ_ROT, i) * sn).T for i in range(2)]
    pad_rows = jnp.zeros((LANES - MLA_NOPE - MLA_ROPE, tm), F32)
    for p in range(MLA_HEADS // 2):
        nope_t = group(q2, UQ_NOPE, p).T
        v_t = group(kv2, UKV_V, p).T
        for e in range(2):
            h = 2 * p + e
            r = (h % 4) * MLA_ROPE
            qa = jnp.concatenate([nope_t[e * HALF:(e + 1) * HALF], rope_t[h // 4][r:r + MLA_ROPE], pad_rows], axis=0)
            qa_ref[0, h] = (qa * qscale).astype(BF16)
            _store_vt(va_ref, h, v_t[e * HALF:(e + 1) * HALF], t)
    for h in range(MLA_HEADS):
        ka_ref[0, h] = (group(kv2, 0, h) + krope).astype(BF16)

    relf = rel_ref[0].astype(F32)

    pq = _dot(nb, w_ref[:, O_DQ:O_DK])
    pk = _dot(nb, w_ref[:, O_DK:O_DV])
    pv = _dot(nb, w_ref[:, O_DV:O_MQ])
    dscale = DIFF_HD ** -0.5 * LOG2E
    ones_rows = _bias_rows(DIFF_MAP2 - DIFF_HD, tm)
    lane1 = _lane((1, LANES))
    qn_row = jnp.zeros((1, LANES), F32)
    kn_row = jnp.zeros((1, LANES), F32)
    for p in range(DIFF_HEADS // 2):
        q_t = group(pq, 0, p).T * dscale
        v_t = group(pv, 0, p).T
        for e in range(2):
            o = e * HALF
            qd_ref[0, 2 * p + e] = jnp.concatenate(
                [q_t[o:o + DIFF_HD], ones_rows, q_t[o + DIFF_HD:o + HALF], ones_rows], axis=0).astype(BF16)
            _store_vt(vd_ref, 2 * p + e, v_t[o:o + HALF], t)
            for mp in range(2):
                qm = q_t[o + mp * DIFF_HD:o + (mp + 1) * DIFF_HD]
                qsq = jnp.max(jnp.sum(qm * qm, axis=0, keepdims=True), axis=1, keepdims=True)
                qn_row = jnp.where(lane1 == 2 * (2 * p + e) + mp, jnp.sqrt(qsq), qn_row)
    in_map1 = _lane((tm, LANES)) < DIFF_MAP2
    for h in range(DIFF_HEADS):
        kb = group(pk, 0, h)
        for mp in range(2):
            mine = in_map1 if mp == 0 else jnp.logical_not(in_map1)
            ksq = jnp.sum(jnp.where(mine, kb * kb, 0.0), axis=1, keepdims=True)
            kn_row = jnp.where(lane1 == 2 * h + mp, jnp.sqrt(jnp.max(ksq, axis=0, keepdims=True)), kn_row)
        for i, piece in enumerate(_split3(relf * (_alibi_slope(h) * LOG2E))):
            kb = kb + piece * _onehot_lanes([DIFF_BIAS + i, DIFF_MAP2 + DIFF_BIAS + i])
        kd_ref[0, h] = kb.astype(BF16)
    qnorm_ref[0, 0] = qn_row
    knorm_ref[0, 0] = kn_row

    pq = _dot(nb, w_ref[:, O_MQ:O_MK])
    pk = _dot(nb, w_ref[:, O_MK:O_MV])
    pv = _dot(nb, w_ref[:, O_MV:C_WIDE])
    for p in range(MOBA_HEADS // 2):
        q_t = group(pq, 0, p).T
        v_t = group(pv, 0, p).T
        for e in range(2):
            mq_ref[0, 2 * p + e] = q_t[e * HALF:(e + 1) * HALF]
            _store_vt(mv_ref, 2 * p + e, v_t[e * HALF:(e + 1) * HALF], t)
    row = lax.broadcasted_iota(jnp.int32, (tm, 1), 0)
    blk = (si * tm + row) // MOBA_BLOCK
    blk_onehot = jnp.where(_lane((tm, LANES)) == MOBA_SEL + blk, 1.0, 0.0)
    for h in range(MOBA_HEADS):
        kb = group(pk, 0, h) + blk_onehot
        for i, piece in enumerate(_split3(relf * (_alibi_slope(DIFF_HEADS + h) * LOG2E))):
            kb = kb + piece * _onehot_lanes([MOBA_BIAS + i])
        mk_ref[0, h] = kb.astype(BF16)
    for j in range(tm // MOBA_BLOCK):
        kmean_ref[0, j] = jnp.mean(pk[j * MOBA_BLOCK:(j + 1) * MOBA_BLOCK, :], axis=0, keepdims=True)


def _layer_spec(a, l, single_buffer=False):
    zeros = (0,) * (a.ndim - 1)
    mode = dict(pipeline_mode=pl.Buffered(1)) if single_buffer else {}
    return pl.BlockSpec((None,) + a.shape[1:], lambda *_: (l,) + zeros, **mode)


def _inproj(h, pos, rel, inv_lane, g, w_wide, qn, wuq, kvn, wukv, *, l, tm, t):
    B, S, _ = h.shape
    grid = (B, S // tm)
    tok = lambda b, i: (b, i, 0)
    const2 = lambda b, i: (0, 0)
    q_out = lambda nh: (jax.ShapeDtypeStruct((B, nh, LANES, S), BF16),
                        pl.BlockSpec((1, nh, LANES, tm), lambda b, i: (b, 0, 0, i)))
    k_out = lambda nh, dt: (jax.ShapeDtypeStruct((B, nh, S, LANES), dt),
                            pl.BlockSpec((1, nh, tm, LANES), lambda b, i: (b, 0, i, 0)))
    v_out = lambda nh: (jax.ShapeDtypeStruct((B, nh, S // t, V_ROWS, t), BF16),
                        pl.BlockSpec((1, nh, tm // t, V_ROWS, t), lambda b, i: (b, 0, i, 0, 0)))
    outs = [q_out(MLA_HEADS), k_out(MLA_HEADS, BF16), v_out(MLA_HEADS),
            q_out(DIFF_HEADS), k_out(DIFF_HEADS, BF16), v_out(DIFF_HEADS),
            (jax.ShapeDtypeStruct((B, MOBA_HEADS, MOBA_HD, S), F32),
             pl.BlockSpec((1, MOBA_HEADS, MOBA_HD, tm), lambda b, i: (b, 0, 0, i))),
            k_out(MOBA_HEADS, BF16), v_out(MOBA_HEADS),
            (jax.ShapeDtypeStruct((B, S // MOBA_BLOCK, 1, MOBA_HEADS * LANES), F32),
             pl.BlockSpec((1, tm // MOBA_BLOCK, 1, MOBA_HEADS * LANES), lambda b, i: (b, i, 0, 0)))]
    outs += [(jax.ShapeDtypeStruct((B, S // tm, 1, LANES), F32),
              pl.BlockSpec((1, 1, 1, LANES), lambda b, i: (b, i, 0, 0)))] * 2
    full = lambda a: pl.BlockSpec(a.shape, const2)
    return pl.pallas_call(
        functools.partial(_inproj_kernel, tm=tm, t=t),
        grid=grid,
        in_specs=[pl.BlockSpec((1, tm, D_MODEL), tok), pl.BlockSpec((1, 1, tm), lambda b, i: (b, 0, i)),
                  pl.BlockSpec((1, tm, 1), tok), full(inv_lane)]
                 + [_layer_spec(a, l) for a in (g, w_wide, qn, wuq, kvn, wukv)],
        out_specs=[o[1] for o in outs],
        out_shape=[o[0] for o in outs],
        compiler_params=pltpu.CompilerParams(dimension_semantics=("parallel", "parallel"),
                                             vmem_limit_bytes=VMEM_LIMIT),
        name="inproj",
    )(h, pos, rel, inv_lane, g, w_wide, qn, wuq, kvn, wukv)


def _moba_gate_kernel(mq_ref, km_ref, o_ref, *, tm):
    si = pl.program_id(1)
    nblk = LANES - MOBA_SEL
    blk = lax.broadcasted_iota(jnp.int32, (nblk, tm), 0)
    blkf = blk.astype(F32)
    own = (si * tm + _lane((1, tm))) // MOBA_BLOCK
    valid = blk < own
    ones_rows = _bias_rows(MOBA_SEL - MOBA_HD, tm)
    for h in range(MOBA_HEADS):
        qt = mq_ref[0, h]
        gate = jnp.dot(km_ref[0, h], qt, precision=lax.Precision.HIGHEST, preferred_element_type=F32)
        g = jnp.where(valid, gate, NEG)
        sel = blk == own
        for _ in range(MOBA_TOPK):
            mx = jnp.max(g, axis=0, keepdims=True)
            idx = jnp.min(jnp.where(g == mx, blkf, float(nblk)), axis=0, keepdims=True)
            pick = (blkf == idx) & (mx > 0.5 * NEG)
            sel = sel | pick
            g = jnp.where(pick, NEG, g)
        selbias = jnp.where(sel, 0.0, NEG)
        o_ref[0, h] = jnp.concatenate([qt * (MOBA_HD ** -0.5 * LOG2E), ones_rows, selbias],
                                      axis=0).astype(BF16)


def _moba_gate(mq, km_pad, *, tm):
    B, H, _, S = mq.shape
    return pl.pallas_call(
        functools.partial(_moba_gate_kernel, tm=tm),
        grid=(B, S // tm),
        in_specs=[pl.BlockSpec((1, H, MOBA_HD, tm), lambda b, i: (b, 0, 0, i)),
                  pl.BlockSpec((1, H, LANES - MOBA_SEL, MOBA_HD), lambda b, i: (b, 0, 0, 0))],
        out_specs=pl.BlockSpec((1, H, LANES, tm), lambda b, i: (b, 0, 0, i)),
        out_shape=jax.ShapeDtypeStruct((B, H, LANES, S), BF16),
        compiler_params=pltpu.CompilerParams(dimension_semantics=("parallel", "parallel"),
                                             vmem_limit_bytes=VMEM_LIMIT),
        name="moba_gate",
    )(mq, km_pad)


def _causal_flash(chains, next_chains, k_ref, vt_ref, scratch, qi, plan, t, tk, early=None):
    assert t == 2 * tk
    s_a, s_b, mb_a, mb_b, m_scr, acc_scr = scratch
    nc = len(chains)
    j0, j0_late, j0_next, j0_late_next = plan
    early = tuple(range(nc)) if early is None else tuple(early)

    every, first, second = slice(0, t), slice(0, tk), slice(tk, t)

    everyone = tuple(range(nc))
    late = tuple(c for c in everyone if c not in early)

    def scores(kt, s_dst, mb_dst, qs=every, chains=chains, only=everyone):
        tiles = {}
        for c, (hh, qt) in enumerate(chains):
            if c not in only:
                continue
            if hh not in tiles:
                tiles[hh] = k_ref[0, hh, pl.ds(pl.multiple_of(kt * tk, tk), tk), :]
            s = _dot(tiles[hh], qt[:, qs])
            s_dst[c, :, qs] = s
            mb_dst[c, :, qs] = jnp.max(s, axis=0, keepdims=True)

    def softmax_pv(kt, s_src, mb_src, qs=every, diagonal=False, only=everyone):
        for c, (hh, _) in enumerate(chains):
            if c not in only:
                continue
            s = s_src[c, :, qs]
            if diagonal:
                keep = (lax.broadcasted_iota(jnp.int32, (tk, tk), 0)
                        <= lax.broadcasted_iota(jnp.int32, (tk, tk), 1))
                s = jnp.where(keep, s, NEG)
                mb = jnp.max(s, axis=0, keepdims=True)
            else:
                mb = mb_src[c, :, qs]
            m = m_scr[c, :, qs]
            m_new = jnp.maximum(m, mb)
            p = jnp.exp2(s - m_new).astype(BF16)
            acc_scr[c, :, qs] = acc_scr[c, :, qs] * jnp.exp2(m - m_new) + _dot(vt_ref[0, hh, kt], p)
            m_scr[c, :, qs] = m_new

    for c in range(nc):
        m_scr[c] = jnp.full((1, t), NEG, F32)
        acc_scr[c] = jnp.zeros((V_ROWS, t), F32)

    @pl.when(qi == 0)
    def _():
        scores(2 * j0, s_a, mb_a, only=early)
        scores(2 * j0_late, s_a, mb_a, only=late)

    def far_pairs(start, stop, only):
        def pair(j):
            scores(2 * j + 1, s_b, mb_b, only=only)
            softmax_pv(2 * j, s_a, mb_a, only=only)
            scores(2 * j + 2, s_a, mb_a, only=only)
            softmax_pv(2 * j + 1, s_b, mb_b, only=only)

        unroll = min(MAX_UNROLL, UNROLL_CHAIN_PAIRS // len(only))
        for width in sorted({unroll, min(unroll, 2), 1}, reverse=True):
            def body(i, carry, width=width, start=start):
                for u in range(width):
                    pair(start + width * i + u)
                return carry

            trips = (stop - start) // width
            lax.fori_loop(0, trips, body, 0)
            start = start + trips * width

    if late:
        far_pairs(j0, j0_late, early)
    far_pairs(j0_late, qi, everyone)
    scores(2 * qi + 1, s_b, mb_b, second)
    softmax_pv(2 * qi, s_a, mb_a, first, diagonal=True)
    softmax_pv(2 * qi, s_a, mb_a, second)
    scores(2 * j0_next, s_a, mb_a, chains=next_chains, only=early)
    scores(2 * j0_late_next, s_a, mb_a, chains=next_chains, only=late)
    softmax_pv(2 * qi + 1, s_b, mb_b, second, diagonal=True)
    outs = []
    for c in range(nc):
        acc = acc_scr[c]
        outs.append(acc[:V_DIM] * (1.0 / acc[V_ONE:V_ONE + 1]))
    return outs


def _flash_scratch(nc, t, tk):
    return ([pltpu.VMEM((nc, tk, t), F32)] * 2 + [pltpu.VMEM((nc, 1, t), F32)] * 3
            + [pltpu.VMEM((nc, V_ROWS, t), F32)])


def _first_pairs(j0_ref):
    b, p, i = pl.program_id(0), pl.program_id(1), pl.program_id(2)
    step = (b * pl.num_programs(1) + p) * pl.num_programs(2) + i
    last = pl.num_programs(0) * pl.num_programs(1) * pl.num_programs(2) - 1
    nxt = jnp.minimum(step + 1, last)
    return j0_ref[2 * step], j0_ref[2 * step + 1], j0_ref[2 * nxt], j0_ref[2 * nxt + 1]


def _mla_attn_kernel(j0_ref, q_ref, qnext_ref, k_ref, vt_ref, o_ref, *scratch, t, tk):
    heads = range(q_ref.shape[1])
    outs = _causal_flash([(hh, q_ref[0, hh]) for hh in heads], [(hh, qnext_ref[0, hh]) for hh in heads],
                         k_ref, vt_ref, scratch, pl.program_id(2), _first_pairs(j0_ref), t, tk)
    o_ref[0] = jnp.concatenate(outs, axis=0).astype(o_ref.dtype)


def _diff_attn_kernel(j0_ref, lam_ref, gain_ref, q_ref, qnext_ref, k_ref, vt_ref, o_ref, *scratch, t, tk, lam_init):
    lv = lam_ref[...]
    lam = (jnp.exp(jnp.sum(lv[0:1] * lv[1:2], axis=-1, keepdims=True))
           - jnp.exp(jnp.sum(lv[2:3] * lv[3:4], axis=-1, keepdims=True)) + lam_init)
    feat = lax.broadcasted_iota(jnp.int32, (LANES, t), 0)

    def map_chains(ref):
        chains = []
        for hh in range(ref.shape[1]):
            q = ref[0, hh]
            chains.append((hh, jnp.where(feat < DIFF_MAP2, q, jnp.zeros_like(q))))
            chains.append((hh, jnp.where(feat >= DIFF_MAP2, q, jnp.zeros_like(q))))
        return chains

    res = _causal_flash(map_chains(q_ref), map_chains(qnext_ref), k_ref, vt_ref, scratch,
                        pl.program_id(2), _first_pairs(j0_ref), t, tk, early=(2, 3))
    outs = []
    for a, b in zip(res[0::2], res[1::2]):
        o = a - lam * b
        outs.append(o * lax.rsqrt(jnp.mean(o * o, axis=0, keepdims=True) + EPS))
    o_ref[0] = (jnp.concatenate(outs, axis=0) * gain_ref[...] * (1.0 - lam_init)).astype(o_ref.dtype)


def _group_attention(kernel_fn, q, k, vt, extra=(), first_pair=None, *, t, tk, group, chains, name):
    B, H, _, S = q.shape
    if first_pair is None:
        first_pair = jnp.zeros((B, H // group, S // t, 2), jnp.int32)
    nq = S // t
    qspec = pl.BlockSpec((1, group, LANES, t), lambda b, p, i, j0: (b, p, 0, i))
    qnext = pl.BlockSpec((1, group, LANES, t), lambda b, p, i, j0: (b, p, 0, jnp.minimum(i + 1, nq - 1)))
    kspec = pl.BlockSpec((1, group, S, LANES), lambda b, p, i, j0: (b, p, 0, 0))
    vspec = pl.BlockSpec((1, group, S // tk, V_ROWS, tk), lambda b, p, i, j0: (b, p, 0, 0, 0))
    xspecs = [pl.BlockSpec(a.shape, lambda b, p, i, j0: (0, 0)) for a in extra]
    return pl.pallas_call(
        functools.partial(kernel_fn, t=t, tk=tk),
        grid_spec=pltpu.PrefetchScalarGridSpec(
            num_scalar_prefetch=1,
            grid=(B, H // group, S // t),
            in_specs=xspecs + [qspec, qnext, kspec, vspec],
            out_specs=pl.BlockSpec((1, group * V_DIM, t), lambda b, p, i, j0: (b, p, i)),
            scratch_shapes=_flash_scratch(chains, t, tk)),
        out_shape=jax.ShapeDtypeStruct((B, H * V_DIM, S), BF16),
        compiler_params=pltpu.CompilerParams(dimension_semantics=("parallel", "parallel", "arbitrary"),
                                             vmem_limit_bytes=VMEM_LIMIT),
        name=name,
    )(first_pair.reshape(-1), *extra, q, q, k, vt)


def _memkv_kernel(x_ref, g_ref, w_ref, o_ref):
    o_ref[0] = _dot(_rms(x_ref[0], g_ref[...]).astype(BF16), w_ref[...]).astype(o_ref.dtype)


def _memkv(mem, g, wkv, *, l):
    B, M, _ = mem.shape
    return pl.pallas_call(
        _memkv_kernel,
        grid=(B,),
        in_specs=[pl.BlockSpec((1, M, D_MODEL), lambda b: (b, 0, 0)), _layer_spec(g, l), _layer_spec(wkv, l)],
        out_specs=pl.BlockSpec((1, M, 2 * D_MODEL), lambda b: (b, 0, 0)),
        out_shape=jax.ShapeDtypeStruct((B, M, 2 * D_MODEL), BF16),
        compiler_params=pltpu.CompilerParams(dimension_semantics=("parallel",), vmem_limit_bytes=VMEM_LIMIT),
        name="memkv",
    )(mem, g, wkv)


def _post_kernel(h_ref, oa_ref, ob_ref, oc_ref, wout_ref, g_ref, wq_ref, mem_ref, gmem_ref, wkv_ref, wo_ref,
                 g2_ref, w1_ref, w2_ref, gf_ref, o_ref, kv_scr, *, final, chunk):
    @pl.when(pl.program_id(1) == 0)
    def _():
        kv_scr[...] = _dot(_rms(mem_ref[0], gmem_ref[...]).astype(BF16), wkv_ref[...]).astype(BF16)

    na, nb = oa_ref.shape[1], ob_ref.shape[1]
    h1 = (h_ref[0] + _dot_tn(oa_ref[0], wout_ref[0:na]) + _dot_tn(ob_ref[0], wout_ref[na:na + nb])
          + _dot_tn(oc_ref[0], wout_ref[na + nb:]))
    n = _rms(h1, g_ref[...]).astype(BF16)
    q = (_dot(n, wq_ref[...]) * (CROSS_HD ** -0.5 * LOG2E)).astype(BF16)
    ctx = []
    for h in range(CROSS_HEADS):
        sl = slice(h * CROSS_HD, (h + 1) * CROSS_HD)
        s = _dot_nt(q[:, sl], kv_scr[:, sl])
        p = jnp.exp2(s - jnp.max(s, axis=-1, keepdims=True))
        l = jnp.sum(p, axis=-1, keepdims=True)
        c = _dot(p.astype(BF16), kv_scr[:, D_MODEL + h * CROSS_HD:D_MODEL + (h + 1) * CROSS_HD])
        ctx.append((c * (1.0 / l)).astype(BF16))
    h2 = h1 + _dot(jnp.concatenate(ctx, axis=-1), wo_ref[...])
    n = _rms(h2, g2_ref[...]).astype(BF16)
    acc = h2
    for c in range(D_FF // chunk):
        a = jnp.maximum(_dot(n, w1_ref[:, c * chunk:(c + 1) * chunk]), 0.0)
        acc = acc + _dot((a * a).astype(BF16), w2_ref[c * chunk:(c + 1) * chunk, :])
    o_ref[0] = _rms(acc, gf_ref[...]) if final else acc


def _post(h, oa, ob, oc, wout, g, wq, mem, gmem, wkv, wo, g2, w1, w2, gf, *, l, tm, final):
    B, S, _ = h.shape
    tok = lambda b, i: (b, i, 0)
    const2 = lambda b, i: (0, 0)
    small = lambda a: pl.BlockSpec(a.shape, const2)
    resident = lambda a: _layer_spec(a, l, single_buffer=True)
    return pl.pallas_call(
        functools.partial(_post_kernel, final=final, chunk=D_MODEL),
        grid=(B, S // tm),
        in_specs=[pl.BlockSpec((1, tm, D_MODEL), tok)]
                 + [pl.BlockSpec((1, o.shape[1], tm), lambda b, i: (b, 0, i)) for o in (oa, ob, oc)] + [
                  resident(wout), _layer_spec(g, l), resident(wq),
                  pl.BlockSpec((1,) + mem.shape[1:], lambda b, i: (b, 0, 0)), _layer_spec(gmem, l), resident(wkv),
                  resident(wo), _layer_spec(g2, l), resident(w1), resident(w2), small(gf)],
        out_specs=pl.BlockSpec((1, tm, D_MODEL), tok),
        out_shape=jax.ShapeDtypeStruct(h.shape, F32),
        scratch_shapes=[pltpu.VMEM((mem.shape[1], 2 * D_MODEL), BF16)],
        compiler_params=pltpu.CompilerParams(dimension_semantics=("parallel", "arbitrary"),
                                             vmem_limit_bytes=VMEM_LIMIT),
        name="post",
    )(h, oa, ob, oc, wout, g, wq, mem, gmem, wkv, wo, g2, w1, w2, gf)


def _rot_pairs(w):
    half = w.shape[-1] // 2
    return jnp.concatenate([-w[..., half:], w[..., :half]], axis=-1)


def _pad_lanes(w, lo, width=LANES):
    pad = [(0, 0)] * (w.ndim - 1) + [(lo, width - lo - w.shape[-1])]
    return jnp.pad(w, pad)


def _head_groups(w, nheads, hd, lo=0, width=LANES):
    K = w.shape[0]
    return _pad_lanes(w.reshape(K, nheads, hd), lo, width).reshape(K, nheads * width)


def _widen_w_in(w):
    cuts = np.cumsum([0, MLA_Q_RANK, MLA_KV_RANK, MLA_ROPE, C_DIFF, C_DIFF, C_DIFF, C_MOBA, C_MOBA, C_MOBA])
    cq, ckv, kr, dq, dk, dv, mq, mk, mv = [w[:, int(a):int(b)] for a, b in zip(cuts[:-1], cuts[1:])]
    two_maps = lambda x: _head_groups(x, 2 * DIFF_HEADS, DIFF_HD, 0, HALF)
    out = jnp.concatenate([
        cq, ckv, _pad_lanes(kr, MLA_NOPE), _pad_lanes(_rot_pairs(kr), MLA_NOPE),
        dq, two_maps(dk), dv, mq, _head_groups(mk, MOBA_HEADS, MOBA_HD), mv], axis=1).astype(BF16)
    assert out.shape[1] == C_WIDE
    return out


def _widen_w_uq(w):
    K = w.shape[0]
    e = w.reshape(K, MLA_HEADS, MLA_NOPE + MLA_ROPE)
    nope = e[..., :MLA_NOPE].reshape(K, MLA_HEADS * MLA_NOPE)
    rope = _pad_lanes(e[..., MLA_NOPE:].reshape(K, MLA_HEADS * MLA_ROPE), 0, 2 * LANES)
    rot = _pad_lanes(_rot_pairs(e[..., MLA_NOPE:]).reshape(K, MLA_HEADS * MLA_ROPE), 0, 2 * LANES)
    out = jnp.concatenate([nope, rope, rot], axis=1).astype(BF16)
    assert out.shape[1] == UQ_COLS
    return out


def _widen_w_ukv(w):
    K = w.shape[0]
    e = w.reshape(K, MLA_HEADS, MLA_NOPE + MLA_V)
    kn = _pad_lanes(e[..., :MLA_NOPE], 0).reshape(K, MLA_HEADS * LANES)
    vv = e[..., MLA_NOPE:].reshape(K, MLA_HEADS * MLA_V)
    out = jnp.concatenate([kn, vv], axis=1).astype(BF16)
    assert out.shape[1] == UKV_COLS
    return out


def _diff_first_pairs(qnorm, knorm, rel, t, tk, group):
    B, nq = qnorm.shape[:2]
    nm = 2 * DIFF_HEADS
    qn = qnorm[:, :, 0, :nm]
    kn = jnp.max(knorm[:, :, 0, :nm], axis=1)
    spread = NORM_MARGIN * 2.0 * qn * kn[:, None, :]
    relf = rel[..., 0].astype(F32)
    dist = relf[:, ::t][:, :, None] - relf[:, tk - 1::tk][:, None, :]
    c = jnp.asarray([_alibi_slope(i // 2) * LOG2E for i in range(nm)], F32)
    dead = c[None, None, :, None] * dist[:, :, None, :] > spread[..., None] + UNDERFLOW_BITS
    tiles = jnp.sum(dead.astype(jnp.int32), axis=-1)
    assert group == 2
    per_head = jnp.min(tiles.reshape(B, nq, DIFF_HEADS // 2, 2, 2), axis=-1) // 2
    late = per_head[..., 0]
    early = jnp.minimum(per_head[..., 1], late)
    return jnp.stack([early, late], axis=-1).transpose(0, 2, 1, 3)


def kernel(x, mem, positions, attn_norm, w_in, mla_q_norm, mla_w_uq, mla_kv_norm, mla_w_ukv, diff_lambda_q1, diff_lambda_k1, diff_lambda_q2, diff_lambda_k2, diff_sub_norm, w_out, cross_norm, mem_norm, cross_wq, cross_wkv, cross_wo, mlp_norm, mlp_w1, mlp_w2, final_norm):
    B, S, _ = x.shape
    depth = w_in.shape[0]
    tm = 512
    t = 512
    tk = t // 2
    assert S % tm == 0 and tm == t and tk % MOBA_BLOCK == 0 and S // MOBA_BLOCK <= LANES - MOBA_SEL

    pos = positions.astype(jnp.int32)[:, None, :]
    rel = (positions - positions[:, :1]).astype(jnp.int32)[..., None]
    half = MLA_ROPE // 2
    inv_lane = (ROPE_THETA ** (-jnp.arange(half, dtype=F32) / half))[:, None]
    rows = lambda v: v.astype(F32)[:, None, :]
    wout_b, wq_b, wkv_b, wo_b, w1_b, w2_b = (w.astype(BF16) for w in (w_out, cross_wq, cross_wkv, cross_wo,
                                                                       mlp_w1, mlp_w2))
    w_wide, wuq, wukv = jax.vmap(_widen_w_in)(w_in), jax.vmap(_widen_w_uq)(mla_w_uq), jax.vmap(_widen_w_ukv)(mla_w_ukv)
    g_attn, g_q, g_kv, g_cross, g_mem, g_mlp = (rows(v) for v in (attn_norm, mla_q_norm, mla_kv_norm, cross_norm,
                                                                  mem_norm, mlp_norm))

    h = x
    for l in range(depth):
        outs = _inproj(h, pos, rel, inv_lane, g_attn, w_wide, g_q, wuq, g_kv, wukv, l=l, tm=tm, t=tk)
        qa, ka, va, qd, kd, vd, mq, mk, mv, kmean, qnorm, knorm = outs
        km = kmean.reshape(B, S // MOBA_BLOCK, MOBA_HEADS, LANES)[..., :MOBA_HD].transpose(0, 2, 1, 3)
        km_pad = jnp.pad(km, ((0, 0), (0, 0), (0, LANES - MOBA_SEL - S // MOBA_BLOCK), (0, 0)))
        mq_aug = _moba_gate(mq, km_pad, tm=math.gcd(S, GATE_TILE))

        attn = functools.partial(_group_attention, t=t, tk=tk)
        o_a = attn(_mla_attn_kernel, qa, ka, va, group=3, chains=3, name="mla_attn")
        lam_rows = jnp.stack([diff_lambda_q1[l], diff_lambda_k1[l], diff_lambda_q2[l], diff_lambda_k2[l]])
        lam_rows = jnp.pad(lam_rows.astype(F32), ((0, 4), (0, LANES - DIFF_HD)))
        gain = jnp.tile(diff_sub_norm[l].astype(F32), DIFF_GROUP)[:, None]
        lam_init = 0.8 - 0.6 * math.exp(-0.3 * l)
        o_b = attn(functools.partial(_diff_attn_kernel, lam_init=lam_init), qd, kd, vd,
                   extra=(lam_rows, gain), first_pair=_diff_first_pairs(qnorm, knorm, rel, t, tk, DIFF_GROUP),
                   group=DIFF_GROUP, chains=2 * DIFF_GROUP, name="diff_attn")
        o_c = attn(_mla_attn_kernel, mq_aug, mk, mv, group=4, chains=4, name="moba_attn")

        h = _post(h, o_a, o_b, o_c, wout_b, g_cross, wq_b, mem, g_mem, wkv_b, wo_b, g_mlp, w1_b, w2_b,
                  final_norm.astype(F32)[None, :], l=l, tm=tm, final=(l == depth - 1))
    return h
```
